```python
import math
import jax, jax.numpy as jnp
from jax import lax
import numpy as np

D_MODEL = 1024
BATCH = 16
SEQ = 2048
DEPTH = 2

CHUNK = 64
N_A = DEPTH // 2
N_B = DEPTH - N_A
N_DENSE = (DEPTH + 1) // 2
N_MOE = DEPTH // 2
GMLP_CHUNK = 128
A_FFN = 6 * D_MODEL
A_HALF = A_FFN // 2
A_GROUPS = 8
A_GROUP_DIM = A_HALF // A_GROUPS
B_HEAD_DIM = 64
B_HEADS = D_MODEL // B_HEAD_DIM
Q_BLOCK = 128
D_FF = ((8 * D_MODEL // 3 + 255) // 256) * 256
N_EXPERTS = 8
TOP_K = 2
D_EXPERT = 7 * D_MODEL // 2
RMS_EPS = 1e-6
NEG_INF = -1e30

kernel_name = "yoco_gmlp_fox_moe_trunk"


def rms_norm(x, g):
    xf = x.astype(jnp.float32)
    y = xf * lax.rsqrt(jnp.mean(xf * xf, axis=-1, keepdims=True) + RMS_EPS)
    return (y * g.astype(jnp.float32)).astype(x.dtype)


def swiglu(x, w_in, w_out):
    a, b = jnp.split(x @ w_in, 2, axis=-1)
    return (jax.nn.silu(a) * b) @ w_out


def mixer_a(x, w_in, v_norm_g, w_spatial, b_spatial, w_out):
    B, S, _ = x.shape
    uv = jax.nn.gelu(x @ w_in, approximate=False)
    u, v = jnp.split(uv, 2, axis=-1)
    v = rms_norm(v, v_norm_g)
    v = v.reshape(B, S // GMLP_CHUNK, GMLP_CHUNK, A_GROUPS, A_GROUP_DIM)
    pos = jnp.arange(GMLP_CHUNK)
    mask = (pos[None, :] // CHUNK) <= (pos[:, None] // CHUNK)
    w = jnp.where(mask[None], w_spatial, jnp.zeros_like(w_spatial))
    v = jnp.einsum('gij,bcjgd->bcigd', w, v) + b_spatial.T[None, None, :, :, None]
    return (u * v.reshape(B, S, A_HALF)) @ w_out


def shared_kv(h, norm_g, w_kvf, b_f, k_norm_g):
    B, S, _ = h.shape
    s = rms_norm(h, norm_g)
    kvf = s @ w_kvf
    k = kvf[..., :D_MODEL].reshape(B, S, B_HEADS, B_HEAD_DIM)
    v = kvf[..., D_MODEL:2 * D_MODEL].reshape(B, S, B_HEADS, B_HEAD_DIM)
    f = kvf[..., 2 * D_MODEL:]
    k = rms_norm(k, k_norm_g).transpose(0, 2, 1, 3)
    v = v.transpose(0, 2, 1, 3)
    log_f = jax.nn.log_sigmoid((f + b_f).astype(jnp.float32))
    cum_log_f = jnp.cumsum(log_f, axis=1).transpose(0, 2, 1)
    return k, v, cum_log_f


def mixer_b(x, k, v, cum_log_f, w_in, q_norm_g, w_out):
    B, S, _ = x.shape
    qg = x @ w_in
    q, gate = jnp.split(qg, 2, axis=-1)
    q = rms_norm(q.reshape(B, S, B_HEADS, B_HEAD_DIM), q_norm_g).transpose(0, 2, 1, 3)
    scale = B_HEAD_DIM ** -0.5
    outs = []
    for blk in range(S // Q_BLOCK):
        qs, qe = blk * Q_BLOCK, (blk + 1) * Q_BLOCK
        logits = jnp.einsum('bhqd,bhkd->bhqk', q[:, :, qs:qe], k[:, :, :qe]).astype(jnp.float32) * scale
        logits = logits + cum_log_f[:, :, qs:qe, None] - cum_log_f[:, :, None, :qe]
        causal = jnp.arange(qs, qe)[:, None] >= jnp.arange(qe)[None, :]
        logits = jnp.where(causal[None, None], logits, NEG_INF)
        p = jax.nn.softmax(logits, axis=-1).astype(v.dtype)
        outs.append(jnp.einsum('bhqk,bhkd->bhqd', p, v[:, :, :qe]))
    o = jnp.concatenate(outs, axis=2).transpose(0, 2, 1, 3).reshape(B, S, D_MODEL)
    o = o * jax.nn.sigmoid(gate)
    return o @ w_out


def moe_swiglu(x, w_router, w_in, w_out):
    B, S, D = x.shape
    xt = x.reshape(B * S, D)
    logits = (xt @ w_router).astype(jnp.float32)
    top_v, top_i = lax.top_k(logits, TOP_K)
    top_w = jax.nn.softmax(top_v, axis=-1)
    gates = jnp.sum(jax.nn.one_hot(top_i, N_EXPERTS, dtype=jnp.float32) * top_w[..., None], axis=1)
    y = jnp.zeros_like(xt)
    for e in range(N_EXPERTS):
        y = y + gates[:, e:e + 1].astype(xt.dtype) * swiglu(xt, w_in[e], w_out[e])
    return y.reshape(B, S, D)


def setup_inputs(seed: int = 0) -> dict:
    key = jax.random.key(seed)
    ks = jax.random.split(key, 24)

    def dense(k, shape, fan_in, s=1.0):
        return jax.random.normal(k, shape, jnp.float32) * (s * fan_in ** -0.5)

    def gain(k, shape):
        return 1.0 + 0.02 * jax.random.normal(k, shape, jnp.float32)

    return {
        "x": jax.random.normal(ks[0], (BATCH, SEQ, D_MODEL), jnp.float32),
        "a_norm_g": gain(ks[1], (N_A, D_MODEL)),
        "a_w_in": dense(ks[2], (N_A, D_MODEL, A_FFN), D_MODEL),
        "a_v_norm_g": gain(ks[3], (N_A, A_HALF)),
        "a_w_spatial": dense(ks[4], (N_A, A_GROUPS, GMLP_CHUNK, GMLP_CHUNK), GMLP_CHUNK, 0.5),
        "a_b_spatial": 1.0 + 0.1 * jax.random.normal(ks[5], (N_A, A_GROUPS, GMLP_CHUNK), jnp.float32),
        "a_w_out": dense(ks[6], (N_A, A_HALF, D_MODEL), A_HALF),
        "f_norm_g": gain(ks[7], (N_DENSE, D_MODEL)),
        "f_w_in": dense(ks[8], (N_DENSE, D_MODEL, 2 * D_FF), D_MODEL),
        "f_w_out": dense(ks[9], (N_DENSE, D_FF, D_MODEL), D_FF),
        "kv_norm_g": gain(ks[10], (D_MODEL,)),
        "kv_w": dense(ks[11], (D_MODEL, 2 * D_MODEL + B_HEADS), D_MODEL),
        "kv_b_f": 4.0 + 0.1 * jax.random.normal(ks[12], (B_HEADS,), jnp.float32),
        "k_norm_g": gain(ks[13], (B_HEAD_DIM,)),
        "b_norm_g": gain(ks[14], (N_B, D_MODEL)),
        "b_w_in": dense(ks[15], (N_B, D_MODEL, 2 * D_MODEL), D_MODEL),
        "q_norm_g": gain(ks[16], (N_B, B_HEAD_DIM)),
        "b_w_out": dense(ks[17], (N_B, D_MODEL, D_MODEL), D_MODEL),
        "m_norm_g": gain(ks[18], (N_MOE, D_MODEL)),
        "m_w_router": dense(ks[19], (N_MOE, D_MODEL, N_EXPERTS), D_MODEL),
        "m_w_in": dense(ks[20], (N_MOE, N_EXPERTS, D_MODEL, 2 * D_EXPERT), D_MODEL),
        "m_w_out": dense(ks[21], (N_MOE, N_EXPERTS, D_EXPERT, D_MODEL), D_EXPERT),
    }


def reference(x, a_norm_g, a_w_in, a_v_norm_g, a_w_spatial, a_b_spatial, a_w_out,
              f_norm_g, f_w_in, f_w_out,
              kv_norm_g, kv_w, kv_b_f, k_norm_g,
              b_norm_g, b_w_in, q_norm_g, b_w_out,
              m_norm_g, m_w_router, m_w_in, m_w_out):
    h = x
    k = v = cum_log_f = None
    for layer in range(DEPTH):
        if layer < N_A:
            i = layer
            h = h + mixer_a(rms_norm(h, a_norm_g[i]), a_w_in[i], a_v_norm_g[i],
                            a_w_spatial[i], a_b_spatial[i], a_w_out[i])
        else:
            if layer == N_A:
                k, v, cum_log_f = shared_kv(h, kv_norm_g, kv_w, kv_b_f, k_norm_g)
            i = layer - N_A
            h = h + mixer_b(rms_norm(h, b_norm_g[i]), k, v, cum_log_f,
                            b_w_in[i], q_norm_g[i], b_w_out[i])
        if layer % 2 == 0:
            j = layer // 2
            h = h + swiglu(rms_norm(h, f_norm_g[j]), f_w_in[j], f_w_out[j])
        else:
            j = layer // 2
            h = h + moe_swiglu(rms_norm(h, m_norm_g[j]), m_w_router[j], m_w_in[j], m_w_out[j])
    return h
```

```python
import functools

import jax
import jax.numpy as jnp
from jax import lax
from jax.experimental import pallas as pl
from jax.experimental.pallas import tpu as pltpu

RMS_EPS = 1e-6
NEG_INF = -1e30
GMLP_CHUNK = 128
CAUSAL_CHUNK = 64
A_GROUPS = 8
HEAD_DIM = 64
LANES = 128
TOP_K = 2
VMEM_LIMIT = 56 * 1024 * 1024

BF16 = jnp.bfloat16
F32 = jnp.float32


def _dot(a, b, **kw):
    return jnp.dot(a, b, preferred_element_type=F32, **kw)


def _dot_nt(a, b, **kw):
    return lax.dot_general(a, b, (((1,), (1,)), ((), ())), preferred_element_type=F32, **kw)


def _rms(x, g):
    return x * lax.rsqrt(jnp.mean(x * x, axis=-1, keepdims=True) + RMS_EPS) * g


def _gelu(x):
    return 0.5 * x * (1.0 + lax.erf(x * (2.0 ** -0.5)))


def _const_spec(shape):
    nd = len(shape)
    return pl.BlockSpec(shape, lambda *_: (0,) * nd, pipeline_mode=pl.Buffered(1))


def _params(sem):
    return pltpu.CompilerParams(dimension_semantics=sem, vmem_limit_bytes=VMEM_LIMIT)


def _mixer_a_kernel(x_ref, g_ref, win_ref, gv_ref, ws_ref, bs_ref, wout_ref, o_ref, z_ref):
    tm = x_ref.shape[0]
    half = wout_ref.shape[0]
    gd = half // A_GROUPS
    x = x_ref[...]
    xb = _rms(x, g_ref[...]).astype(BF16)
    v = _gelu(_dot(xb, win_ref[:, half:]))
    v = (_rms(v, gv_ref[...])).astype(BF16)
    u = _gelu(_dot(xb, win_ref[:, :half]))
    row = lax.broadcasted_iota(jnp.int32, (GMLP_CHUNK, GMLP_CHUNK), 0)
    col = lax.broadcasted_iota(jnp.int32, (GMLP_CHUNK, GMLP_CHUNK), 1)
    keep = (col // CAUSAL_CHUNK) <= (row // CAUSAL_CHUNK)
    bs = bs_ref[...]
    for g in range(A_GROUPS):
        wg = jnp.where(keep, ws_ref[g], 0.0).astype(BF16)
        bg = bs[:, g:g + 1]
        for c in range(tm // GMLP_CHUNK):
            rs = slice(c * GMLP_CHUNK, (c + 1) * GMLP_CHUNK)
            cs = slice(g * gd, (g + 1) * gd)
            sv = _dot(wg, v[rs, cs]) + bg
            z_ref[rs, cs] = (u[rs, cs] * sv).astype(BF16)
    o_ref[...] = x + _dot(z_ref[...], wout_ref[...])


def _mixer_a(h, g, w_in, gv, ws, bs_t, w_out, *, tm):
    n, d = h.shape
    half = w_out.shape[0]
    return pl.pallas_call(
        _mixer_a_kernel,
        out_shape=jax.ShapeDtypeStruct((n, d), F32),
        grid=(n // tm,),
        in_specs=[
            pl.BlockSpec((tm, d), lambda i: (i, 0)),
            _const_spec(g.shape), _const_spec(w_in.shape), _const_spec(gv.shape),
            _const_spec(ws.shape), _const_spec(bs_t.shape), _const_spec(w_out.shape),
        ],
        out_specs=pl.BlockSpec((tm, d), lambda i: (i, 0)),
        scratch_shapes=[pltpu.VMEM((tm, half), BF16)],
        compiler_params=_params(("parallel",)),
        name="mixer_a",
    )(h, g, w_in, gv, ws, bs_t, w_out)


def _swiglu_kernel(x_ref, g_ref, win_ref, wout_ref, o_ref):
    f = wout_ref.shape[0]
    x = x_ref[...]
    xb = _rms(x, g_ref[...]).astype(BF16)
    a = _dot(xb, win_ref[:, :f])
    b = _dot(xb, win_ref[:, f:])
    hm = (a * jax.nn.sigmoid(a) * b).astype(BF16)
    o_ref[...] = x + _dot(hm, wout_ref[...])


def _swiglu(h, g, w_in, w_out, *, tm):
    n, d = h.shape
    return pl.pallas_call(
        _swiglu_kernel,
        out_shape=jax.ShapeDtypeStruct((n, d), F32),
        grid=(n // tm,),
        in_specs=[
            pl.BlockSpec((tm, d), lambda i: (i, 0)),
            _const_spec(g.shape), _const_spec(w_in.shape), _const_spec(w_out.shape),
        ],
        out_specs=pl.BlockSpec((tm, d), lambda i: (i, 0)),
        compiler_params=_params(("parallel",)),
        name="swiglu",
    )(h, g, w_in, w_out)


def _lane_cumsum(x):
    n = x.shape[-1]
    lane = lax.broadcasted_iota(jnp.int32, x.shape, x.ndim - 1)
    sh = 1
    while sh < n:
        x = x + jnp.where(lane >= sh, pltpu.roll(x, sh, axis=x.ndim - 1), 0.0)
        sh *= 2
    return x


def _qkv_kernel(x_ref, gkv_ref, gb_ref, wkv_ref, wft_ref, bf_ref, gk_ref, gq_ref, wqg_ref,
                hsum_ref, q_ref, k_ref, v_ref, sg_ref, ft_ref, carry_ref, *, tiles_per_seq):
    d = x_ref.shape[1]
    i = pl.program_id(0)
    x = x_ref[...]
    y = x * lax.rsqrt(jnp.mean(x * x, axis=-1, keepdims=True) + RMS_EPS)
    skv = (y * gkv_ref[...]).astype(BF16)
    sb = (y * gb_ref[...]).astype(BF16)
    hsum = hsum_ref[...]

    def head_norm(t, gain):
        ssq = _dot((t * t).astype(BF16), hsum)
        return t * lax.rsqrt(ssq * (1.0 / HEAD_DIM) + RMS_EPS) * gain

    k = _dot(skv, wkv_ref[:, :d])
    k_ref[...] = head_norm(k, gk_ref[...]).astype(BF16)
    v_ref[...] = _dot(skv, wkv_ref[:, d:]).astype(BF16)
    q = _dot(sb, wqg_ref[:, :d])
    q_ref[...] = (head_norm(q, gq_ref[...]) * (HEAD_DIM ** -0.5)).astype(BF16)
    sg_ref[...] = jax.nn.sigmoid(_dot(sb, wqg_ref[:, d:])).astype(BF16)

    f = _dot_nt(wft_ref[...], skv) + bf_ref[...]
    logf = jax.nn.log_sigmoid(f)

    @pl.when(i % tiles_per_seq == 0)
    def _():
        carry_ref[...] = jnp.zeros_like(carry_ref)

    cum = _lane_cumsum(logf) + carry_ref[:, 0:1]
    ft_ref[0] = cum
    carry_ref[...] = jnp.broadcast_to(cum[:, -1:], carry_ref.shape)


def _qkv(h, gkv, gb, wkv, wft, bf, gk, gq, wqg, hsum, *, tm, batch, seq):
    n, d = h.shape
    nh = wft.shape[0]
    tps = seq // tm
    tok = pl.BlockSpec((tm, d), lambda i: (i, 0))
    return pl.pallas_call(
        functools.partial(_qkv_kernel, tiles_per_seq=tps),
        out_shape=[jax.ShapeDtypeStruct((n, d), BF16)] * 4
        + [jax.ShapeDtypeStruct((batch, nh, seq), F32)],
        grid=(n // tm,),
        in_specs=[tok] + [_const_spec(a.shape) for a in (gkv, gb, wkv, wft, bf, gk, gq, wqg, hsum)],
        out_specs=[tok] * 4 + [pl.BlockSpec((1, nh, tm), lambda i: (i // tps, 0, i % tps))],
        scratch_shapes=[pltpu.VMEM((nh, LANES), F32)],
        compiler_params=_params(("arbitrary",)),
        name="qkv",
    )(h, gkv, gb, wkv, wft, bf, gk, gq, wqg, hsum)


def _attention_kernel(q_ref, k_ref, v_ref, ft_ref, fq_ref, sg_ref, o_ref, *, blk):
    hp = pl.program_id(1)
    qi = pl.program_id(2)
    nh = fq_ref.shape[2]
    lane = lax.broadcasted_iota(jnp.int32, (1, LANES), 1)
    first = lane < HEAD_DIM
    q = q_ref[...]
    zero = jnp.zeros_like(q)
    qs = (jnp.where(first, q, zero), jnp.where(first, zero, q))
    er = lax.broadcasted_iota(jnp.int32, (nh, LANES), 0)
    ec = lax.broadcasted_iota(jnp.int32, (nh, LANES), 1)
    pick = (er == 2 * hp + (ec >= HEAD_DIM).astype(jnp.int32)).astype(F32)
    fq = _dot(fq_ref[0], pick, precision=lax.Precision.HIGHEST)
    fqs = (fq[:, 0:1], fq[:, HEAD_DIM:HEAD_DIM + 1])

    def step(kb, carry, masked):
        k0 = pl.multiple_of(kb * blk, blk)
        k = k_ref[pl.ds(k0, blk), :]
        v = v_ref[pl.ds(k0, blk), :]
        fk = ft_ref[0, 0, :, pl.ds(k0, blk)]
        new = []
        outs = []
        for h in range(2):
            m_prev, l_prev = carry[2 * h], carry[2 * h + 1]
            s = (_dot_nt(qs[h], k) + fqs[h]) - fk[h:h + 1, :]
            if masked:
                r = lax.broadcasted_iota(jnp.int32, s.shape, 0)
                c = lax.broadcasted_iota(jnp.int32, s.shape, 1)
                s = jnp.where(r >= c, s, NEG_INF)
            m_new = jnp.maximum(m_prev, jnp.max(s, axis=-1, keepdims=True))
            alpha = jnp.exp(m_prev - m_new)
            p = jnp.exp(s - m_new)
            l_new = alpha * l_prev + jnp.sum(p, axis=-1, keepdims=True)
            outs.append((alpha, _dot(p.astype(BF16), v)))
            new += [m_new, l_new]
        acc = carry[4]
        alpha = jnp.where(first, outs[0][0], outs[1][0])
        acc = alpha * acc + jnp.where(first, outs[0][1], outs[1][1])
        return tuple(new) + (acc,)

    init = (jnp.full((blk, 1), NEG_INF, F32), jnp.zeros((blk, 1), F32)) * 2 + (
        jnp.zeros((blk, LANES), F32),)
    carry = lax.fori_loop(0, qi, lambda kb, c: step(kb, c, False), init)
    carry = step(qi, carry, True)
    l = jnp.where(first, carry[1], carry[3])
    o_ref[...] = (carry[4] / l * sg_ref[...].astype(F32)).astype(BF16)


def _attention(q, k, v, ft, fq, sg, *, batch, seq, blk):
    n, d = q.shape
    nh = fq.shape[2]
    nq = seq // blk
    qspec = pl.BlockSpec((blk, LANES), lambda b, hp, qi: (b * nq + qi, hp))
    kvspec = pl.BlockSpec((seq, LANES), lambda b, hp, qi: (b, hp))
    return pl.pallas_call(
        functools.partial(_attention_kernel, blk=blk),
        out_shape=jax.ShapeDtypeStruct((n, d), BF16),
        grid=(batch, d // LANES, nq),
        in_specs=[
            qspec, kvspec, kvspec,
            pl.BlockSpec((1, 1, 2, seq), lambda b, hp, qi: (b, hp, 0, 0)),
            pl.BlockSpec((1, blk, nh), lambda b, hp, qi: (b, qi, 0)),
            qspec,
        ],
        out_specs=qspec,
        compiler_params=_params(("parallel", "parallel", "arbitrary")),
        name="attention",
    )(q, k, v, ft, fq, sg)


def _out_router_kernel(og_ref, h_ref, wo_ref, gm_ref, wrt_ref, h2_ref, xn_ref, meta_ref, cnt_ref,
                       carry_ref):
    i = pl.program_id(0)
    ne, tm = meta_ref.shape
    h2 = h_ref[...] + _dot(og_ref[...], wo_ref[...])
    h2_ref[...] = h2
    xn = _rms(h2, gm_ref[...])
    xn_ref[...] = xn
    logits = _dot_nt(wrt_ref[...], xn, precision=lax.Precision.HIGHEST)
    row = lax.broadcasted_iota(jnp.int32, (ne, tm), 0).astype(F32)
    v1 = jnp.max(logits, axis=0, keepdims=True)
    i1 = jnp.min(jnp.where(logits == v1, row, ne), axis=0, keepdims=True)
    rest = jnp.where(row == i1, -jnp.inf, logits)
    v2 = jnp.max(rest, axis=0, keepdims=True)
    i2 = jnp.min(jnp.where(rest == v2, row, ne), axis=0, keepdims=True)
    e = jnp.exp(v2 - v1)
    w1 = 1.0 / (1.0 + e)
    w2 = e / (1.0 + e)
    sel1 = row == i1
    sel2 = row == i2
    oh = (sel1 | sel2).astype(F32)

    @pl.when(i == 0)
    def _():
        carry_ref[...] = jnp.zeros_like(carry_ref)

    base = carry_ref[:, 0:1]
    rank = base + (_lane_cumsum(oh) - oh)
    r1 = jnp.sum(jnp.where(sel1, rank, 0.0), axis=0, keepdims=True)
    r2 = jnp.sum(jnp.where(sel2, rank, 0.0), axis=0, keepdims=True)
    total = base + jnp.sum(oh, axis=1, keepdims=True)
    carry_ref[...] = jnp.broadcast_to(total, carry_ref.shape)
    cnt_ref[...] = jnp.broadcast_to(total, cnt_ref.shape)
    zeros = jnp.zeros_like(w1)
    meta_ref[...] = jnp.concatenate(
        [i1.astype(F32), i2.astype(F32), w1, w2, r1, r2, zeros, zeros], axis=0)


def _out_router(og, h, wo, gm, wrt, *, tm):
    n, d = h.shape
    ne = wrt.shape[0]
    tok = pl.BlockSpec((tm, d), lambda i: (i, 0))
    return pl.pallas_call(
        _out_router_kernel,
        out_shape=[jax.ShapeDtypeStruct((n, d), F32), jax.ShapeDtypeStruct((n, d), F32),
                   jax.ShapeDtypeStruct((ne, n), F32), jax.ShapeDtypeStruct((ne, LANES), F32)],
        grid=(n // tm,),
        in_specs=[tok, tok, _const_spec(wo.shape), _const_spec(gm.shape), _const_spec(wrt.shape)],
        out_specs=[tok, tok, pl.BlockSpec((ne, tm), lambda i: (0, i)),
                   pl.BlockSpec((ne, LANES), lambda i: (0, 0))],
        scratch_shapes=[pltpu.VMEM((ne, LANES), F32)],
        compiler_params=_params(("arbitrary",)),
        name="out_router",
    )(og, h, wo, gm, wrt)


def _row_copy(src_ref, src_row, dst_ref, dst_row, sem):
    return pltpu.make_async_copy(src_ref.at[pl.ds(src_row, 1)], dst_ref.at[pl.ds(dst_row, 1)], sem)


def _dispatch_kernel(dest_ref, xn_ref, init_ref, xs_ref, sem):
    del init_ref
    td = dest_ref.shape[1]
    base = pl.program_id(0) * td

    def issue(t, _):
        for s in range(TOP_K):
            _row_copy(xn_ref, base + t, xs_ref, dest_ref[s, t], sem).start()
        return 0

    lax.fori_loop(0, td, issue, 0)

    def drain(t, _):
        for s in range(TOP_K):
            _row_copy(xn_ref, base + t, xs_ref, dest_ref[s, t], sem).wait()
        return 0

    lax.fori_loop(0, td, drain, 0)


def _dispatch(dest, xn, xs_init, *, td):
    n, d = xn.shape
    return pl.pallas_call(
        _dispatch_kernel,
        out_shape=jax.ShapeDtypeStruct(xs_init.shape, xs_init.dtype),
        grid=(n // td,),
        in_specs=[pl.BlockSpec((TOP_K, td), lambda i: (0, i), memory_space=pltpu.SMEM),
                  pl.BlockSpec(memory_space=pl.ANY), pl.BlockSpec(memory_space=pl.ANY)],
        out_specs=pl.BlockSpec(memory_space=pl.ANY),
        scratch_shapes=[pltpu.SemaphoreType.DMA],
        input_output_aliases={2: 0},
        compiler_params=_params(("arbitrary",)),
        name="dispatch",
    )(dest, xn, xs_init)


def _experts_kernel(te_ref, nu_ref, x_ref, wa_ref, wb_ref, wo_ref, o_ref, xb_ref, acc_ref):
    del te_ref
    i = pl.program_id(0)
    f = pl.program_id(1)

    @pl.when(i < nu_ref[0])
    def _():
        @pl.when(f == 0)
        def _():
            xb_ref[...] = x_ref[...].astype(BF16)
            acc_ref[...] = jnp.zeros_like(acc_ref)

        xb = xb_ref[...]
        a = _dot(xb, wa_ref[0])
        b = _dot(xb, wb_ref[0])
        hm = (a * jax.nn.sigmoid(a) * b).astype(BF16)
        acc_ref[...] += _dot(hm, wo_ref[0])

        @pl.when(f == pl.num_programs(1) - 1)
        def _():
            o_ref[...] = acc_ref[...]

    @pl.when((i >= nu_ref[0]) & (f == 0))
    def _():
        o_ref[...] = jnp.zeros_like(o_ref)


def _experts(tile_expert, n_used, xs, w_in, w_out, *, tr, tf):
    p, d = xs.shape
    de = w_out.shape[1]
    nf = de // tf

    def row_map(i, f, te, nu):
        return (jnp.minimum(i, nu[0] - 1), 0)

    return pl.pallas_call(
        _experts_kernel,
        out_shape=jax.ShapeDtypeStruct((p, d), F32),
        grid_spec=pltpu.PrefetchScalarGridSpec(
            num_scalar_prefetch=2,
            grid=(p // tr, nf),
            in_specs=[
                pl.BlockSpec((tr, d), row_map),
                pl.BlockSpec((1, d, tf), lambda i, f, te, nu: (te[i], 0, f)),
                pl.BlockSpec((1, d, tf), lambda i, f, te, nu: (te[i], 0, nf + f)),
                pl.BlockSpec((1, tf, d), lambda i, f, te, nu: (te[i], f, 0)),
            ],
            out_specs=pl.BlockSpec((tr, d), lambda i, f, te, nu: (i, 0)),
            scratch_shapes=[pltpu.VMEM((tr, d), BF16), pltpu.VMEM((tr, d), F32)],
        ),
        compiler_params=_params(("arbitrary", "arbitrary")),
        name="experts",
    )(tile_expert, n_used, xs, w_in, w_in, w_out)


def _combine_kernel(dest_ref, eo_ref, h_ref, w_ref, o_ref, buf_ref, sem):
    tc = h_ref.shape[0]

    def issue(t, _):
        for s in range(TOP_K):
            _row_copy(eo_ref, dest_ref[s, t], buf_ref.at[s], t, sem).start()
        return 0

    lax.fori_loop(0, tc, issue, 0)

    def drain(t, _):
        for s in range(TOP_K):
            _row_copy(eo_ref, dest_ref[s, t], buf_ref.at[s], t, sem).wait()
        return 0

    lax.fori_loop(0, tc, drain, 0)
    w = w_ref[...]
    o_ref[...] = h_ref[...] + (w[:, 0:1] * buf_ref[0] + w[:, 1:2] * buf_ref[1])


def _combine(dest, eo, h, w, *, tc):
    n, d = h.shape
    tok = pl.BlockSpec((tc, d), lambda i: (i, 0))
    return pl.pallas_call(
        _combine_kernel,
        out_shape=jax.ShapeDtypeStruct((n, d), F32),
        grid=(n // tc,),
        in_specs=[pl.BlockSpec((TOP_K, tc), lambda i: (0, i), memory_space=pltpu.SMEM),
                  pl.BlockSpec(memory_space=pl.ANY), tok,
                  pl.BlockSpec((tc, TOP_K), lambda i: (i, 0))],
        out_specs=tok,
        scratch_shapes=[pltpu.VMEM((TOP_K, tc, d), F32), pltpu.SemaphoreType.DMA],
        compiler_params=_params(("arbitrary",)),
        name="combine",
    )(dest, eo, h, w)


def _tiles(n, seq):
    def pick(limit, of):
        t = limit
        while of % t:
            t //= 2
        return t
    return dict(
        tm_a=pick(256, seq), tm_f=pick(512, n), tm_qkv=pick(512, seq), blk=pick(256, seq),
        tm_o=pick(512, n), td=pick(512, n), tr=pick(1024, TOP_K * n), tc=pick(256, n))


def kernel(x, a_norm_g, a_w_in, a_v_norm_g, a_w_spatial, a_b_spatial, a_w_out, f_norm_g, f_w_in, f_w_out, kv_norm_g, kv_w, kv_b_f, k_norm_g, b_norm_g, b_w_in, q_norm_g, b_w_out, m_norm_g, m_w_router, m_w_in, m_w_out):
    batch, seq, d = x.shape
    n = batch * seq
    nh = d // HEAD_DIM
    ne = m_w_router.shape[-1]
    assert a_w_in.shape[0] == 1 and b_w_in.shape[0] == 1 and f_w_in.shape[0] == 1 and m_w_in.shape[0] == 1
    assert seq % GMLP_CHUNK == 0 and d % LANES == 0
    t = _tiles(n, seq)
    row = lambda g: g.reshape(1, -1)

    h = x.reshape(n, d)
    h = _mixer_a(h, row(a_norm_g[0]), a_w_in[0].astype(BF16), row(a_v_norm_g[0]), a_w_spatial[0],
                 a_b_spatial[0].T, a_w_out[0].astype(BF16), tm=t["tm_a"])
    h = _swiglu(h, row(f_norm_g[0]), f_w_in[0].astype(BF16), f_w_out[0].astype(BF16), tm=t["tm_f"])

    head = jnp.arange(d, dtype=jnp.int32) // HEAD_DIM
    hsum = (head[:, None] == head[None, :]).astype(BF16)
    q, k, v, sg, ft = _qkv(
        h, row(kv_norm_g), row(b_norm_g[0]), kv_w[:, :2 * d].astype(BF16),
        kv_w[:, 2 * d:].T.astype(BF16), kv_b_f.reshape(nh, 1),
        row(jnp.tile(k_norm_g, nh)), row(jnp.tile(q_norm_g[0], nh)), b_w_in[0].astype(BF16), hsum,
        tm=t["tm_qkv"], batch=batch, seq=seq)
    og = _attention(q, k, v, ft.reshape(batch, nh // 2, 2, seq), jnp.swapaxes(ft, 1, 2), sg,
                    batch=batch, seq=seq, blk=t["blk"])

    h2, xn, meta, cnt = _out_router(og, h, b_w_out[0].astype(BF16), row(m_norm_g[0]),
                                    m_w_router[0].T, tm=t["tm_o"])

    tr = t["tr"]
    n_tiles = TOP_K * n // tr + ne
    counts = cnt[:, 0].astype(jnp.int32)
    tiles_per_expert = (counts + tr - 1) // tr
    tile_end = jnp.cumsum(tiles_per_expert)
    start = (tile_end - tiles_per_expert) * tr
    idx = meta[0:TOP_K].astype(jnp.int32)
    dest = start[idx] + meta[4:4 + TOP_K].astype(jnp.int32)
    tile_expert = jnp.minimum(
        jnp.searchsorted(tile_end, jnp.arange(n_tiles, dtype=jnp.int32), side="right"),
        ne - 1).astype(jnp.int32)
    n_used = tile_end[-1:].astype(jnp.int32)

    xs = _dispatch(dest, xn, jnp.zeros((n_tiles * tr, d), F32), td=t["td"])
    eo = _experts(tile_expert, n_used, xs, m_w_in[0].astype(BF16), m_w_out[0].astype(BF16),
                  tr=tr, tf=m_w_out.shape[2] // 4)
    out = _combine(dest, eo, h2, meta[2:2 + TOP_K].T, tc=t["tc"])
    return out.reshape(batch, seq, d)
```

```python
import functools

import jax
import jax.numpy as jnp
from jax import lax
from jax.experimental import pallas as pl
from jax.experimental.pallas import tpu as pltpu

RMS_EPS = 1e-6
NEG_INF = -1e30
LOG2E = 1.4426950408889634
N_PIECES = 3
GMLP_CHUNK = 128
CAUSAL_CHUNK = 64
A_GROUPS = 8
HEAD_DIM = 64
LANES = 128
TOP_K = 2
VMEM_LIMIT = 56 * 1024 * 1024

BF16 = jnp.bfloat16
F32 = jnp.float32


def _dot(a, b, **kw):
    return jnp.dot(a, b, preferred_element_type=F32, **kw)


def _dot_nt(a, b, **kw):
    return lax.dot_general(a, b, (((1,), (1,)), ((), ())), preferred_element_type=F32, **kw)


def _rms(x, g):
    return x * lax.rsqrt(jnp.mean(x * x, axis=-1, keepdims=True) + RMS_EPS) * g


def _gelu(x):
    return 0.5 * x * (1.0 + lax.erf(x * (2.0 ** -0.5)))


def _const_spec(shape):
    nd = len(shape)
    return pl.BlockSpec(shape, lambda *_: (0,) * nd, pipeline_mode=pl.Buffered(1))


def _params(sem):
    return pltpu.CompilerParams(dimension_semantics=sem, vmem_limit_bytes=VMEM_LIMIT)


def _mixer_a_kernel(x_ref, g_ref, win_ref, gv_ref, ws_ref, bs_ref, wout_ref, o_ref, z_ref):
    tm = x_ref.shape[0]
    half = wout_ref.shape[0]
    gd = half // A_GROUPS
    x = x_ref[...]
    xb = _rms(x, g_ref[...]).astype(BF16)
    v = _gelu(_dot(xb, win_ref[:, half:]))
    v = (_rms(v, gv_ref[...])).astype(BF16)
    u = _gelu(_dot(xb, win_ref[:, :half]))
    row = lax.broadcasted_iota(jnp.int32, (GMLP_CHUNK, GMLP_CHUNK), 0)
    col = lax.broadcasted_iota(jnp.int32, (GMLP_CHUNK, GMLP_CHUNK), 1)
    keep = (col // CAUSAL_CHUNK) <= (row // CAUSAL_CHUNK)
    bs = bs_ref[...]
    for g in range(A_GROUPS):
        wg = jnp.where(keep, ws_ref[g], 0.0).astype(BF16)
        bg = bs[:, g:g + 1]
        for c in range(tm // GMLP_CHUNK):
            rs = slice(c * GMLP_CHUNK, (c + 1) * GMLP_CHUNK)
            cs = slice(g * gd, (g + 1) * gd)
            sv = _dot(wg, v[rs, cs]) + bg
            z_ref[rs, cs] = (u[rs, cs] * sv).astype(BF16)
    o_ref[...] = x + _dot(z_ref[...], wout_ref[...])


def _mixer_a(h, g, w_in, gv, ws, bs_t, w_out, *, tm):
    n, d = h.shape
    half = w_out.shape[0]
    return pl.pallas_call(
        _mixer_a_kernel,
        out_shape=jax.ShapeDtypeStruct((n, d), F32),
        grid=(n // tm,),
        in_specs=[
            pl.BlockSpec((tm, d), lambda i: (i, 0)),
            _const_spec(g.shape), _const_spec(w_in.shape), _const_spec(gv.shape),
            _const_spec(ws.shape), _const_spec(bs_t.shape), _const_spec(w_out.shape),
        ],
        out_specs=pl.BlockSpec((tm, d), lambda i: (i, 0)),
        scratch_shapes=[pltpu.VMEM((tm, half), BF16)],
        compiler_params=_params(("parallel",)),
        name="mixer_a",
    )(h, g, w_in, gv, ws, bs_t, w_out)


def _swiglu_kernel(x_ref, g_ref, win_ref, wout_ref, o_ref):
    f = wout_ref.shape[0]
    x = x_ref[...]
    xb = _rms(x, g_ref[...]).astype(BF16)
    a = _dot(xb, win_ref[:, :f])
    b = _dot(xb, win_ref[:, f:])
    hm = (a * jax.nn.sigmoid(a) * b).astype(BF16)
    o_ref[...] = x + _dot(hm, wout_ref[...])


def _swiglu(h, g, w_in, w_out, *, tm):
    n, d = h.shape
    return pl.pallas_call(
        _swiglu_kernel,
        out_shape=jax.ShapeDtypeStruct((n, d), F32),
        grid=(n // tm,),
        in_specs=[
            pl.BlockSpec((tm, d), lambda i: (i, 0)),
            _const_spec(g.shape), _const_spec(w_in.shape), _const_spec(w_out.shape),
        ],
        out_specs=pl.BlockSpec((tm, d), lambda i: (i, 0)),
        compiler_params=_params(("parallel",)),
        name="swiglu",
    )(h, g, w_in, w_out)


def _lane_cumsum(x):
    n = x.shape[-1]
    lane = lax.broadcasted_iota(jnp.int32, x.shape, x.ndim - 1)
    sh = 1
    while sh < n:
        x = x + jnp.where(lane >= sh, pltpu.roll(x, sh, axis=x.ndim - 1), 0.0)
        sh *= 2
    return x


def _split_bf16(x):
    pieces = []
    for _ in range(N_PIECES):
        p = x.astype(BF16)
        pieces.append(p)
        x = x - p.astype(F32)
    return pieces


def _qkv_kernel(x_ref, gkv_ref, gb_ref, wk_ref, wvt_ref, wft_ref, bf_ref, gk_ref, gq_ref, wqg_ref,
                hsum_ref, q_ref, k_ref, vt_ref, sg_ref, fp_ref, carry_ref, *, tiles_per_seq):
    d = x_ref.shape[1]
    i = pl.program_id(0)
    x = x_ref[...]
    y = x * lax.rsqrt(jnp.mean(x * x, axis=-1, keepdims=True) + RMS_EPS)
    skv = (y * gkv_ref[...]).astype(BF16)
    sb = (y * gb_ref[...]).astype(BF16)
    hsum = hsum_ref[...]

    def head_norm(t, gain):
        ssq = _dot((t * t).astype(BF16), hsum)
        return t * lax.rsqrt(ssq * (1.0 / HEAD_DIM) + RMS_EPS) * gain

    k = _dot(skv, wk_ref[...])
    k_ref[...] = head_norm(k, gk_ref[...]).astype(BF16)
    vt_ref[0] = _dot_nt(wvt_ref[...], skv).astype(BF16)
    q = _dot(sb, wqg_ref[:, :d])
    q_ref[...] = (head_norm(q, gq_ref[...]) * (LOG2E * HEAD_DIM ** -0.5)).astype(BF16)
    sg_ref[...] = jax.nn.sigmoid(_dot(sb, wqg_ref[:, d:])).astype(BF16)

    f = _dot_nt(wft_ref[...], skv) + bf_ref[...]
    logf = jax.nn.log_sigmoid(f)

    @pl.when(i % tiles_per_seq == 0)
    def _():
        carry_ref[...] = jnp.zeros_like(carry_ref)

    cum = _lane_cumsum(logf) + carry_ref[:, 0:1]
    carry_ref[...] = jnp.broadcast_to(cum[:, -1:], carry_ref.shape)
    fp_ref[0] = jnp.concatenate(_split_bf16(cum * LOG2E), axis=0)


def _qkv(h, gkv, gb, wk, wvt, wft, bf, gk, gq, wqg, hsum, *, tm, batch, seq):
    n, d = h.shape
    nh = wft.shape[0]
    tps = seq // tm
    tok = pl.BlockSpec((tm, d), lambda i: (i, 0))
    tok_bf = jax.ShapeDtypeStruct((n, d), BF16)
    seq_map = lambda i: (i // tps, 0, i % tps)
    consts = (gkv, gb, wk, wvt, wft, bf, gk, gq, wqg, hsum)
    return pl.pallas_call(
        functools.partial(_qkv_kernel, tiles_per_seq=tps),
        out_shape=[tok_bf, tok_bf, jax.ShapeDtypeStruct((batch, d, seq), BF16), tok_bf,
                   jax.ShapeDtypeStruct((batch, N_PIECES * nh, seq), BF16)],
        grid=(n // tm,),
        in_specs=[tok] + [_const_spec(a.shape) for a in consts],
        out_specs=[tok, tok, pl.BlockSpec((1, d, tm), seq_map), tok,
                   pl.BlockSpec((1, N_PIECES * nh, tm), seq_map)],
        scratch_shapes=[pltpu.VMEM((nh, LANES), F32)],
        compiler_params=_params(("arbitrary",)),
        name="qkv",
    )(h, *consts)


def _attention_kernel(q_ref, k_ref, vt_ref, fp_ref, sg_ref, o_ref, kaug_ref, qfill_ref, *, blk, nq):
    hp = pl.program_id(1)
    qi = pl.program_id(2)
    nh = fp_ref.shape[2] // N_PIECES
    lane = lax.broadcasted_iota(jnp.int32, (1, LANES), 1)
    own = (lane < HEAD_DIM, lane >= HEAD_DIM)
    spare = (HEAD_DIM, 0)
    q_lanes = [(lane >= s) & (lane < s + N_PIECES) for s in spare]
    k_lanes = [(lane >= s + N_PIECES) & (lane < s + 2 * N_PIECES) for s in spare]

    @pl.when(qi == 0)
    def _():
        er = lax.broadcasted_iota(jnp.int32, (N_PIECES * nh, LANES), 0)
        ec = lax.broadcasted_iota(jnp.int32, (N_PIECES * nh, LANES), 1)
        place = jnp.zeros((N_PIECES * nh, LANES), F32)
        for h in range(2):
            for j in range(N_PIECES):
                src = er == j * nh + 2 * hp + h
                place = (place + (src & (ec == spare[h] + j)).astype(F32)
                         - (src & (ec == spare[h] + N_PIECES + j)).astype(F32))
        g = _dot(fp_ref[0], place.astype(BF16))
        for h in range(2):
            k_fill = jnp.where(k_lanes[h], g, q_lanes[h].astype(F32)).astype(BF16)
            kaug_ref[h] = jnp.where(own[h], k_ref[...], k_fill)
            qfill_ref[h] = jnp.where(q_lanes[h], g, k_lanes[h].astype(F32)).astype(BF16)

    first_rows = lax.broadcasted_iota(jnp.int32, (LANES, 1), 0) < HEAD_DIM
    key = lax.broadcasted_iota(jnp.int32, (blk, blk), 0)
    qry = lax.broadcasted_iota(jnp.int32, (blk, blk), 1)
    causal = qry >= key

    def q_block(i):
        lo, hi = i * blk, (i + 1) * blk
        vt = vt_ref[0, :, 0:hi]
        scores = []
        for h in range(2):
            qa = jnp.where(own[h], q_ref[...], qfill_ref[h, lo:hi, :])
            scores.append(_dot_nt(kaug_ref[h, 0:hi, :], qa))
        probs = []
        for s in scores:
            diag = jnp.where(causal, s[lo:hi], NEG_INF)
            s = diag if i == 0 else jnp.concatenate([s[0:lo], diag], axis=0)
            p = jnp.exp2(s - jnp.max(s, axis=0, keepdims=True))
            probs.append((p.astype(BF16), jnp.sum(p, axis=0, keepdims=True)))
        outs = [_dot(vt, p) / l for p, l in probs]
        o = jnp.where(first_rows, outs[0], outs[1]).T
        o_ref[...] = (o * sg_ref[...].astype(F32)).astype(BF16)

    for i in range(nq):
        pl.when(qi == i)(functools.partial(q_block, i))


def _attention(q, k, vt, fp, sg, *, batch, seq, blk):
    n, d = q.shape
    nq = seq // blk
    qspec = pl.BlockSpec((blk, LANES), lambda b, hp, qi: (b * nq + qi, hp))
    return pl.pallas_call(
        functools.partial(_attention_kernel, blk=blk, nq=nq),
        out_shape=jax.ShapeDtypeStruct((n, d), BF16),
        grid=(batch, d // LANES, nq),
        in_specs=[
            qspec,
            pl.BlockSpec((seq, LANES), lambda b, hp, qi: (b, hp)),
            pl.BlockSpec((1, LANES, seq), lambda b, hp, qi: (b, hp, 0)),
            pl.BlockSpec((1, seq, fp.shape[2]), lambda b, hp, qi: (b, 0, 0)),
            qspec,
        ],
        out_specs=qspec,
        scratch_shapes=[pltpu.VMEM((2, seq, LANES), BF16)] * 2,
        compiler_params=_params(("parallel", "parallel", "arbitrary")),
        name="attention",
    )(q, k, vt, fp, sg)


def _out_router_kernel(og_ref, h_ref, wo_ref, gm_ref, wrt_ref, h2_ref, xn_ref, meta_ref, cnt_ref,
                       carry_ref):
    i = pl.program_id(0)
    ne, tm = meta_ref.shape
    h2 = h_ref[...] + _dot(og_ref[...], wo_ref[...])
    h2_ref[...] = h2
    xn = _rms(h2, gm_ref[...])
    xn_ref[...] = xn
    logits = _dot_nt(wrt_ref[...], xn, precision=lax.Precision.HIGHEST)
    row = lax.broadcasted_iota(jnp.int32, (ne, tm), 0).astype(F32)
    v1 = jnp.max(logits, axis=0, keepdims=True)
    i1 = jnp.min(jnp.where(logits == v1, row, ne), axis=0, keepdims=True)
    rest = jnp.where(row == i1, -jnp.inf, logits)
    v2 = jnp.max(rest, axis=0, keepdims=True)
    i2 = jnp.min(jnp.where(rest == v2, row, ne), axis=0, keepdims=True)
    e = jnp.exp(v2 - v1)
    w1 = 1.0 / (1.0 + e)
    w2 = e / (1.0 + e)
    sel1 = row == i1
    sel2 = row == i2
    oh = (sel1 | sel2).astype(F32)

    @pl.when(i == 0)
    def _():
        carry_ref[...] = jnp.zeros_like(carry_ref)

    base = carry_ref[:, 0:1]
    rank = base + (_lane_cumsum(oh) - oh)
    r1 = jnp.sum(jnp.where(sel1, rank, 0.0), axis=0, keepdims=True)
    r2 = jnp.sum(jnp.where(sel2, rank, 0.0), axis=0, keepdims=True)
    total = base + jnp.sum(oh, axis=1, keepdims=True)
    carry_ref[...] = jnp.broadcast_to(total, carry_ref.shape)
    cnt_ref[...] = jnp.broadcast_to(total, cnt_ref.shape)
    zeros = jnp.zeros_like(w1)
    meta_ref[...] = jnp.concatenate(
        [i1.astype(F32), i2.astype(F32), w1, w2, r1, r2, zeros, zeros], axis=0)


def _out_router(og, h, wo, gm, wrt, *, tm):
    n, d = h.shape
    ne = wrt.shape[0]
    tok = pl.BlockSpec((tm, d), lambda i: (i, 0))
    return pl.pallas_call(
        _out_router_kernel,
        out_shape=[jax.ShapeDtypeStruct((n, d), F32), jax.ShapeDtypeStruct((n, d), F32),
                   jax.ShapeDtypeStruct((ne, n), F32), jax.ShapeDtypeStruct((ne, LANES), F32)],
        grid=(n // tm,),
        in_specs=[tok, tok, _const_spec(wo.shape), _const_spec(gm.shape), _const_spec(wrt.shape)],
        out_specs=[tok, tok, pl.BlockSpec((ne, tm), lambda i: (0, i)),
                   pl.BlockSpec((ne, LANES), lambda i: (0, 0))],
        scratch_shapes=[pltpu.VMEM((ne, LANES), F32)],
        compiler_params=_params(("arbitrary",)),
        name="out_router",
    )(og, h, wo, gm, wrt)


def _row_copy(src_ref, src_row, dst_ref, dst_row, sem):
    return pltpu.make_async_copy(src_ref.at[pl.ds(src_row, 1)], dst_ref.at[pl.ds(dst_row, 1)], sem)


def _dispatch_kernel(dest_ref, xn_ref, init_ref, xs_ref, sem):
    del init_ref
    td = dest_ref.shape[1]

    def issue(t, _):
        for s in range(TOP_K):
            _row_copy(xn_ref, t, xs_ref, dest_ref[s, t], sem).start()
        return 0

    lax.fori_loop(0, td, issue, 0)

    def drain(t, _):
        for s in range(TOP_K):
            _row_copy(xn_ref, t, xs_ref, dest_ref[s, t], sem).wait()
        return 0

    lax.fori_loop(0, td, drain, 0)


def _dispatch(dest, xn, xs_init, *, td):
    n, d = xn.shape
    return pl.pallas_call(
        _dispatch_kernel,
        out_shape=jax.ShapeDtypeStruct(xs_init.shape, xs_init.dtype),
        grid=(n // td,),
        in_specs=[pl.BlockSpec((TOP_K, td), lambda i: (0, i), memory_space=pltpu.SMEM),
                  pl.BlockSpec((td, d), lambda i: (i, 0)), pl.BlockSpec(memory_space=pl.ANY)],
        out_specs=pl.BlockSpec(memory_space=pl.ANY),
        scratch_shapes=[pltpu.SemaphoreType.DMA],
        input_output_aliases={2: 0},
        compiler_params=_params(("arbitrary",)),
        name="dispatch",
    )(dest, xn, xs_init)


def _experts_kernel(te_ref, nu_ref, x_ref, wa_ref, wb_ref, wo_ref, o_ref, xb_ref):
    del te_ref
    i = pl.program_id(0)
    f = pl.program_id(1)

    @pl.when(f == 0)
    def _():
        o_ref[...] = jnp.zeros_like(o_ref)

    @pl.when(i < nu_ref[0])
    def _():
        @pl.when(f == 0)
        def _():
            xb_ref[...] = x_ref[...].astype(BF16)

        xb = xb_ref[...]
        a = _dot(xb, wa_ref[0])
        b = _dot(xb, wb_ref[0])
        hm = (a * jax.nn.sigmoid(a) * b).astype(BF16)
        o_ref[...] += _dot(hm, wo_ref[0])


def _experts(tile_expert, n_used, xs, w_in, w_out, *, tr, tf):
    p, d = xs.shape
    de = w_out.shape[1]
    nf = de // tf

    def row_map(i, f, te, nu):
        return (jnp.minimum(i, nu[0] - 1), 0)

    return pl.pallas_call(
        _experts_kernel,
        out_shape=jax.ShapeDtypeStruct((p, d), F32),
        grid_spec=pltpu.PrefetchScalarGridSpec(
            num_scalar_prefetch=2,
            grid=(p // tr, nf),
            in_specs=[
                pl.BlockSpec((tr, d), row_map),
                pl.BlockSpec((1, d, tf), lambda i, f, te, nu: (te[i], 0, f)),
                pl.BlockSpec((1, d, tf), lambda i, f, te, nu: (te[i], 0, nf + f)),
                pl.BlockSpec((1, tf, d), lambda i, f, te, nu: (te[i], f, 0)),
            ],
            out_specs=pl.BlockSpec((tr, d), lambda i, f, te, nu: (i, 0)),
            scratch_shapes=[pltpu.VMEM((tr, d), BF16)],
        ),
        compiler_params=_params(("arbitrary", "arbitrary")),
        name="experts",
    )(tile_expert, n_used, xs, w_in, w_in, w_out)


def _combine_kernel(dest_ref, eo_ref, h_ref, w_ref, o_ref, buf_ref, sem):
    tc = h_ref.shape[0]

    def issue(t, _):
        for s in range(TOP_K):
            _row_copy(eo_ref, dest_ref[s, t], buf_ref.at[s], t, sem).start()
        return 0

    lax.fori_loop(0, tc, issue, 0)

    def drain(t, _):
        for s in range(TOP_K):
            _row_copy(eo_ref, dest_ref[s, t], buf_ref.at[s], t, sem).wait()
        return 0

    lax.fori_loop(0, tc, drain, 0)
    w = w_ref[...]
    o_ref[...] = h_ref[...] + (w[:, 0:1] * buf_ref[0] + w[:, 1:2] * buf_ref[1])


def _combine(dest, eo, h, w, *, tc):
    n, d = h.shape
    tok = pl.BlockSpec((tc, d), lambda i: (i, 0))
    return pl.pallas_call(
        _combine_kernel,
        out_shape=jax.ShapeDtypeStruct((n, d), F32),
        grid=(n // tc,),
        in_specs=[pl.BlockSpec((TOP_K, tc), lambda i: (0, i), memory_space=pltpu.SMEM),
                  pl.BlockSpec(memory_space=pl.ANY), tok,
                  pl.BlockSpec((tc, TOP_K), lambda i: (i, 0))],
        out_specs=tok,
        scratch_shapes=[pltpu.VMEM((TOP_K, tc, d), F32), pltpu.SemaphoreType.DMA],
        compiler_params=_params(("arbitrary",)),
        name="combine",
    )(dest, eo, h, w)


def _tiles(n, seq):
    def pick(limit, of):
        t = limit
        while of % t:
            t //= 2
        return t
    return dict(
        tm_a=pick(256, seq), tm_f=pick(512, n), tm_qkv=pick(512, seq), blk=pick(256, seq),
        tm_o=pick(512, n), td=pick(512, n), tr=pick(1024, TOP_K * n), tc=pick(256, n))


def kernel(x, a_norm_g, a_w_in, a_v_norm_g, a_w_spatial, a_b_spatial, a_w_out, f_norm_g, f_w_in, f_w_out, kv_norm_g, kv_w, kv_b_f, k_norm_g, b_norm_g, b_w_in, q_norm_g, b_w_out, m_norm_g, m_w_router, m_w_in, m_w_out):
    batch, seq, d = x.shape
    n = batch * seq
    nh = d // HEAD_DIM
    ne = m_w_router.shape[-1]
    assert a_w_in.shape[0] == 1 and b_w_in.shape[0] == 1 and f_w_in.shape[0] == 1 and m_w_in.shape[0] == 1
    assert seq % GMLP_CHUNK == 0 and d % LANES == 0
    t = _tiles(n, seq)
    row = lambda g: g.reshape(1, -1)

    h = x.reshape(n, d)
    h = _mixer_a(h, row(a_norm_g[0]), a_w_in[0].astype(BF16), row(a_v_norm_g[0]), a_w_spatial[0],
                 a_b_spatial[0].T, a_w_out[0].astype(BF16), tm=t["tm_a"])
    h = _swiglu(h, row(f_norm_g[0]), f_w_in[0].astype(BF16), f_w_out[0].astype(BF16), tm=t["tm_f"])

    head = jnp.arange(d, dtype=jnp.int32) // HEAD_DIM
    hsum = (head[:, None] == head[None, :]).astype(BF16)
    q, k, vt, sg, fp = _qkv(
        h, row(kv_norm_g), row(b_norm_g[0]), kv_w[:, :d].astype(BF16),
        kv_w[:, d:2 * d].T.astype(BF16), kv_w[:, 2 * d:].T.astype(BF16), kv_b_f.reshape(nh, 1),
        row(jnp.tile(k_norm_g, nh)), row(jnp.tile(q_norm_g[0], nh)), b_w_in[0].astype(BF16), hsum,
        tm=t["tm_qkv"], batch=batch, seq=seq)
    og = _attention(q, k, vt, jnp.swapaxes(fp, 1, 2), sg, batch=batch, seq=seq, blk=t["blk"])

    h2, xn, meta, cnt = _out_router(og, h, b_w_out[0].astype(BF16), row(m_norm_g[0]),
                                    m_w_router[0].T, tm=t["tm_o"])

    tr = t["tr"]
    n_tiles = TOP_K * n // tr + ne
    counts = cnt[:, 0].astype(jnp.int32)
    tiles_per_expert = (counts + tr - 1) // tr
    tile_end = jnp.cumsum(tiles_per_expert)
    start = (tile_end - tiles_per_expert) * tr
    idx = meta[0:TOP_K].astype(jnp.int32)
    expert_ids = jnp.arange(ne, dtype=jnp.int32)
    start_of = jnp.sum(jnp.where(idx[:, :, None] == expert_ids, start, 0), axis=-1)
    dest = start_of + meta[4:4 + TOP_K].astype(jnp.int32)
    tile_ids = jnp.arange(n_tiles, dtype=jnp.int32)
    tile_expert = jnp.minimum(
        jnp.sum((tile_ids[:, None] >= tile_end[None, :]).astype(jnp.int32), axis=1), ne - 1)
    n_used = tile_end[-1:].astype(jnp.int32)

    xs = _dispatch(dest, xn, jnp.zeros((n_tiles * tr, d), F32), td=t["td"])
    eo = _experts(tile_expert, n_used, xs, m_w_in[0].astype(BF16), m_w_out[0].astype(BF16),
                  tr=tr, tf=m_w_out.shape[2] // 4)
    out = _combine(dest, eo, h2, meta[2:2 + TOP_K].T, tc=t["tc"])
    return out.reshape(batch, seq, d)
```

```python
import functools

import jax
import jax.numpy as jnp
from jax import lax
from jax.experimental import pallas as pl
from jax.experimental.pallas import tpu as pltpu

RMS_EPS = 1e-6
NEG_INF = -1e30
LOG2E = 1.4426950408889634
N_PIECES = 3
GMLP_CHUNK = 128
CAUSAL_CHUNK = 64
A_GROUPS = 8
HEAD_DIM = 64
LANES = 128
TOP_K = 2
VMEM_LIMIT = 56 * 1024 * 1024

BF16 = jnp.bfloat16
F32 = jnp.float32


def _dot(a, b, **kw):
    return jnp.dot(a, b, preferred_element_type=F32, **kw)


def _dot_nt(a, b, **kw):
    return lax.dot_general(a, b, (((1,), (1,)), ((), ())), preferred_element_type=F32, **kw)


def _rms(x, g):
    return x * lax.rsqrt(jnp.mean(x * x, axis=-1, keepdims=True) + RMS_EPS) * g


def _gelu(x):
    return 0.5 * x * (1.0 + lax.erf(x * (2.0 ** -0.5)))


def _const_spec(shape):
    nd = len(shape)
    return pl.BlockSpec(shape, lambda *_: (0,) * nd, pipeline_mode=pl.Buffered(1))


def _params(sem):
    return pltpu.CompilerParams(dimension_semantics=sem, vmem_limit_bytes=VMEM_LIMIT)


def _mixer_a_kernel(x_ref, g_ref, win_ref, gv_ref, ws_ref, bs_ref, wout_ref, o_ref, z_ref):
    tm = x_ref.shape[0]
    half = wout_ref.shape[0]
    gd = half // A_GROUPS
    x = x_ref[...]
    xb = _rms(x, g_ref[...]).astype(BF16)
    v = _gelu(_dot(xb, win_ref[:, half:]))
    v = (_rms(v, gv_ref[...])).astype(BF16)
    u = _gelu(_dot(xb, win_ref[:, :half]))
    row = lax.broadcasted_iota(jnp.int32, (GMLP_CHUNK, GMLP_CHUNK), 0)
    col = lax.broadcasted_iota(jnp.int32, (GMLP_CHUNK, GMLP_CHUNK), 1)
    keep = (col // CAUSAL_CHUNK) <= (row // CAUSAL_CHUNK)
    bs = bs_ref[...]
    for g in range(A_GROUPS):
        wg = jnp.where(keep, ws_ref[g], 0.0).astype(BF16)
        bg = bs[:, g:g + 1]
        for c in range(tm // GMLP_CHUNK):
            rs = slice(c * GMLP_CHUNK, (c + 1) * GMLP_CHUNK)
            cs = slice(g * gd, (g + 1) * gd)
            sv = _dot(wg, v[rs, cs]) + bg
            z_ref[rs, cs] = (u[rs, cs] * sv).astype(BF16)
    o_ref[...] = x + _dot(z_ref[...], wout_ref[...])


def _mixer_a(h, g, w_in, gv, ws, bs_t, w_out, *, tm):
    n, d = h.shape
    half = w_out.shape[0]
    return pl.pallas_call(
        _mixer_a_kernel,
        out_shape=jax.ShapeDtypeStruct((n, d), F32),
        grid=(n // tm,),
        in_specs=[
            pl.BlockSpec((tm, d), lambda i: (i, 0)),
            _const_spec(g.shape), _const_spec(w_in.shape), _const_spec(gv.shape),
            _const_spec(ws.shape), _const_spec(bs_t.shape), _const_spec(w_out.shape),
        ],
        out_specs=pl.BlockSpec((tm, d), lambda i: (i, 0)),
        scratch_shapes=[pltpu.VMEM((tm, half), BF16)],
        compiler_params=_params(("parallel",)),
        name="mixer_a",
    )(h, g, w_in, gv, ws, bs_t, w_out)


def _swiglu_kernel(x_ref, g_ref, win_ref, wout_ref, o_ref):
    f = wout_ref.shape[0]
    x = x_ref[...]
    xb = _rms(x, g_ref[...]).astype(BF16)
    a = _dot(xb, win_ref[:, :f])
    b = _dot(xb, win_ref[:, f:])
    hm = (a * jax.nn.sigmoid(a) * b).astype(BF16)
    o_ref[...] = x + _dot(hm, wout_ref[...])


def _swiglu(h, g, w_in, w_out, *, tm):
    n, d = h.shape
    return pl.pallas_call(
        _swiglu_kernel,
        out_shape=jax.ShapeDtypeStruct((n, d), F32),
        grid=(n // tm,),
        in_specs=[
            pl.BlockSpec((tm, d), lambda i: (i, 0)),
            _const_spec(g.shape), _const_spec(w_in.shape), _const_spec(w_out.shape),
        ],
        out_specs=pl.BlockSpec((tm, d), lambda i: (i, 0)),
        compiler_params=_params(("parallel",)),
        name="swiglu",
    )(h, g, w_in, w_out)


def _lane_cumsum(x):
    n = x.shape[-1]
    lane = lax.broadcasted_iota(jnp.int32, x.shape, x.ndim - 1)
    sh = 1
    while sh < n:
        x = x + jnp.where(lane >= sh, pltpu.roll(x, sh, axis=x.ndim - 1), 0.0)
        sh *= 2
    return x


def _split_bf16(x):
    pieces = []
    for _ in range(N_PIECES):
        p = x.astype(BF16)
        pieces.append(p)
        x = x - p.astype(F32)
    return pieces


def _qkv_kernel(x_ref, gkv_ref, gb_ref, wk_ref, wvt_ref, wft_ref, bf_ref, gk_ref, gq_ref, wqg_ref,
                hsum_ref, q_ref, k_ref, vt_ref, sg_ref, fp_ref, carry_ref, *, tiles_per_seq):
    d = x_ref.shape[1]
    i = pl.program_id(0)
    x = x_ref[...]
    y = x * lax.rsqrt(jnp.mean(x * x, axis=-1, keepdims=True) + RMS_EPS)
    skv = (y * gkv_ref[...]).astype(BF16)
    sb = (y * gb_ref[...]).astype(BF16)
    hsum = hsum_ref[...]

    def head_norm(t, gain):
        ssq = _dot((t * t).astype(BF16), hsum)
        return t * lax.rsqrt(ssq * (1.0 / HEAD_DIM) + RMS_EPS) * gain

    k = _dot(skv, wk_ref[...])
    k_ref[...] = head_norm(k, gk_ref[...]).astype(BF16)
    vt_ref[0] = _dot_nt(wvt_ref[...], skv).astype(BF16)
    q = _dot(sb, wqg_ref[:, :d])
    q_ref[...] = (head_norm(q, gq_ref[...]) * (LOG2E * HEAD_DIM ** -0.5)).astype(BF16)
    sg_ref[...] = jax.nn.sigmoid(_dot(sb, wqg_ref[:, d:])).astype(BF16)

    f = _dot_nt(wft_ref[...], skv) + bf_ref[...]
    logf = jax.nn.log_sigmoid(f)

    @pl.when(i % tiles_per_seq == 0)
    def _():
        carry_ref[...] = jnp.zeros_like(carry_ref)

    cum = _lane_cumsum(logf) + carry_ref[:, 0:1]
    carry_ref[...] = jnp.broadcast_to(cum[:, -1:], carry_ref.shape)
    fp_ref[0] = jnp.concatenate(_split_bf16(cum * LOG2E), axis=0)


def _qkv(h, gkv, gb, wk, wvt, wft, bf, gk, gq, wqg, hsum, *, tm, batch, seq):
    n, d = h.shape
    nh = wft.shape[0]
    tps = seq // tm
    tok = pl.BlockSpec((tm, d), lambda i: (i, 0))
    tok_bf = jax.ShapeDtypeStruct((n, d), BF16)
    seq_map = lambda i: (i // tps, 0, i % tps)
    consts = (gkv, gb, wk, wvt, wft, bf, gk, gq, wqg, hsum)
    return pl.pallas_call(
        functools.partial(_qkv_kernel, tiles_per_seq=tps),
        out_shape=[tok_bf, tok_bf, jax.ShapeDtypeStruct((batch, d, seq), BF16), tok_bf,
                   jax.ShapeDtypeStruct((batch, N_PIECES * nh, seq), BF16)],
        grid=(n // tm,),
        in_specs=[tok] + [_const_spec(a.shape) for a in consts],
        out_specs=[tok, tok, pl.BlockSpec((1, d, tm), seq_map), tok,
                   pl.BlockSpec((1, N_PIECES * nh, tm), seq_map)],
        scratch_shapes=[pltpu.VMEM((nh, LANES), F32)],
        compiler_params=_params(("arbitrary",)),
        name="qkv",
    )(h, *consts)


def _attention_kernel(q_ref, k_ref, vt_ref, fp_ref, sg_ref, o_ref, kaug_ref, qaug_ref, *, blk, nq):
    hp = pl.program_id(1)
    nh = fp_ref.shape[2] // N_PIECES
    lane = lax.broadcasted_iota(jnp.int32, (1, LANES), 1)
    own = (lane < HEAD_DIM, lane >= HEAD_DIM)
    spare = (HEAD_DIM, 0)
    q_lanes = [(lane >= s) & (lane < s + N_PIECES) for s in spare]
    k_lanes = [(lane >= s + N_PIECES) & (lane < s + 2 * N_PIECES) for s in spare]

    er = lax.broadcasted_iota(jnp.int32, (N_PIECES * nh, LANES), 0)
    ec = lax.broadcasted_iota(jnp.int32, (N_PIECES * nh, LANES), 1)
    place = jnp.zeros((N_PIECES * nh, LANES), F32)
    for h in range(2):
        for j in range(N_PIECES):
            src = er == j * nh + 2 * hp + h
            place = (place + (src & (ec == spare[h] + j)).astype(F32)
                     - (src & (ec == spare[h] + N_PIECES + j)).astype(F32))
    g = _dot(fp_ref[0], place.astype(BF16))
    for h in range(2):
        k_fill = jnp.where(k_lanes[h], g, q_lanes[h].astype(F32)).astype(BF16)
        kaug_ref[h] = jnp.where(own[h], k_ref[...], k_fill)
        q_fill = jnp.where(q_lanes[h], g, k_lanes[h].astype(F32)).astype(BF16)
        qaug_ref[h] = jnp.where(own[h], q_ref[...], q_fill)

    first_rows = lax.broadcasted_iota(jnp.int32, (LANES, 1), 0) < HEAD_DIM
    key = lax.broadcasted_iota(jnp.int32, (blk, blk), 0)
    qry = lax.broadcasted_iota(jnp.int32, (blk, blk), 1)
    causal = qry >= key

    for i in range(nq):
        lo, hi = i * blk, (i + 1) * blk
        vt = vt_ref[0, :, 0:hi]
        scores = [_dot_nt(kaug_ref[h, 0:hi, :], qaug_ref[h, lo:hi, :]) for h in range(2)]
        probs = []
        for s in scores:
            diag = jnp.where(causal, s[lo:hi], NEG_INF)
            s = diag if i == 0 else jnp.concatenate([s[0:lo], diag], axis=0)
            p = jnp.exp2(s - jnp.max(s, axis=0, keepdims=True))
            probs.append((p.astype(BF16), jnp.sum(p, axis=0, keepdims=True)))
        outs = [_dot(vt, p) / l for p, l in probs]
        o = jnp.where(first_rows, outs[0], outs[1]).T
        o_ref[lo:hi, :] = (o * sg_ref[lo:hi, :].astype(F32)).astype(BF16)


def _attention(q, k, vt, fp, sg, *, batch, seq, blk):
    n, d = q.shape
    pair = pl.BlockSpec((seq, LANES), lambda b, hp: (b, hp))
    return pl.pallas_call(
        functools.partial(_attention_kernel, blk=blk, nq=seq // blk),
        out_shape=jax.ShapeDtypeStruct((n, d), BF16),
        grid=(batch, d // LANES),
        in_specs=[
            pair, pair,
            pl.BlockSpec((1, LANES, seq), lambda b, hp: (b, hp, 0)),
            pl.BlockSpec((1, seq, fp.shape[2]), lambda b, hp: (b, 0, 0)),
            pair,
        ],
        out_specs=pair,
        scratch_shapes=[pltpu.VMEM((2, seq, LANES), BF16)] * 2,
        compiler_params=_params(("parallel", "parallel")),
        name="attention",
    )(q, k, vt, fp, sg)


def _out_router_kernel(og_ref, h_ref, wo_ref, gm_ref, wrt_ref, h2_ref, xn_ref, meta_ref, cnt_ref,
                       carry_ref):
    i = pl.program_id(0)
    ne, tm = meta_ref.shape
    h2 = h_ref[...] + _dot(og_ref[...], wo_ref[...])
    h2_ref[...] = h2
    xn = _rms(h2, gm_ref[...])
    xn_ref[...] = xn
    logits = _dot_nt(wrt_ref[...], xn, precision=lax.Precision.HIGHEST)
    row = lax.broadcasted_iota(jnp.int32, (ne, tm), 0).astype(F32)
    v1 = jnp.max(logits, axis=0, keepdims=True)
    i1 = jnp.min(jnp.where(logits == v1, row, ne), axis=0, keepdims=True)
    rest = jnp.where(row == i1, -jnp.inf, logits)
    v2 = jnp.max(rest, axis=0, keepdims=True)
    i2 = jnp.min(jnp.where(rest == v2, row, ne), axis=0, keepdims=True)
    e = jnp.exp(v2 - v1)
    w1 = 1.0 / (1.0 + e)
    w2 = e / (1.0 + e)
    sel1 = row == i1
    sel2 = row == i2
    oh = (sel1 | sel2).astype(F32)

    @pl.when(i == 0)
    def _():
        carry_ref[...] = jnp.zeros_like(carry_ref)

    base = carry_ref[:, 0:1]
    rank = base + (_lane_cumsum(oh) - oh)
    r1 = jnp.sum(jnp.where(sel1, rank, 0.0), axis=0, keepdims=True)
    r2 = jnp.sum(jnp.where(sel2, rank, 0.0), axis=0, keepdims=True)
    total = base + jnp.sum(oh, axis=1, keepdims=True)
    carry_ref[...] = jnp.broadcast_to(total, carry_ref.shape)
    cnt_ref[...] = jnp.broadcast_to(total, cnt_ref.shape)
    zeros = jnp.zeros_like(w1)
    meta_ref[...] = jnp.concatenate(
        [i1.astype(F32), i2.astype(F32), w1, w2, r1, r2, zeros, zeros], axis=0)


def _out_router(og, h, wo, gm, wrt, *, tm):
    n, d = h.shape
    ne = wrt.shape[0]
    tok = pl.BlockSpec((tm, d), lambda i: (i, 0))
    return pl.pallas_call(
        _out_router_kernel,
        out_shape=[jax.ShapeDtypeStruct((n, d), F32), jax.ShapeDtypeStruct((n, d), F32),
                   jax.ShapeDtypeStruct((ne, n), F32), jax.ShapeDtypeStruct((ne, LANES), F32)],
        grid=(n // tm,),
        in_specs=[tok, tok, _const_spec(wo.shape), _const_spec(gm.shape), _const_spec(wrt.shape)],
        out_specs=[tok, tok, pl.BlockSpec((ne, tm), lambda i: (0, i)),
                   pl.BlockSpec((ne, LANES), lambda i: (0, 0))],
        scratch_shapes=[pltpu.VMEM((ne, LANES), F32)],
        compiler_params=_params(("arbitrary",)),
        name="out_router",
    )(og, h, wo, gm, wrt)


def _row_copy(src_ref, src_row, dst_ref, dst_row, sem):
    return pltpu.make_async_copy(src_ref.at[pl.ds(src_row, 1)], dst_ref.at[pl.ds(dst_row, 1)], sem)


def _dispatch_kernel(dest_ref, xn_ref, init_ref, xs_ref, sem):
    del init_ref
    td = dest_ref.shape[1]

    def issue(t, _):
        for s in range(TOP_K):
            _row_copy(xn_ref, t, xs_ref, dest_ref[s, t], sem).start()
        return 0

    lax.fori_loop(0, td, issue, 0)

    def drain(t, _):
        for s in range(TOP_K):
            _row_copy(xn_ref, t, xs_ref, dest_ref[s, t], sem).wait()
        return 0

    lax.fori_loop(0, td, drain, 0)


def _dispatch(dest, xn, xs_init, *, td):
    n, d = xn.shape
    return pl.pallas_call(
        _dispatch_kernel,
        out_shape=jax.ShapeDtypeStruct(xs_init.shape, xs_init.dtype),
        grid=(n // td,),
        in_specs=[pl.BlockSpec((TOP_K, td), lambda i: (0, i), memory_space=pltpu.SMEM),
                  pl.BlockSpec((td, d), lambda i: (i, 0)), pl.BlockSpec(memory_space=pl.ANY)],
        out_specs=pl.BlockSpec(memory_space=pl.ANY),
        scratch_shapes=[pltpu.SemaphoreType.DMA],
        input_output_aliases={2: 0},
        compiler_params=_params(("arbitrary",)),
        name="dispatch",
    )(dest, xn, xs_init)


def _experts_kernel(te_ref, nu_ref, x_ref, wa_ref, wb_ref, wo_ref, o_ref, xb_ref):
    del te_ref
    i = pl.program_id(0)
    f = pl.program_id(1)

    @pl.when(f == 0)
    def _():
        o_ref[...] = jnp.zeros_like(o_ref)

    @pl.when(i < nu_ref[0])
    def _():
        @pl.when(f == 0)
        def _():
            xb_ref[...] = x_ref[...].astype(BF16)

        xb = xb_ref[...]
        a = _dot(xb, wa_ref[0])
        b = _dot(xb, wb_ref[0])
        hm = (a * jax.nn.sigmoid(a) * b).astype(BF16)
        o_ref[...] += _dot(hm, wo_ref[0])


def _experts(tile_expert, n_used, xs, w_in, w_out, *, tr, tf):
    p, d = xs.shape
    de = w_out.shape[1]
    nf = de // tf

    def row_map(i, f, te, nu):
        return (jnp.minimum(i, nu[0] - 1), 0)

    return pl.pallas_call(
        _experts_kernel,
        out_shape=jax.ShapeDtypeStruct((p, d), F32),
        grid_spec=pltpu.PrefetchScalarGridSpec(
            num_scalar_prefetch=2,
            grid=(p // tr, nf),
            in_specs=[
                pl.BlockSpec((tr, d), row_map),
                pl.BlockSpec((1, d, tf), lambda i, f, te, nu: (te[i], 0, f)),
                pl.BlockSpec((1, d, tf), lambda i, f, te, nu: (te[i], 0, nf + f)),
                pl.BlockSpec((1, tf, d), lambda i, f, te, nu: (te[i], f, 0)),
            ],
            out_specs=pl.BlockSpec((tr, d), lambda i, f, te, nu: (i, 0)),
            scratch_shapes=[pltpu.VMEM((tr, d), BF16)],
        ),
        compiler_params=_params(("arbitrary", "arbitrary")),
        name="experts",
    )(tile_expert, n_used, xs, w_in, w_in, w_out)


def _combine_kernel(dest_ref, eo_ref, h_ref, w_ref, o_ref, buf_ref, sem):
    tc = h_ref.shape[0]

    def issue(t, _):
        for s in range(TOP_K):
            _row_copy(eo_ref, dest_ref[s, t], buf_ref.at[s], t, sem).start()
        return 0

    lax.fori_loop(0, tc, issue, 0)

    def drain(t, _):
        for s in range(TOP_K):
            _row_copy(eo_ref, dest_ref[s, t], buf_ref.at[s], t, sem).wait()
        return 0

    lax.fori_loop(0, tc, drain, 0)
    w = w_ref[...]
    o_ref[...] = h_ref[...] + (w[:, 0:1] * buf_ref[0] + w[:, 1:2] * buf_ref[1])


def _combine(dest, eo, h, w, *, tc):
    n, d = h.shape
    tok = pl.BlockSpec((tc, d), lambda i: (i, 0))
    return pl.pallas_call(
        _combine_kernel,
        out_shape=jax.ShapeDtypeStruct((n, d), F32),
        grid=(n // tc,),
        in_specs=[pl.BlockSpec((TOP_K, tc), lambda i: (0, i), memory_space=pltpu.SMEM),
                  pl.BlockSpec(memory_space=pl.ANY), tok,
                  pl.BlockSpec((tc, TOP_K), lambda i: (i, 0))],
        out_specs=tok,
        scratch_shapes=[pltpu.VMEM((TOP_K, tc, d), F32), pltpu.SemaphoreType.DMA],
        compiler_params=_params(("arbitrary",)),
        name="combine",
    )(dest, eo, h, w)


def _tiles(n, seq):
    def pick(limit, of):
        t = limit
        while of % t:
            t //= 2
        return t
    return dict(
        tm_a=pick(256, seq), tm_f=pick(512, n), tm_qkv=pick(512, seq), blk=pick(256, seq),
        tm_o=pick(512, n), td=pick(512, n), tr=pick(1024, TOP_K * n), tc=pick(256, n))


def kernel(x, a_norm_g, a_w_in, a_v_norm_g, a_w_spatial, a_b_spatial, a_w_out, f_norm_g, f_w_in, f_w_out, kv_norm_g, kv_w, kv_b_f, k_norm_g, b_norm_g, b_w_in, q_norm_g, b_w_out, m_norm_g, m_w_router, m_w_in, m_w_out):
    batch, seq, d = x.shape
    n = batch * seq
    nh = d // HEAD_DIM
    ne = m_w_router.shape[-1]
    assert a_w_in.shape[0] == 1 and b_w_in.shape[0] == 1 and f_w_in.shape[0] == 1 and m_w_in.shape[0] == 1
    assert seq % GMLP_CHUNK == 0 and d % LANES == 0
    t = _tiles(n, seq)
    row = lambda g: g.reshape(1, -1)

    h = x.reshape(n, d)
    h = _mixer_a(h, row(a_norm_g[0]), a_w_in[0].astype(BF16), row(a_v_norm_g[0]), a_w_spatial[0],
                 a_b_spatial[0].T, a_w_out[0].astype(BF16), tm=t["tm_a"])
    h = _swiglu(h, row(f_norm_g[0]), f_w_in[0].astype(BF16), f_w_out[0].astype(BF16), tm=t["tm_f"])

    head = jnp.arange(d, dtype=jnp.int32) // HEAD_DIM
    hsum = (head[:, None] == head[None, :]).astype(BF16)
    q, k, vt, sg, fp = _qkv(
        h, row(kv_norm_g), row(b_norm_g[0]), kv_w[:, :d].astype(BF16),
        kv_w[:, d:2 * d].T.astype(BF16), kv_w[:, 2 * d:].T.astype(BF16), kv_b_f.reshape(nh, 1),
        row(jnp.tile(k_norm_g, nh)), row(jnp.tile(q_norm_g[0], nh)), b_w_in[0].astype(BF16), hsum,
        tm=t["tm_qkv"], batch=batch, seq=seq)
    og = _attention(q, k, vt, jnp.swapaxes(fp, 1, 2), sg, batch=batch, seq=seq, blk=t["blk"])

    h2, xn, meta, cnt = _out_router(og, h, b_w_out[0].astype(BF16), row(m_norm_g[0]),
                                    m_w_router[0].T, tm=t["tm_o"])

    tr = t["tr"]
    n_tiles = TOP_K * n // tr + ne
    counts = cnt[:, 0].astype(jnp.int32)
    tiles_per_expert = (counts + tr - 1) // tr
    tile_end = jnp.cumsum(tiles_per_expert)
    start = (tile_end - tiles_per_expert) * tr
    idx = meta[0:TOP_K].astype(jnp.int32)
    expert_ids = jnp.arange(ne, dtype=jnp.int32)
    start_of = jnp.sum(jnp.where(idx[:, :, None] == expert_ids, start, 0), axis=-1)
    dest = start_of + meta[4:4 + TOP_K].astype(jnp.int32)
    tile_ids = jnp.arange(n_tiles, dtype=jnp.int32)
    tile_expert = jnp.minimum(
        jnp.sum((tile_ids[:, None] >= tile_end[None, :]).astype(jnp.int32), axis=1), ne - 1)
    n_used = tile_end[-1:].astype(jnp.int32)

    xs = _dispatch(dest, xn, jnp.zeros((n_tiles * tr, d), F32), td=t["td"])
    eo = _experts(tile_expert, n_used, xs, m_w_in[0].astype(BF16), m_w_out[0].astype(BF16),
                  tr=tr, tf=m_w_out.shape[2] // 4)
    out = _combine(dest, eo, h2, meta[2:2 + TOP_K].T, tc=t["tc"])
    return out.reshape(batch, seq, d)
```

```python
import functools

import jax
import jax.numpy as jnp
from jax import lax
from jax.experimental import pallas as pl
from jax.experimental.pallas import tpu as pltpu
from jax.experimental.pallas import tpu_sc as plsc

RMS_EPS = 1e-6
NEG_INF = -1e30
LOG2E = 1.4426950408889634
N_PIECES = 3
SC_WINDOW = 128
SUBROWS = 2
GMLP_CHUNK = 128
CAUSAL_CHUNK = 64
A_GROUPS = 8
HEAD_DIM = 64
LANES = 128
TOP_K = 2
VMEM_LIMIT = 56 * 1024 * 1024

BF16 = jnp.bfloat16
F32 = jnp.float32


def _dot(a, b, **kw):
    return jnp.dot(a, b, preferred_element_type=F32, **kw)


def _dot_nt(a, b, **kw):
    return lax.dot_general(a, b, (((1,), (1,)), ((), ())), preferred_element_type=F32, **kw)


def _rms(x, g):
    return x * lax.rsqrt(jnp.mean(x * x, axis=-1, keepdims=True) + RMS_EPS) * g


def _gelu(x):
    return 0.5 * x * (1.0 + lax.erf(x * (2.0 ** -0.5)))


def _const_spec(shape):
    nd = len(shape)
    return pl.BlockSpec(shape, lambda *_: (0,) * nd, pipeline_mode=pl.Buffered(1))


def _params(sem):
    return pltpu.CompilerParams(dimension_semantics=sem, vmem_limit_bytes=VMEM_LIMIT)


def _mixer_a_kernel(x_ref, g_ref, win_ref, gv_ref, ws_ref, bs_ref, wout_ref, o_ref, z_ref):
    tm = x_ref.shape[0]
    half = wout_ref.shape[0]
    gd = half // A_GROUPS
    x = x_ref[...]
    xb = _rms(x, g_ref[...]).astype(BF16)
    v = _gelu(_dot(xb, win_ref[:, half:]))
    v = (_rms(v, gv_ref[...])).astype(BF16)
    u = _gelu(_dot(xb, win_ref[:, :half]))
    row = lax.broadcasted_iota(jnp.int32, (GMLP_CHUNK, GMLP_CHUNK), 0)
    col = lax.broadcasted_iota(jnp.int32, (GMLP_CHUNK, GMLP_CHUNK), 1)
    keep = (col // CAUSAL_CHUNK) <= (row // CAUSAL_CHUNK)
    bs = bs_ref[...]
    for g in range(A_GROUPS):
        wg = jnp.where(keep, ws_ref[g], 0.0).astype(BF16)
        bg = bs[:, g:g + 1]
        for c in range(tm // GMLP_CHUNK):
            rs = slice(c * GMLP_CHUNK, (c + 1) * GMLP_CHUNK)
            cs = slice(g * gd, (g + 1) * gd)
            sv = _dot(wg, v[rs, cs]) + bg
            z_ref[rs, cs] = (u[rs, cs] * sv).astype(BF16)
    o_ref[...] = x + _dot(z_ref[...], wout_ref[...])


def _mixer_a(h, g, w_in, gv, ws, bs_t, w_out, *, tm):
    n, d = h.shape
    half = w_out.shape[0]
    return pl.pallas_call(
        _mixer_a_kernel,
        out_shape=jax.ShapeDtypeStruct((n, d), F32),
        grid=(n // tm,),
        in_specs=[
            pl.BlockSpec((tm, d), lambda i: (i, 0)),
            _const_spec(g.shape), _const_spec(w_in.shape), _const_spec(gv.shape),
            _const_spec(ws.shape), _const_spec(bs_t.shape), _const_spec(w_out.shape),
        ],
        out_specs=pl.BlockSpec((tm, d), lambda i: (i, 0)),
        scratch_shapes=[pltpu.VMEM((tm, half), BF16)],
        compiler_params=_params(("parallel",)),
        name="mixer_a",
    )(h, g, w_in, gv, ws, bs_t, w_out)


def _swiglu_kernel(x_ref, g_ref, win_ref, wout_ref, o_ref):
    f = wout_ref.shape[0]
    x = x_ref[...]
    xb = _rms(x, g_ref[...]).astype(BF16)
    a = _dot(xb, win_ref[:, :f])
    b = _dot(xb, win_ref[:, f:])
    hm = (a * jax.nn.sigmoid(a) * b).astype(BF16)
    o_ref[...] = x + _dot(hm, wout_ref[...])


def _swiglu(h, g, w_in, w_out, *, tm):
    n, d = h.shape
    return pl.pallas_call(
        _swiglu_kernel,
        out_shape=jax.ShapeDtypeStruct((n, d), F32),
        grid=(n // tm,),
        in_specs=[
            pl.BlockSpec((tm, d), lambda i: (i, 0)),
            _const_spec(g.shape), _const_spec(w_in.shape), _const_spec(w_out.shape),
        ],
        out_specs=pl.BlockSpec((tm, d), lambda i: (i, 0)),
        compiler_params=_params(("parallel",)),
        name="swiglu",
    )(h, g, w_in, w_out)


def _lane_cumsum(x):
    n = x.shape[-1]
    lane = lax.broadcasted_iota(jnp.int32, x.shape, x.ndim - 1)
    sh = 1
    while sh < n:
        x = x + jnp.where(lane >= sh, pltpu.roll(x, sh, axis=x.ndim - 1), 0.0)
        sh *= 2
    return x


def _split_bf16(x):
    pieces = []
    for _ in range(N_PIECES):
        p = x.astype(BF16)
        pieces.append(p)
        x = x - p.astype(F32)
    return pieces


def _qkv_kernel(x_ref, gkv_ref, gb_ref, wk_ref, wvt_ref, wft_ref, bf_ref, gk_ref, gq_ref, wqg_ref,
                hsum_ref, q_ref, k_ref, vt_ref, sg_ref, fp_ref, carry_ref, *, tiles_per_seq):
    d = x_ref.shape[1]
    i = pl.program_id(0)
    x = x_ref[...]
    y = x * lax.rsqrt(jnp.mean(x * x, axis=-1, keepdims=True) + RMS_EPS)
    skv = (y * gkv_ref[...]).astype(BF16)
    sb = (y * gb_ref[...]).astype(BF16)
    hsum = hsum_ref[...]

    def head_norm(t, gain):
        ssq = _dot((t * t).astype(BF16), hsum)
        return t * lax.rsqrt(ssq * (1.0 / HEAD_DIM) + RMS_EPS) * gain

    k = _dot(skv, wk_ref[...])
    k_ref[...] = head_norm(k, gk_ref[...]).astype(BF16)
    vt_ref[0] = _dot_nt(wvt_ref[...], skv).astype(BF16)
    q = _dot(sb, wqg_ref[:, :d])
    q_ref[...] = (head_norm(q, gq_ref[...]) * (LOG2E * HEAD_DIM ** -0.5)).astype(BF16)
    sg_ref[...] = jax.nn.sigmoid(_dot(sb, wqg_ref[:, d:])).astype(BF16)

    f = _dot_nt(wft_ref[...], skv) + bf_ref[...]
    logf = jax.nn.log_sigmoid(f)

    @pl.when(i % tiles_per_seq == 0)
    def _():
        carry_ref[...] = jnp.zeros_like(carry_ref)

    cum = _lane_cumsum(logf) + carry_ref[:, 0:1]
    carry_ref[...] = jnp.broadcast_to(cum[:, -1:], carry_ref.shape)
    fp_ref[0] = jnp.concatenate(_split_bf16(cum * LOG2E), axis=0)


def _qkv(h, gkv, gb, wk, wvt, wft, bf, gk, gq, wqg, hsum, *, tm, batch, seq):
    n, d = h.shape
    nh = wft.shape[0]
    tps = seq // tm
    tok = pl.BlockSpec((tm, d), lambda i: (i, 0))
    tok_bf = jax.ShapeDtypeStruct((n, d), BF16)
    seq_map = lambda i: (i // tps, 0, i % tps)
    consts = (gkv, gb, wk, wvt, wft, bf, gk, gq, wqg, hsum)
    return pl.pallas_call(
        functools.partial(_qkv_kernel, tiles_per_seq=tps),
        out_shape=[tok_bf, tok_bf, jax.ShapeDtypeStruct((batch, d, seq), BF16), tok_bf,
                   jax.ShapeDtypeStruct((batch, N_PIECES * nh, seq), BF16)],
        grid=(n // tm,),
        in_specs=[tok] + [_const_spec(a.shape) for a in consts],
        out_specs=[tok, tok, pl.BlockSpec((1, d, tm), seq_map), tok,
                   pl.BlockSpec((1, N_PIECES * nh, tm), seq_map)],
        scratch_shapes=[pltpu.VMEM((nh, LANES), F32)],
        compiler_params=_params(("arbitrary",)),
        name="qkv",
    )(h, *consts)


def _attention_kernel(q_ref, k_ref, vt_ref, fp_ref, sg_ref, o_ref, kaug_ref, qaug_ref, *, blk, nq):
    hp = pl.program_id(1)
    nh = fp_ref.shape[2] // N_PIECES
    lane = lax.broadcasted_iota(jnp.int32, (1, LANES), 1)
    own = (lane < HEAD_DIM, lane >= HEAD_DIM)
    spare = (HEAD_DIM, 0)
    q_lanes = [(lane >= s) & (lane < s + N_PIECES) for s in spare]
    k_lanes = [(lane >= s + N_PIECES) & (lane < s + 2 * N_PIECES) for s in spare]

    er = lax.broadcasted_iota(jnp.int32, (N_PIECES * nh, LANES), 0)
    ec = lax.broadcasted_iota(jnp.int32, (N_PIECES * nh, LANES), 1)
    place = jnp.zeros((N_PIECES * nh, LANES), F32)
    for h in range(2):
        for j in range(N_PIECES):
            src = er == j * nh + 2 * hp + h
            place = (place + (src & (ec == spare[h] + j)).astype(F32)
                     - (src & (ec == spare[h] + N_PIECES + j)).astype(F32))
    g = _dot(fp_ref[0], place.astype(BF16))
    for h in range(2):
        k_fill = jnp.where(k_lanes[h], g, q_lanes[h].astype(F32)).astype(BF16)
        kaug_ref[h] = jnp.where(own[h], k_ref[...], k_fill)
        q_fill = jnp.where(q_lanes[h], g, k_lanes[h].astype(F32)).astype(BF16)
        qaug_ref[h] = jnp.where(own[h], q_ref[...], q_fill)

    first_rows = lax.broadcasted_iota(jnp.int32, (LANES, 1), 0) < HEAD_DIM
    key = lax.broadcasted_iota(jnp.int32, (blk, blk), 0)
    qry = lax.broadcasted_iota(jnp.int32, (blk, blk), 1)
    causal = qry >= key

    for i in range(nq):
        lo, hi = i * blk, (i + 1) * blk
        vt = vt_ref[0, :, 0:hi]
        scores = [_dot_nt(kaug_ref[h, 0:hi, :], qaug_ref[h, lo:hi, :]) for h in range(2)]
        probs = []
        for s in scores:
            diag = jnp.where(causal, s[lo:hi], NEG_INF)
            s = diag if i == 0 else jnp.concatenate([s[0:lo], diag], axis=0)
            p = jnp.exp2(s - jnp.max(s, axis=0, keepdims=True))
            probs.append((p.astype(BF16), jnp.sum(p, axis=0, keepdims=True)))
        outs = [_dot(vt, p) / l for p, l in probs]
        o = jnp.where(first_rows, outs[0], outs[1]).T
        o_ref[lo:hi, :] = (o * sg_ref[lo:hi, :].astype(F32)).astype(BF16)


def _attention(q, k, vt, fp, sg, *, batch, seq, blk):
    n, d = q.shape
    pair = pl.BlockSpec((seq, LANES), lambda b, hp: (b, hp))
    return pl.pallas_call(
        functools.partial(_attention_kernel, blk=blk, nq=seq // blk),
        out_shape=jax.ShapeDtypeStruct((n, d), BF16),
        grid=(batch, d // LANES),
        in_specs=[
            pair, pair,
            pl.BlockSpec((1, LANES, seq), lambda b, hp: (b, hp, 0)),
            pl.BlockSpec((1, seq, fp.shape[2]), lambda b, hp: (b, 0, 0)),
            pair,
        ],
        out_specs=pair,
        scratch_shapes=[pltpu.VMEM((2, seq, LANES), BF16)] * 2,
        compiler_params=_params(("parallel", "parallel")),
        name="attention",
    )(q, k, vt, fp, sg)


def _pack_bf16_pair(lo, hi):
    ulo = lax.bitcast_convert_type(lo.astype(BF16).astype(F32), jnp.uint32)
    uhi = lax.bitcast_convert_type(hi.astype(BF16).astype(F32), jnp.uint32)
    return (ulo >> 16) | uhi


def _pack_row(x):
    half = x.shape[1] // 2
    w = _pack_bf16_pair(x[:, :half], x[:, half:])
    sw = half // SUBROWS
    return [w[:, c * sw:(c + 1) * sw] for c in range(SUBROWS)]


def _unpack_row(subrows):
    lo = [lax.bitcast_convert_type(p << 16, F32) for p in subrows]
    hi = [lax.bitcast_convert_type(p & jnp.uint32(0xFFFF0000), F32) for p in subrows]
    return jnp.concatenate(lo + hi, axis=1)


def _out_router_kernel(og_ref, h_ref, wo_ref, gm_ref, wrt_ref, h2_ref, xp_ref, meta_ref, cnt_ref,
                       carry_ref):
    i = pl.program_id(0)
    ne, tm = meta_ref.shape
    h2 = h_ref[...] + _dot(og_ref[...], wo_ref[...])
    h2_ref[...] = h2
    xn = _rms(h2, gm_ref[...])
    for c, sub in enumerate(_pack_row(xn)):
        xp_ref[c] = sub
    logits = _dot_nt(wrt_ref[...], xn, precision=lax.Precision.HIGHEST)
    row = lax.broadcasted_iota(jnp.int32, (ne, tm), 0).astype(F32)
    v1 = jnp.max(logits, axis=0, keepdims=True)
    i1 = jnp.min(jnp.where(logits == v1, row, ne), axis=0, keepdims=True)
    rest = jnp.where(row == i1, -jnp.inf, logits)
    v2 = jnp.max(rest, axis=0, keepdims=True)
    i2 = jnp.min(jnp.where(rest == v2, row, ne), axis=0, keepdims=True)
    e = jnp.exp(v2 - v1)
    w1 = 1.0 / (1.0 + e)
    w2 = e / (1.0 + e)
    sel1 = row == i1
    sel2 = row == i2
    oh = (sel1 | sel2).astype(F32)

    @pl.when(i == 0)
    def _():
        carry_ref[...] = jnp.zeros_like(carry_ref)

    base = carry_ref[:, 0:1]
    rank = base + (_lane_cumsum(oh) - oh)
    r1 = jnp.sum(jnp.where(sel1, rank, 0.0), axis=0, keepdims=True)
    r2 = jnp.sum(jnp.where(sel2, rank, 0.0), axis=0, keepdims=True)
    total = base + jnp.sum(oh, axis=1, keepdims=True)
    carry_ref[...] = jnp.broadcast_to(total, carry_ref.shape)
    cnt_ref[...] = jnp.broadcast_to(total, cnt_ref.shape)
    zeros = jnp.zeros_like(w1)
    meta_ref[...] = jnp.concatenate(
        [i1.astype(F32), i2.astype(F32), w1, w2, r1, r2, zeros, zeros], axis=0)


def _out_router(og, h, wo, gm, wrt, *, tm):
    n, d = h.shape
    ne = wrt.shape[0]
    sw = d // 2 // SUBROWS
    tok = pl.BlockSpec((tm, d), lambda i: (i, 0))
    return pl.pallas_call(
        _out_router_kernel,
        out_shape=[jax.ShapeDtypeStruct((n, d), F32), jax.ShapeDtypeStruct((SUBROWS, n, sw), jnp.uint32),
                   jax.ShapeDtypeStruct((ne, n), F32), jax.ShapeDtypeStruct((ne, LANES), F32)],
        grid=(n // tm,),
        in_specs=[tok, tok, _const_spec(wo.shape), _const_spec(gm.shape), _const_spec(wrt.shape)],
        out_specs=[tok, pl.BlockSpec((SUBROWS, tm, sw), lambda i: (0, i, 0)),
                   pl.BlockSpec((ne, tm), lambda i: (0, i)),
                   pl.BlockSpec((ne, LANES), lambda i: (0, 0))],
        scratch_shapes=[pltpu.VMEM((ne, LANES), F32)],
        compiler_params=_params(("arbitrary",)),
        name="out_router",
    )(og, h, wo, gm, wrt)


def _sc_mesh():
    return plsc.VectorSubcoreMesh(core_axis_name="core", subcore_axis_name="subcore")


def _sc_scatter(x, idx, out_rows):
    rows, w = x.shape
    n_idx = idx.shape[0]

    @pl.kernel(out_type=jax.ShapeDtypeStruct((out_rows, w), x.dtype), mesh=_sc_mesh(),
               scratch_types=[])
    def scatter_rows(x_hbm, i_hbm, o_hbm):
        def body(x_vmem, i_vmem):
            for s in range(n_idx):
                pltpu.sync_copy(x_vmem, o_hbm.at[i_vmem.at[s]])

        pltpu.emit_pipeline(
            body, grid=(rows // SC_WINDOW,),
            in_specs=[pl.BlockSpec((SC_WINDOW, w), lambda i: (i, 0)),
                      pl.BlockSpec((n_idx, SC_WINDOW), lambda i: (0, i))],
            out_specs=[],
            core_axis_name=("core", "subcore"),
            dimension_semantics=(pltpu.PARALLEL,),
        )(x_hbm, i_hbm)

    return scatter_rows(x, idx)


def _sc_gather(x, idx):
    n = idx.shape[1]
    w = x.shape[1]

    @pl.kernel(out_type=jax.ShapeDtypeStruct((n, w), x.dtype), mesh=_sc_mesh(), scratch_types=[])
    def gather_rows(x_hbm, i_hbm, o_hbm):
        def body(i_vmem, o_vmem):
            pltpu.sync_copy(x_hbm.at[i_vmem.at[0]], o_vmem)

        pltpu.emit_pipeline(
            body, grid=(n // SC_WINDOW,),
            in_specs=[pl.BlockSpec((1, SC_WINDOW), lambda i: (0, i))],
            out_specs=[pl.BlockSpec((SC_WINDOW, w), lambda i: (i, 0))],
            core_axis_name=("core", "subcore"),
            dimension_semantics=(pltpu.PARALLEL,),
        )(i_hbm, o_hbm)

    return gather_rows(x, idx)


def _experts_kernel(te_ref, rows_ref, x_ref, wa_ref, wb_ref, wo_ref, o_ref, xb_ref, acc_ref):
    del te_ref
    i = pl.program_id(0)
    f = pl.program_id(1)
    tr = xb_ref.shape[0]
    n_valid = rows_ref[i]

    @pl.when((n_valid > 0) & (f == 0))
    def _():
        x = _unpack_row([x_ref[c] for c in range(SUBROWS)])
        live = lax.broadcasted_iota(jnp.int32, (tr, 1), 0) < n_valid
        xb_ref[...] = jnp.where(live, x, 0.0).astype(BF16)
        acc_ref[...] = jnp.zeros_like(acc_ref)

    @pl.when(n_valid > 0)
    def _():
        xb = xb_ref[...]
        a = _dot(xb, wa_ref[0])
        b = _dot(xb, wb_ref[0])
        hm = (a * jax.nn.sigmoid(a) * b).astype(BF16)
        acc_ref[...] += _dot(hm, wo_ref[0])

    @pl.when((n_valid > 0) & (f == pl.num_programs(1) - 1))
    def _():
        for c, sub in enumerate(_pack_row(acc_ref[...])):
            o_ref[c] = sub

    @pl.when((n_valid == 0) & (f == 0))
    def _():
        o_ref[...] = jnp.zeros_like(o_ref)


def _experts(tile_expert, tile_rows, xs, w_in, w_out, *, tr, tf):
    _, p, sw = xs.shape
    d = w_out.shape[2]
    de = w_out.shape[1]
    nf = de // tf
    rows = pl.BlockSpec((SUBROWS, tr, sw), lambda i, f, te, tv: (0, i, 0))
    return pl.pallas_call(
        _experts_kernel,
        out_shape=jax.ShapeDtypeStruct((SUBROWS, p, sw), jnp.uint32),
        grid_spec=pltpu.PrefetchScalarGridSpec(
            num_scalar_prefetch=2,
            grid=(p // tr, nf),
            in_specs=[
                rows,
                pl.BlockSpec((1, d, tf), lambda i, f, te, tv: (te[i], 0, f)),
                pl.BlockSpec((1, d, tf), lambda i, f, te, tv: (te[i], 0, nf + f)),
                pl.BlockSpec((1, tf, d), lambda i, f, te, tv: (te[i], f, 0)),
            ],
            out_specs=rows,
            scratch_shapes=[pltpu.VMEM((tr, d), BF16), pltpu.VMEM((tr, d), F32)],
        ),
        compiler_params=_params(("arbitrary", "arbitrary")),
        name="experts",
    )(tile_expert, tile_rows, xs, w_in, w_in, w_out)


def _combine_kernel(g_ref, h_ref, w_ref, o_ref):
    w = w_ref[...]
    y = [_unpack_row([g_ref[s, c] for c in range(SUBROWS)]) for s in range(TOP_K)]
    o_ref[...] = h_ref[...] + (w[:, 0:1] * y[0] + w[:, 1:2] * y[1])


def _combine(g, h, w, *, tc):
    n, d = h.shape
    sw = g.shape[3]
    tok = pl.BlockSpec((tc, d), lambda i: (i, 0))
    return pl.pallas_call(
        _combine_kernel,
        out_shape=jax.ShapeDtypeStruct((n, d), F32),
        grid=(n // tc,),
        in_specs=[pl.BlockSpec((TOP_K, SUBROWS, tc, sw), lambda i: (0, 0, i, 0)), tok,
                  pl.BlockSpec((tc, TOP_K), lambda i: (i, 0))],
        out_specs=tok,
        compiler_params=_params(("parallel",)),
        name="combine",
    )(g, h, w)


def _tiles(n, seq):
    def pick(limit, of):
        t = limit
        while of % t:
            t //= 2
        return t
    return dict(
        tm_a=pick(256, seq), tm_f=pick(512, n), tm_qkv=pick(512, seq), blk=pick(256, seq),
        tm_o=pick(512, n), tr=pick(1024, TOP_K * n), tc=pick(512, n))


def kernel(x, a_norm_g, a_w_in, a_v_norm_g, a_w_spatial, a_b_spatial, a_w_out, f_norm_g, f_w_in, f_w_out, kv_norm_g, kv_w, kv_b_f, k_norm_g, b_norm_g, b_w_in, q_norm_g, b_w_out, m_norm_g, m_w_router, m_w_in, m_w_out):
    batch, seq, d = x.shape
    n = batch * seq
    nh = d // HEAD_DIM
    ne = m_w_router.shape[-1]
    assert a_w_in.shape[0] == 1 and b_w_in.shape[0] == 1 and f_w_in.shape[0] == 1 and m_w_in.shape[0] == 1
    assert seq % GMLP_CHUNK == 0 and d % LANES == 0 and (SUBROWS * n) % SC_WINDOW == 0
    t = _tiles(n, seq)
    row = lambda g: g.reshape(1, -1)

    h = x.reshape(n, d)
    h = _mixer_a(h, row(a_norm_g[0]), a_w_in[0].astype(BF16), row(a_v_norm_g[0]), a_w_spatial[0],
                 a_b_spatial[0].T, a_w_out[0].astype(BF16), tm=t["tm_a"])
    h = _swiglu(h, row(f_norm_g[0]), f_w_in[0].astype(BF16), f_w_out[0].astype(BF16), tm=t["tm_f"])

    head = jnp.arange(d, dtype=jnp.int32) // HEAD_DIM
    hsum = (head[:, None] == head[None, :]).astype(BF16)
    q, k, vt, sg, fp = _qkv(
        h, row(kv_norm_g), row(b_norm_g[0]), kv_w[:, :d].astype(BF16),
        kv_w[:, d:2 * d].T.astype(BF16), kv_w[:, 2 * d:].T.astype(BF16), kv_b_f.reshape(nh, 1),
        row(jnp.tile(k_norm_g, nh)), row(jnp.tile(q_norm_g[0], nh)), b_w_in[0].astype(BF16), hsum,
        tm=t["tm_qkv"], batch=batch, seq=seq)
    og = _attention(q, k, vt, jnp.swapaxes(fp, 1, 2), sg, batch=batch, seq=seq, blk=t["blk"])

    h2, xp, meta, cnt = _out_router(og, h, b_w_out[0].astype(BF16), row(m_norm_g[0]),
                                    m_w_router[0].T, tm=t["tm_o"])

    tr = t["tr"]
    n_tiles = TOP_K * n // tr + ne
    p = n_tiles * tr
    counts = cnt[:, 0].astype(jnp.int32)
    tiles_per_expert = (counts + tr - 1) // tr
    tile_end = jnp.cumsum(tiles_per_expert)
    tile_start = tile_end - tiles_per_expert
    expert_ids = jnp.arange(ne, dtype=jnp.int32)
    idx = meta[0:TOP_K].astype(jnp.int32)
    start_of = jnp.sum(jnp.where(idx[:, :, None] == expert_ids, tile_start * tr, 0), axis=-1)
    dest = start_of + meta[4:4 + TOP_K].astype(jnp.int32)
    tile_ids = jnp.arange(n_tiles, dtype=jnp.int32)
    tile_expert = jnp.minimum(
        jnp.sum((tile_ids[:, None] >= tile_end[None, :]).astype(jnp.int32), axis=1), ne - 1)
    mine = tile_expert[:, None] == expert_ids[None, :]
    tile_rows = jnp.clip(
        jnp.sum(jnp.where(mine, counts - (tile_ids[:, None] - tile_start) * tr, 0), axis=1), 0, tr)
    tile_rows = jnp.where(tile_ids < tile_end[-1], tile_rows, 0).astype(jnp.int32)

    sub = (jnp.arange(SUBROWS, dtype=jnp.int32) * p)[None, :, None]
    sub_dest = sub + dest[:, None, :]
    sw = xp.shape[2]
    xs = _sc_scatter(xp.reshape(SUBROWS * n, sw), sub_dest.reshape(TOP_K, SUBROWS * n), SUBROWS * p)
    eo = _experts(tile_expert, tile_rows, xs.reshape(SUBROWS, p, sw), m_w_in[0].astype(BF16),
                  m_w_out[0].astype(BF16), tr=tr, tf=m_w_out.shape[2] // 4)
    g = _sc_gather(eo.reshape(SUBROWS * p, sw), sub_dest.reshape(1, TOP_K * SUBROWS * n))
    out = _combine(g.reshape(TOP_K, SUBROWS, n, sw), h2, meta[2:2 + TOP_K].T, tc=t["tc"])
    return out.reshape(batch, seq, d)
```

```python
import functools

import jax
import jax.numpy as jnp
from jax import lax
from jax.experimental import pallas as pl
from jax.experimental.pallas import tpu as pltpu
from jax.experimental.pallas import tpu_sc as plsc

RMS_EPS = 1e-6
NEG_INF = -1e30
LOG2E = 1.4426950408889634
N_PIECES = 3
SC_WINDOW = 128
SUBROWS = 2
GMLP_CHUNK = 128
CAUSAL_CHUNK = 64
A_GROUPS = 8
HEAD_DIM = 64
LANES = 128
TOP_K = 2
VMEM_LIMIT = 56 * 1024 * 1024

BF16 = jnp.bfloat16
F32 = jnp.float32


def _dot(a, b, **kw):
    return jnp.dot(a, b, preferred_element_type=F32, **kw)


def _dot_nt(a, b, **kw):
    return lax.dot_general(a, b, (((1,), (1,)), ((), ())), preferred_element_type=F32, **kw)


def _rms(x, g):
    return x * lax.rsqrt(jnp.mean(x * x, axis=-1, keepdims=True) + RMS_EPS) * g


def _gelu(x):
    return 0.5 * x * (1.0 + lax.erf(x * (2.0 ** -0.5)))


def _const_spec(shape):
    nd = len(shape)
    return pl.BlockSpec(shape, lambda *_: (0,) * nd, pipeline_mode=pl.Buffered(1))


def _params(sem):
    return pltpu.CompilerParams(dimension_semantics=sem, vmem_limit_bytes=VMEM_LIMIT)


def _mixer_a_kernel(x_ref, g_ref, win_ref, gv_ref, ws_ref, bs_ref, wout_ref, o_ref, z_ref):
    tm = x_ref.shape[0]
    half = wout_ref.shape[0]
    gd = half // A_GROUPS
    x = x_ref[...]
    xb = _rms(x, g_ref[...]).astype(BF16)
    v = _gelu(_dot(xb, win_ref[:, half:]))
    v = (_rms(v, gv_ref[...])).astype(BF16)
    u = _gelu(_dot(xb, win_ref[:, :half]))
    row = lax.broadcasted_iota(jnp.int32, (GMLP_CHUNK, GMLP_CHUNK), 0)
    col = lax.broadcasted_iota(jnp.int32, (GMLP_CHUNK, GMLP_CHUNK), 1)
    keep = (col // CAUSAL_CHUNK) <= (row // CAUSAL_CHUNK)
    bs = bs_ref[...]
    for g in range(A_GROUPS):
        wg = jnp.where(keep, ws_ref[g], 0.0).astype(BF16)
        bg = bs[:, g:g + 1]
        for c in range(tm // GMLP_CHUNK):
            rs = slice(c * GMLP_CHUNK, (c + 1) * GMLP_CHUNK)
            cs = slice(g * gd, (g + 1) * gd)
            sv = _dot(wg, v[rs, cs]) + bg
            z_ref[rs, cs] = (u[rs, cs] * sv).astype(BF16)
    o_ref[...] = x + _dot(z_ref[...], wout_ref[...])


def _mixer_a(h, g, w_in, gv, ws, bs_t, w_out, *, tm):
    n, d = h.shape
    half = w_out.shape[0]
    return pl.pallas_call(
        _mixer_a_kernel,
        out_shape=jax.ShapeDtypeStruct((n, d), F32),
        grid=(n // tm,),
        in_specs=[
            pl.BlockSpec((tm, d), lambda i: (i, 0)),
            _const_spec(g.shape), _const_spec(w_in.shape), _const_spec(gv.shape),
            _const_spec(ws.shape), _const_spec(bs_t.shape), _const_spec(w_out.shape),
        ],
        out_specs=pl.BlockSpec((tm, d), lambda i: (i, 0)),
        scratch_shapes=[pltpu.VMEM((tm, half), BF16)],
        compiler_params=_params(("parallel",)),
        name="mixer_a",
    )(h, g, w_in, gv, ws, bs_t, w_out)


def _swiglu_kernel(x_ref, g_ref, win_ref, wout_ref, o_ref):
    f = wout_ref.shape[0]
    x = x_ref[...]
    xb = _rms(x, g_ref[...]).astype(BF16)
    a = _dot(xb, win_ref[:, :f])
    b = _dot(xb, win_ref[:, f:])
    hm = (a * jax.nn.sigmoid(a) * b).astype(BF16)
    o_ref[...] = x + _dot(hm, wout_ref[...])


def _swiglu(h, g, w_in, w_out, *, tm):
    n, d = h.shape
    return pl.pallas_call(
        _swiglu_kernel,
        out_shape=jax.ShapeDtypeStruct((n, d), F32),
        grid=(n // tm,),
        in_specs=[
            pl.BlockSpec((tm, d), lambda i: (i, 0)),
            _const_spec(g.shape), _const_spec(w_in.shape), _const_spec(w_out.shape),
        ],
        out_specs=pl.BlockSpec((tm, d), lambda i: (i, 0)),
        compiler_params=_params(("parallel",)),
        name="swiglu",
    )(h, g, w_in, w_out)


def _lane_cumsum(x):
    n = x.shape[-1]
    lane = lax.broadcasted_iota(jnp.int32, x.shape, x.ndim - 1)
    sh = 1
    while sh < n:
        x = x + jnp.where(lane >= sh, pltpu.roll(x, sh, axis=x.ndim - 1), 0.0)
        sh *= 2
    return x


def _split_bf16(x):
    pieces = []
    for _ in range(N_PIECES):
        p = x.astype(BF16)
        pieces.append(p)
        x = x - p.astype(F32)
    return pieces


def _qkv_kernel(x_ref, gkv_ref, gb_ref, wk_ref, wvt_ref, wft_ref, bf_ref, gk_ref, gq_ref, wqg_ref,
                hsum_ref, q_ref, k_ref, vt_ref, sg_ref, fp_ref, carry_ref, *, tiles_per_seq):
    d = x_ref.shape[1]
    i = pl.program_id(0)
    x = x_ref[...]
    y = x * lax.rsqrt(jnp.mean(x * x, axis=-1, keepdims=True) + RMS_EPS)
    skv = (y * gkv_ref[...]).astype(BF16)
    sb = (y * gb_ref[...]).astype(BF16)
    hsum = hsum_ref[...]

    def head_norm(t, gain):
        ssq = _dot((t * t).astype(BF16), hsum)
        return t * lax.rsqrt(ssq * (1.0 / HEAD_DIM) + RMS_EPS) * gain

    k = _dot(skv, wk_ref[...])
    k_ref[...] = head_norm(k, gk_ref[...]).astype(BF16)
    vt_ref[0] = _dot_nt(wvt_ref[...], skv).astype(BF16)
    q = _dot(sb, wqg_ref[:, :d])
    q_ref[...] = (head_norm(q, gq_ref[...]) * (LOG2E * HEAD_DIM ** -0.5)).astype(BF16)
    sg_ref[...] = jax.nn.sigmoid(_dot(sb, wqg_ref[:, d:])).astype(BF16)

    f = _dot_nt(wft_ref[...], skv) + bf_ref[...]
    logf = jax.nn.log_sigmoid(f)

    @pl.when(i % tiles_per_seq == 0)
    def _():
        carry_ref[...] = jnp.zeros_like(carry_ref)

    cum = _lane_cumsum(logf) + carry_ref[:, 0:1]
    carry_ref[...] = jnp.broadcast_to(cum[:, -1:], carry_ref.shape)
    fp_ref[0] = jnp.concatenate(_split_bf16(cum * LOG2E), axis=0)


def _qkv(h, gkv, gb, wk, wvt, wft, bf, gk, gq, wqg, hsum, *, tm, batch, seq):
    n, d = h.shape
    nh = wft.shape[0]
    tps = seq // tm
    tok = pl.BlockSpec((tm, d), lambda i: (i, 0))
    tok_bf = jax.ShapeDtypeStruct((n, d), BF16)
    seq_map = lambda i: (i // tps, 0, i % tps)
    consts = (gkv, gb, wk, wvt, wft, bf, gk, gq, wqg, hsum)
    return pl.pallas_call(
        functools.partial(_qkv_kernel, tiles_per_seq=tps),
        out_shape=[tok_bf, tok_bf, jax.ShapeDtypeStruct((batch, d, seq), BF16), tok_bf,
                   jax.ShapeDtypeStruct((batch, N_PIECES * nh, seq), BF16)],
        grid=(n // tm,),
        in_specs=[tok] + [_const_spec(a.shape) for a in consts],
        out_specs=[tok, tok, pl.BlockSpec((1, d, tm), seq_map), tok,
                   pl.BlockSpec((1, N_PIECES * nh, tm), seq_map)],
        scratch_shapes=[pltpu.VMEM((nh, LANES), F32)],
        compiler_params=_params(("arbitrary",)),
        name="qkv",
    )(h, *consts)


def _attention_kernel(q_ref, k_ref, vt_ref, fp_ref, sg_ref, o_ref, kaug_ref, qaug_ref, *, blk, nq):
    hp = pl.program_id(1)
    nh = fp_ref.shape[2] // N_PIECES
    lane = lax.broadcasted_iota(jnp.int32, (1, LANES), 1)
    own = (lane < HEAD_DIM, lane >= HEAD_DIM)
    spare = (HEAD_DIM, 0)
    q_lanes = [(lane >= s) & (lane < s + N_PIECES) for s in spare]
    k_lanes = [(lane >= s + N_PIECES) & (lane < s + 2 * N_PIECES) for s in spare]

    er = lax.broadcasted_iota(jnp.int32, (N_PIECES * nh, LANES), 0)
    ec = lax.broadcasted_iota(jnp.int32, (N_PIECES * nh, LANES), 1)
    place = jnp.zeros((N_PIECES * nh, LANES), F32)
    for h in range(2):
        for j in range(N_PIECES):
            src = er == j * nh + 2 * hp + h
            place = (place + (src & (ec == spare[h] + j)).astype(F32)
                     - (src & (ec == spare[h] + N_PIECES + j)).astype(F32))
    g = _dot(fp_ref[0], place.astype(BF16))
    for h in range(2):
        k_fill = jnp.where(k_lanes[h], g, q_lanes[h].astype(F32)).astype(BF16)
        kaug_ref[h] = jnp.where(own[h], k_ref[...], k_fill)
        q_fill = jnp.where(q_lanes[h], g, k_lanes[h].astype(F32)).astype(BF16)
        qaug_ref[h] = jnp.where(own[h], q_ref[...], q_fill)

    first_rows = lax.broadcasted_iota(jnp.int32, (LANES, 1), 0) < HEAD_DIM
    key = lax.broadcasted_iota(jnp.int32, (blk, blk), 0)
    qry = lax.broadcasted_iota(jnp.int32, (blk, blk), 1)
    causal = qry >= key

    for i in range(nq):
        lo, hi = i * blk, (i + 1) * blk
        vt = vt_ref[0, :, 0:hi]
        scores = [_dot_nt(kaug_ref[h, 0:hi, :], qaug_ref[h, lo:hi, :]) for h in range(2)]
        probs = []
        for s in scores:
            diag = jnp.where(causal, s[lo:hi], NEG_INF)
            s = diag if i == 0 else jnp.concatenate([s[0:lo], diag], axis=0)
            p = jnp.exp2(s - jnp.max(s, axis=0, keepdims=True))
            probs.append((p.astype(BF16), jnp.sum(p, axis=0, keepdims=True)))
        outs = [_dot(vt, p) / l for p, l in probs]
        o = jnp.where(first_rows, outs[0], outs[1]).T
        o_ref[lo:hi, :] = (o * sg_ref[lo:hi, :].astype(F32)).astype(BF16)


def _attention(q, k, vt, fp, sg, *, batch, seq, blk):
    n, d = q.shape
    pair = pl.BlockSpec((seq, LANES), lambda b, hp: (b, hp))
    return pl.pallas_call(
        functools.partial(_attention_kernel, blk=blk, nq=seq // blk),
        out_shape=jax.ShapeDtypeStruct((n, d), BF16),
        grid=(batch, d // LANES),
        in_specs=[
            pair, pair,
            pl.BlockSpec((1, LANES, seq), lambda b, hp: (b, hp, 0)),
            pl.BlockSpec((1, seq, fp.shape[2]), lambda b, hp: (b, 0, 0)),
            pair,
        ],
        out_specs=pair,
        scratch_shapes=[pltpu.VMEM((2, seq, LANES), BF16)] * 2,
        compiler_params=_params(("parallel", "parallel")),
        name="attention",
    )(q, k, vt, fp, sg)


def _pack_bf16_pair(lo, hi):
    ulo = lax.bitcast_convert_type(lo.astype(BF16).astype(F32), jnp.uint32)
    uhi = lax.bitcast_convert_type(hi.astype(BF16).astype(F32), jnp.uint32)
    return (ulo >> 16) | uhi


def _pack_row(x):
    half = x.shape[1] // 2
    w = _pack_bf16_pair(x[:, :half], x[:, half:])
    sw = half // SUBROWS
    return [w[:, c * sw:(c + 1) * sw] for c in range(SUBROWS)]


def _unpack_row(subrows):
    lo = [lax.bitcast_convert_type(p << 16, F32) for p in subrows]
    hi = [lax.bitcast_convert_type(p & jnp.uint32(0xFFFF0000), F32) for p in subrows]
    return jnp.concatenate(lo + hi, axis=1)


def _out_router_kernel(og_ref, h_ref, wo_ref, gm_ref, wrt_ref, h2_ref, xp_ref, meta_ref, cnt_ref,
                       carry_ref):
    i = pl.program_id(0)
    ne, tm = meta_ref.shape
    h2 = h_ref[...] + _dot(og_ref[...], wo_ref[...])
    h2_ref[...] = h2
    xn = _rms(h2, gm_ref[...])
    for c, sub in enumerate(_pack_row(xn)):
        xp_ref[c] = sub
    logits = _dot_nt(wrt_ref[...], xn, precision=lax.Precision.HIGHEST)
    row = lax.broadcasted_iota(jnp.int32, (ne, tm), 0).astype(F32)
    v1 = jnp.max(logits, axis=0, keepdims=True)
    i1 = jnp.min(jnp.where(logits == v1, row, ne), axis=0, keepdims=True)
    rest = jnp.where(row == i1, -jnp.inf, logits)
    v2 = jnp.max(rest, axis=0, keepdims=True)
    i2 = jnp.min(jnp.where(rest == v2, row, ne), axis=0, keepdims=True)
    e = jnp.exp(v2 - v1)
    w1 = 1.0 / (1.0 + e)
    w2 = e / (1.0 + e)
    sel1 = row == i1
    sel2 = row == i2
    oh = (sel1 | sel2).astype(F32)

    @pl.when(i == 0)
    def _():
        carry_ref[...] = jnp.zeros_like(carry_ref)

    base = carry_ref[:, 0:1]
    rank = base + (_lane_cumsum(oh) - oh)
    r1 = jnp.sum(jnp.where(sel1, rank, 0.0), axis=0, keepdims=True)
    r2 = jnp.sum(jnp.where(sel2, rank, 0.0), axis=0, keepdims=True)
    total = base + jnp.sum(oh, axis=1, keepdims=True)
    carry_ref[...] = jnp.broadcast_to(total, carry_ref.shape)
    cnt_ref[...] = jnp.broadcast_to(total, cnt_ref.shape)
    zeros = jnp.zeros_like(w1)
    meta_ref[...] = jnp.concatenate(
        [i1.astype(F32), i2.astype(F32), w1, w2, r1, r2, zeros, zeros], axis=0)


def _out_router(og, h, wo, gm, wrt, *, tm):
    n, d = h.shape
    ne = wrt.shape[0]
    sw = d // 2 // SUBROWS
    tok = pl.BlockSpec((tm, d), lambda i: (i, 0))
    return pl.pallas_call(
        _out_router_kernel,
        out_shape=[jax.ShapeDtypeStruct((n, d), F32), jax.ShapeDtypeStruct((SUBROWS, n, sw), jnp.uint32),
                   jax.ShapeDtypeStruct((ne, n), F32), jax.ShapeDtypeStruct((ne, LANES), F32)],
        grid=(n // tm,),
        in_specs=[tok, tok, _const_spec(wo.shape), _const_spec(gm.shape), _const_spec(wrt.shape)],
        out_specs=[tok, pl.BlockSpec((SUBROWS, tm, sw), lambda i: (0, i, 0)),
                   pl.BlockSpec((ne, tm), lambda i: (0, i)),
                   pl.BlockSpec((ne, LANES), lambda i: (0, 0))],
        scratch_shapes=[pltpu.VMEM((ne, LANES), F32)],
        compiler_params=_params(("arbitrary",)),
        name="out_router",
    )(og, h, wo, gm, wrt)


def _sc_mesh():
    return plsc.VectorSubcoreMesh(core_axis_name="core", subcore_axis_name="subcore")


def _sc_scatter(x, idx, out_rows):
    rows, w = x.shape
    n_idx = idx.shape[0]

    @pl.kernel(out_type=jax.ShapeDtypeStruct((out_rows, w), x.dtype), mesh=_sc_mesh(),
               scratch_types=[])
    def scatter_rows(x_hbm, i_hbm, o_hbm):
        def body(x_vmem, i_vmem):
            for s in range(n_idx):
                pltpu.sync_copy(x_vmem, o_hbm.at[i_vmem.at[s]])

        pltpu.emit_pipeline(
            body, grid=(rows // SC_WINDOW,),
            in_specs=[pl.BlockSpec((SC_WINDOW, w), lambda i: (i, 0)),
                      pl.BlockSpec((n_idx, SC_WINDOW), lambda i: (0, i))],
            out_specs=[],
            core_axis_name=("core", "subcore"),
            dimension_semantics=(pltpu.PARALLEL,),
        )(x_hbm, i_hbm)

    return scatter_rows(x, idx)


def _sc_gather(x, idx):
    n = idx.shape[1]
    w = x.shape[1]

    @pl.kernel(out_type=jax.ShapeDtypeStruct((n, w), x.dtype), mesh=_sc_mesh(), scratch_types=[])
    def gather_rows(x_hbm, i_hbm, o_hbm):
        def body(i_vmem, o_vmem):
            pltpu.sync_copy(x_hbm.at[i_vmem.at[0]], o_vmem)

        pltpu.emit_pipeline(
            body, grid=(n // SC_WINDOW,),
            in_specs=[pl.BlockSpec((1, SC_WINDOW), lambda i: (0, i))],
            out_specs=[pl.BlockSpec((SC_WINDOW, w), lambda i: (i, 0))],
            core_axis_name=("core", "subcore"),
            dimension_semantics=(pltpu.PARALLEL,),
        )(i_hbm, o_hbm)

    return gather_rows(x, idx)


def _experts_kernel(te_ref, rows_ref, x_ref, wa_ref, wb_ref, wo_ref, o_ref, xb_ref, hm_ref, *, nf):
    del te_ref
    i = pl.program_id(0)
    f = pl.program_id(1)
    tr = xb_ref.shape[0]
    tf = wa_ref.shape[2]
    n_valid = rows_ref[i]

    @pl.when((n_valid > 0) & (f == 0))
    def _():
        x = _unpack_row([x_ref[c] for c in range(SUBROWS)])
        live = lax.broadcasted_iota(jnp.int32, (tr, 1), 0) < n_valid
        xb_ref[...] = jnp.where(live, x, 0.0).astype(BF16)

    for j in range(nf):
        @pl.when((n_valid > 0) & (f == j))
        def _(j=j):
            xb = xb_ref[...]
            a = _dot(xb, wa_ref[0])
            b = _dot(xb, wb_ref[0])
            hm_ref[:, j * tf:(j + 1) * tf] = (a * jax.nn.sigmoid(a) * b).astype(BF16)

    @pl.when((n_valid > 0) & (f == nf - 1))
    def _():
        for c, sub in enumerate(_pack_row(_dot(hm_ref[...], wo_ref[0]))):
            o_ref[c] = sub

    @pl.when((n_valid == 0) & (f == 0))
    def _():
        o_ref[...] = jnp.zeros_like(o_ref)


def _experts(tile_expert, tile_rows, xs, w_in, w_out, *, tr, tf):
    _, p, sw = xs.shape
    d = w_out.shape[2]
    de = w_out.shape[1]
    nf = de // tf
    rows = pl.BlockSpec((SUBROWS, tr, sw), lambda i, f, te, tv: (0, i, 0))
    return pl.pallas_call(
        functools.partial(_experts_kernel, nf=nf),
        out_shape=jax.ShapeDtypeStruct((SUBROWS, p, sw), jnp.uint32),
        grid_spec=pltpu.PrefetchScalarGridSpec(
            num_scalar_prefetch=2,
            grid=(p // tr, nf),
            in_specs=[
                rows,
                pl.BlockSpec((1, d, tf), lambda i, f, te, tv: (te[i], 0, f)),
                pl.BlockSpec((1, d, tf), lambda i, f, te, tv: (te[i], 0, nf + f)),
                pl.BlockSpec((1, de, d), lambda i, f, te, tv: (te[i], 0, 0),
                             pipeline_mode=pl.Buffered(1)),
            ],
            out_specs=rows,
            scratch_shapes=[pltpu.VMEM((tr, d), BF16), pltpu.VMEM((tr, de), BF16)],
        ),
        compiler_params=_params(("arbitrary", "arbitrary")),
        name="experts",
    )(tile_expert, tile_rows, xs, w_in, w_in, w_out)


def _combine_kernel(g_ref, h_ref, w_ref, o_ref):
    w = w_ref[...]
    y = [_unpack_row([g_ref[s, c] for c in range(SUBROWS)]) for s in range(TOP_K)]
    o_ref[...] = h_ref[...] + (w[:, 0:1] * y[0] + w[:, 1:2] * y[1])


def _combine(g, h, w, *, tc):
    n, d = h.shape
    sw = g.shape[3]
    tok = pl.BlockSpec((tc, d), lambda i: (i, 0))
    return pl.pallas_call(
        _combine_kernel,
        out_shape=jax.ShapeDtypeStruct((n, d), F32),
        grid=(n // tc,),
        in_specs=[pl.BlockSpec((TOP_K, SUBROWS, tc, sw), lambda i: (0, 0, i, 0)), tok,
                  pl.BlockSpec((tc, TOP_K), lambda i: (i, 0))],
        out_specs=tok,
        compiler_params=_params(("parallel",)),
        name="combine",
    )(g, h, w)


def _tiles(n, seq):
    def pick(limit, of):
        t = limit
        while of % t:
            t //= 2
        return t
    return dict(
        tm_a=pick(512, seq), tm_f=pick(512, n), tm_qkv=pick(512, seq), blk=pick(256, seq),
        tm_o=pick(512, n), tr=pick(1024, TOP_K * n), tc=pick(512, n))


def kernel(x, a_norm_g, a_w_in, a_v_norm_g, a_w_spatial, a_b_spatial, a_w_out, f_norm_g, f_w_in, f_w_out, kv_norm_g, kv_w, kv_b_f, k_norm_g, b_norm_g, b_w_in, q_norm_g, b_w_out, m_norm_g, m_w_router, m_w_in, m_w_out):
    batch, seq, d = x.shape
    n = batch * seq
    nh = d // HEAD_DIM
    ne = m_w_router.shape[-1]
    assert a_w_in.shape[0] == 1 and b_w_in.shape[0] == 1 and f_w_in.shape[0] == 1 and m_w_in.shape[0] == 1
    assert seq % GMLP_CHUNK == 0 and d % LANES == 0 and (SUBROWS * n) % SC_WINDOW == 0
    t = _tiles(n, seq)
    row = lambda g: g.reshape(1, -1)

    h = x.reshape(n, d)
    h = _mixer_a(h, row(a_norm_g[0]), a_w_in[0].astype(BF16), row(a_v_norm_g[0]), a_w_spatial[0],
                 a_b_spatial[0].T, a_w_out[0].astype(BF16), tm=t["tm_a"])
    h = _swiglu(h, row(f_norm_g[0]), f_w_in[0].astype(BF16), f_w_out[0].astype(BF16), tm=t["tm_f"])

    head = jnp.arange(d, dtype=jnp.int32) // HEAD_DIM
    hsum = (head[:, None] == head[None, :]).astype(BF16)
    q, k, vt, sg, fp = _qkv(
        h, row(kv_norm_g), row(b_norm_g[0]), kv_w[:, :d].astype(BF16),
        kv_w[:, d:2 * d].T.astype(BF16), kv_w[:, 2 * d:].T.astype(BF16), kv_b_f.reshape(nh, 1),
        row(jnp.tile(k_norm_g, nh)), row(jnp.tile(q_norm_g[0], nh)), b_w_in[0].astype(BF16), hsum,
        tm=t["tm_qkv"], batch=batch, seq=seq)
    og = _attention(q, k, vt, jnp.swapaxes(fp, 1, 2), sg, batch=batch, seq=seq, blk=t["blk"])

    h2, xp, meta, cnt = _out_router(og, h, b_w_out[0].astype(BF16), row(m_norm_g[0]),
                                    m_w_router[0].T, tm=t["tm_o"])

    tr = t["tr"]
    n_tiles = TOP_K * n // tr + ne
    p = n_tiles * tr
    counts = cnt[:, 0].astype(jnp.int32)
    tiles_per_expert = (counts + tr - 1) // tr
    tile_end = jnp.cumsum(tiles_per_expert)
    tile_start = tile_end - tiles_per_expert
    expert_ids = jnp.arange(ne, dtype=jnp.int32)
    idx = meta[0:TOP_K].astype(jnp.int32)
    start_of = jnp.sum(jnp.where(idx[:, :, None] == expert_ids, tile_start * tr, 0), axis=-1)
    dest = start_of + meta[4:4 + TOP_K].astype(jnp.int32)
    tile_ids = jnp.arange(n_tiles, dtype=jnp.int32)
    tile_expert = jnp.minimum(
        jnp.sum((tile_ids[:, None] >= tile_end[None, :]).astype(jnp.int32), axis=1), ne - 1)
    mine = tile_expert[:, None] == expert_ids[None, :]
    tile_rows = jnp.clip(
        jnp.sum(jnp.where(mine, counts - (tile_ids[:, None] - tile_start) * tr, 0), axis=1), 0, tr)
    tile_rows = jnp.where(tile_ids < tile_end[-1], tile_rows, 0).astype(jnp.int32)

    sub = (jnp.arange(SUBROWS, dtype=jnp.int32) * p)[None, :, None]
    sub_dest = sub + dest[:, None, :]
    sw = xp.shape[2]
    xs = _sc_scatter(xp.reshape(SUBROWS * n, sw), sub_dest.reshape(TOP_K, SUBROWS * n), SUBROWS * p)
    eo = _experts(tile_expert, tile_rows, xs.reshape(SUBROWS, p, sw), m_w_in[0].astype(BF16),
                  m_w_out[0].astype(BF16), tr=tr, tf=m_w_out.shape[2] // 4)
    g = _sc_gather(eo.reshape(SUBROWS * p, sw), sub_dest.reshape(1, TOP_K * SUBROWS * n))
    out = _combine(g.reshape(TOP_K, SUBROWS, n, sw), h2, meta[2:2 + TOP_K].T, tc=t["tc"])
    return out.reshape(batch, seq, d)
```

```python
import functools

import jax
import jax.numpy as jnp
from jax import lax
from jax.experimental import pallas as pl
from jax.experimental.pallas import tpu as pltpu
from jax.experimental.pallas import tpu_sc as plsc

RMS_EPS = 1e-6
NEG_INF = -1e30
LOG2E = 1.4426950408889634
N_PIECES = 3
SC_WINDOW = 128
SUBROWS = 2
GMLP_CHUNK = 128
CAUSAL_CHUNK = 64
A_GROUPS = 8
HEAD_DIM = 64
LANES = 128
TOP_K = 2
VMEM_LIMIT = 56 * 1024 * 1024

BF16 = jnp.bfloat16
F32 = jnp.float32


def _dot(a, b, **kw):
    return jnp.dot(a, b, preferred_element_type=F32, **kw)


def _dot_nt(a, b, **kw):
    return lax.dot_general(a, b, (((1,), (1,)), ((), ())), preferred_element_type=F32, **kw)


def _rms(x, g):
    return x * lax.rsqrt(jnp.mean(x * x, axis=-1, keepdims=True) + RMS_EPS) * g


def _gelu(x):
    return 0.5 * x * (1.0 + lax.erf(x * (2.0 ** -0.5)))


def _const_spec(shape):
    nd = len(shape)
    return pl.BlockSpec(shape, lambda *_: (0,) * nd, pipeline_mode=pl.Buffered(1))


def _params(sem):
    return pltpu.CompilerParams(dimension_semantics=sem, vmem_limit_bytes=VMEM_LIMIT)


def _mixer_a_kernel(x_ref, g_ref, win_ref, gv_ref, ws_ref, bs_ref, wout_ref, o_ref, z_ref):
    tm = x_ref.shape[0]
    half = wout_ref.shape[0]
    gd = half // A_GROUPS
    x = x_ref[...]
    xb = _rms(x, g_ref[...]).astype(BF16)
    v = _gelu(_dot(xb, win_ref[:, half:]))
    v = (_rms(v, gv_ref[...])).astype(BF16)
    u = _gelu(_dot(xb, win_ref[:, :half]))
    row = lax.broadcasted_iota(jnp.int32, (GMLP_CHUNK, GMLP_CHUNK), 0)
    col = lax.broadcasted_iota(jnp.int32, (GMLP_CHUNK, GMLP_CHUNK), 1)
    keep = (col // CAUSAL_CHUNK) <= (row // CAUSAL_CHUNK)
    bs = bs_ref[...]
    for g in range(A_GROUPS):
        wg = jnp.where(keep, ws_ref[g], 0.0).astype(BF16)
        bg = bs[:, g:g + 1]
        for c in range(tm // GMLP_CHUNK):
            rs = slice(c * GMLP_CHUNK, (c + 1) * GMLP_CHUNK)
            cs = slice(g * gd, (g + 1) * gd)
            sv = _dot(wg, v[rs, cs]) + bg
            z_ref[rs, cs] = (u[rs, cs] * sv).astype(BF16)
    o_ref[...] = x + _dot(z_ref[...], wout_ref[...])


def _mixer_a(h, g, w_in, gv, ws, bs_t, w_out, *, tm):
    n, d = h.shape
    half = w_out.shape[0]
    return pl.pallas_call(
        _mixer_a_kernel,
        out_shape=jax.ShapeDtypeStruct((n, d), F32),
        grid=(n // tm,),
        in_specs=[
            pl.BlockSpec((tm, d), lambda i: (i, 0)),
            _const_spec(g.shape), _const_spec(w_in.shape), _const_spec(gv.shape),
            _const_spec(ws.shape), _const_spec(bs_t.shape), _const_spec(w_out.shape),
        ],
        out_specs=pl.BlockSpec((tm, d), lambda i: (i, 0)),
        scratch_shapes=[pltpu.VMEM((tm, half), BF16)],
        compiler_params=_params(("parallel",)),
        name="mixer_a",
    )(h, g, w_in, gv, ws, bs_t, w_out)


def _swiglu_kernel(x_ref, g_ref, win_ref, wout_ref, o_ref):
    f = wout_ref.shape[0]
    x = x_ref[...]
    xb = _rms(x, g_ref[...]).astype(BF16)
    a = _dot(xb, win_ref[:, :f])
    b = _dot(xb, win_ref[:, f:])
    hm = (a * jax.nn.sigmoid(a) * b).astype(BF16)
    o_ref[...] = x + _dot(hm, wout_ref[...])


def _swiglu(h, g, w_in, w_out, *, tm):
    n, d = h.shape
    return pl.pallas_call(
        _swiglu_kernel,
        out_shape=jax.ShapeDtypeStruct((n, d), F32),
        grid=(n // tm,),
        in_specs=[
            pl.BlockSpec((tm, d), lambda i: (i, 0)),
            _const_spec(g.shape), _const_spec(w_in.shape), _const_spec(w_out.shape),
        ],
        out_specs=pl.BlockSpec((tm, d), lambda i: (i, 0)),
        compiler_params=_params(("parallel",)),
        name="swiglu",
    )(h, g, w_in, w_out)


def _lane_cumsum(x):
    n = x.shape[-1]
    lane = lax.broadcasted_iota(jnp.int32, x.shape, x.ndim - 1)
    sh = 1
    while sh < n:
        x = x + jnp.where(lane >= sh, pltpu.roll(x, sh, axis=x.ndim - 1), 0.0)
        sh *= 2
    return x


def _split_bf16(x):
    pieces = []
    for _ in range(N_PIECES):
        p = x.astype(BF16)
        pieces.append(p)
        x = x - p.astype(F32)
    return pieces


def _qkv_kernel(x_ref, gkv_ref, gb_ref, wk_ref, wvt_ref, wft_ref, bf_ref, gk_ref, gq_ref, wqg_ref,
                hsum_ref, q_ref, k_ref, vt_ref, sg_ref, fp_ref, carry_ref, *, tiles_per_seq):
    d = x_ref.shape[1]
    i = pl.program_id(0)
    x = x_ref[...]
    y = x * lax.rsqrt(jnp.mean(x * x, axis=-1, keepdims=True) + RMS_EPS)
    skv = (y * gkv_ref[...]).astype(BF16)
    sb = (y * gb_ref[...]).astype(BF16)
    hsum = hsum_ref[...]

    def head_norm(t, gain):
        ssq = _dot((t * t).astype(BF16), hsum)
        return t * lax.rsqrt(ssq * (1.0 / HEAD_DIM) + RMS_EPS) * gain

    k = _dot(skv, wk_ref[...])
    k_ref[...] = head_norm(k, gk_ref[...]).astype(BF16)
    vt_ref[0] = _dot_nt(wvt_ref[...], skv).astype(BF16)
    q = _dot(sb, wqg_ref[:, :d])
    q_ref[...] = (head_norm(q, gq_ref[...]) * (LOG2E * HEAD_DIM ** -0.5)).astype(BF16)
    sg_ref[...] = jax.nn.sigmoid(_dot(sb, wqg_ref[:, d:])).astype(BF16)

    f = _dot_nt(wft_ref[...], skv) + bf_ref[...]
    logf = jax.nn.log_sigmoid(f)

    @pl.when(i % tiles_per_seq == 0)
    def _():
        carry_ref[...] = jnp.zeros_like(carry_ref)

    cum = _lane_cumsum(logf) + carry_ref[:, 0:1]
    carry_ref[...] = jnp.broadcast_to(cum[:, -1:], carry_ref.shape)
    fp_ref[0] = jnp.concatenate(_split_bf16(cum * LOG2E), axis=0)


def _qkv(h, gkv, gb, wk, wvt, wft, bf, gk, gq, wqg, hsum, *, tm, batch, seq):
    n, d = h.shape
    nh = wft.shape[0]
    tps = seq // tm
    tok = pl.BlockSpec((tm, d), lambda i: (i, 0))
    tok_bf = jax.ShapeDtypeStruct((n, d), BF16)
    seq_map = lambda i: (i // tps, 0, i % tps)
    consts = (gkv, gb, wk, wvt, wft, bf, gk, gq, wqg, hsum)
    return pl.pallas_call(
        functools.partial(_qkv_kernel, tiles_per_seq=tps),
        out_shape=[tok_bf, tok_bf, jax.ShapeDtypeStruct((batch, d, seq), BF16), tok_bf,
                   jax.ShapeDtypeStruct((batch, N_PIECES * nh, seq), BF16)],
        grid=(n // tm,),
        in_specs=[tok] + [_const_spec(a.shape) for a in consts],
        out_specs=[tok, tok, pl.BlockSpec((1, d, tm), seq_map), tok,
                   pl.BlockSpec((1, N_PIECES * nh, tm), seq_map)],
        scratch_shapes=[pltpu.VMEM((nh, LANES), F32)],
        compiler_params=_params(("arbitrary",)),
        name="qkv",
    )(h, *consts)


def _attention_kernel(q_ref, k_ref, vt_ref, fp_ref, sg_ref, o_ref, kaug_ref, qaug_ref, vsum_ref, *,
                      blk, nq):
    hp = pl.program_id(1)
    nh = fp_ref.shape[2] // N_PIECES
    lane = lax.broadcasted_iota(jnp.int32, (1, LANES), 1)
    own = (lane < HEAD_DIM, lane >= HEAD_DIM)
    spare = (HEAD_DIM, 0)
    q_lanes = [(lane >= s) & (lane < s + N_PIECES) for s in spare]
    k_lanes = [(lane >= s + N_PIECES) & (lane < s + 2 * N_PIECES) for s in spare]

    er = lax.broadcasted_iota(jnp.int32, (N_PIECES * nh, LANES), 0)
    ec = lax.broadcasted_iota(jnp.int32, (N_PIECES * nh, LANES), 1)
    place = jnp.zeros((N_PIECES * nh, LANES), F32)
    for h in range(2):
        for j in range(N_PIECES):
            src = er == j * nh + 2 * hp + h
            place = (place + (src & (ec == spare[h] + j)).astype(F32)
                     - (src & (ec == spare[h] + N_PIECES + j)).astype(F32))
    g = _dot(fp_ref[0], place.astype(BF16))
    for h in range(2):
        k_fill = jnp.where(k_lanes[h], g, q_lanes[h].astype(F32)).astype(BF16)
        kaug_ref[h] = jnp.where(own[h], k_ref[...], k_fill)
        q_fill = jnp.where(q_lanes[h], g, k_lanes[h].astype(F32)).astype(BF16)
        qaug_ref[h] = jnp.where(own[h], q_ref[...], q_fill)

    first_rows = lax.broadcasted_iota(jnp.int32, (LANES, 1), 0) < HEAD_DIM
    vsum_ref[0] = jnp.where(first_rows, vt_ref[0], jnp.ones_like(vt_ref[0]))
    vsum_ref[1] = jnp.where(first_rows, jnp.ones_like(vt_ref[0]), vt_ref[0])
    key = lax.broadcasted_iota(jnp.int32, (blk, blk), 0)
    qry = lax.broadcasted_iota(jnp.int32, (blk, blk), 1)
    causal = qry >= key

    def qk(i):
        lo, hi = i * blk, (i + 1) * blk
        return [_dot_nt(kaug_ref[h, 0:hi, :], qaug_ref[h, lo:hi, :]) for h in range(2)]

    scores = qk(0)
    for i in range(nq):
        lo, hi = i * blk, (i + 1) * blk
        next_scores = qk(i + 1) if i + 1 < nq else None
        probs = []
        for s in scores:
            diag = jnp.where(causal, s[lo:hi], NEG_INF)
            s = diag if i == 0 else jnp.concatenate([s[0:lo], diag], axis=0)
            probs.append(jnp.exp2(s - jnp.max(s, axis=0, keepdims=True)).astype(BF16))
        outs = [_dot(vsum_ref[h, :, 0:hi], probs[h]) for h in range(2)]
        o0 = outs[0] / outs[0][HEAD_DIM:HEAD_DIM + 1, :]
        o1 = outs[1] / outs[1][0:1, :]
        o = jnp.where(first_rows, o0, o1).T
        o_ref[lo:hi, :] = (o * sg_ref[lo:hi, :].astype(F32)).astype(BF16)
        scores = next_scores


def _attention(q, k, vt, fp, sg, *, batch, seq, blk):
    n, d = q.shape
    pair = pl.BlockSpec((seq, LANES), lambda b, hp: (b, hp))
    return pl.pallas_call(
        functools.partial(_attention_kernel, blk=blk, nq=seq // blk),
        out_shape=jax.ShapeDtypeStruct((n, d), BF16),
        grid=(batch, d // LANES),
        in_specs=[
            pair, pair,
            pl.BlockSpec((1, LANES, seq), lambda b, hp: (b, hp, 0)),
            pl.BlockSpec((1, seq, fp.shape[2]), lambda b, hp: (b, 0, 0)),
            pair,
        ],
        out_specs=pair,
        scratch_shapes=[pltpu.VMEM((2, seq, LANES), BF16)] * 2 + [pltpu.VMEM((2, LANES, seq), BF16)],
        compiler_params=_params(("parallel", "parallel")),
        name="attention",
    )(q, k, vt, fp, sg)


def _pack_bf16_pair(lo, hi):
    ulo = lax.bitcast_convert_type(lo.astype(BF16).astype(F32), jnp.uint32)
    uhi = lax.bitcast_convert_type(hi.astype(BF16).astype(F32), jnp.uint32)
    return (ulo >> 16) | uhi


def _pack_row(x):
    half = x.shape[1] // 2
    w = _pack_bf16_pair(x[:, :half], x[:, half:])
    sw = half // SUBROWS
    return [w[:, c * sw:(c + 1) * sw] for c in range(SUBROWS)]


def _unpack_row(subrows):
    lo = [lax.bitcast_convert_type(p << 16, F32) for p in subrows]
    hi = [lax.bitcast_convert_type(p & jnp.uint32(0xFFFF0000), F32) for p in subrows]
    return jnp.concatenate(lo + hi, axis=1)


def _out_router_kernel(og_ref, h_ref, wo_ref, gm_ref, wrt_ref, h2_ref, xp_ref, meta_ref, cnt_ref,
                       carry_ref):
    i = pl.program_id(0)
    ne, tm = meta_ref.shape
    h2 = h_ref[...] + _dot(og_ref[...], wo_ref[...])
    h2_ref[...] = h2
    xn = _rms(h2, gm_ref[...])
    for c, sub in enumerate(_pack_row(xn)):
        xp_ref[c] = sub
    logits = _dot_nt(wrt_ref[...], xn, precision=lax.Precision.HIGHEST)
    row = lax.broadcasted_iota(jnp.int32, (ne, tm), 0).astype(F32)
    v1 = jnp.max(logits, axis=0, keepdims=True)
    i1 = jnp.min(jnp.where(logits == v1, row, ne), axis=0, keepdims=True)
    rest = jnp.where(row == i1, -jnp.inf, logits)
    v2 = jnp.max(rest, axis=0, keepdims=True)
    i2 = jnp.min(jnp.where(rest == v2, row, ne), axis=0, keepdims=True)
    e = jnp.exp(v2 - v1)
    w1 = 1.0 / (1.0 + e)
    w2 = e / (1.0 + e)
    sel1 = row == i1
    sel2 = row == i2
    oh = (sel1 | sel2).astype(F32)

    @pl.when(i == 0)
    def _():
        carry_ref[...] = jnp.zeros_like(carry_ref)

    base = carry_ref[:, 0:1]
    rank = base + (_lane_cumsum(oh) - oh)
    r1 = jnp.sum(jnp.where(sel1, rank, 0.0), axis=0, keepdims=True)
    r2 = jnp.sum(jnp.where(sel2, rank, 0.0), axis=0, keepdims=True)
    total = base + jnp.sum(oh, axis=1, keepdims=True)
    carry_ref[...] = jnp.broadcast_to(total, carry_ref.shape)
    cnt_ref[...] = jnp.broadcast_to(total, cnt_ref.shape)
    zeros = jnp.zeros_like(w1)
    meta_ref[...] = jnp.concatenate(
        [i1.astype(F32), i2.astype(F32), w1, w2, r1, r2, zeros, zeros], axis=0)


def _out_router(og, h, wo, gm, wrt, *, tm):
    n, d = h.shape
    ne = wrt.shape[0]
    sw = d // 2 // SUBROWS
    tok = pl.BlockSpec((tm, d), lambda i: (i, 0))
    return pl.pallas_call(
        _out_router_kernel,
        out_shape=[jax.ShapeDtypeStruct((n, d), F32), jax.ShapeDtypeStruct((SUBROWS, n, sw), jnp.uint32),
                   jax.ShapeDtypeStruct((ne, n), F32), jax.ShapeDtypeStruct((ne, LANES), F32)],
        grid=(n // tm,),
        in_specs=[tok, tok, _const_spec(wo.shape), _const_spec(gm.shape), _const_spec(wrt.shape)],
        out_specs=[tok, pl.BlockSpec((SUBROWS, tm, sw), lambda i: (0, i, 0)),
                   pl.BlockSpec((ne, tm), lambda i: (0, i)),
                   pl.BlockSpec((ne, LANES), lambda i: (0, 0))],
        scratch_shapes=[pltpu.VMEM((ne, LANES), F32)],
        compiler_params=_params(("arbitrary",)),
        name="out_router",
    )(og, h, wo, gm, wrt)


def _sc_mesh():
    return plsc.VectorSubcoreMesh(core_axis_name="core", subcore_axis_name="subcore")


def _sc_scatter(x, idx, out_rows):
    rows, w = x.shape
    n_idx = idx.shape[0]

    @pl.kernel(out_type=jax.ShapeDtypeStruct((out_rows, w), x.dtype), mesh=_sc_mesh(),
               scratch_types=[])
    def scatter_rows(x_hbm, i_hbm, o_hbm):
        def body(x_vmem, i_vmem):
            for s in range(n_idx):
                pltpu.sync_copy(x_vmem, o_hbm.at[i_vmem.at[s]])

        pltpu.emit_pipeline(
            body, grid=(rows // SC_WINDOW,),
            in_specs=[pl.BlockSpec((SC_WINDOW, w), lambda i: (i, 0)),
                      pl.BlockSpec((n_idx, SC_WINDOW), lambda i: (0, i))],
            out_specs=[],
            core_axis_name=("core", "subcore"),
            dimension_semantics=(pltpu.PARALLEL,),
        )(x_hbm, i_hbm)

    return scatter_rows(x, idx)


def _sc_gather(x, idx):
    n = idx.shape[1]
    w = x.shape[1]

    @pl.kernel(out_type=jax.ShapeDtypeStruct((n, w), x.dtype), mesh=_sc_mesh(), scratch_types=[])
    def gather_rows(x_hbm, i_hbm, o_hbm):
        def body(i_vmem, o_vmem):
            pltpu.sync_copy(x_hbm.at[i_vmem.at[0]], o_vmem)

        pltpu.emit_pipeline(
            body, grid=(n // SC_WINDOW,),
            in_specs=[pl.BlockSpec((1, SC_WINDOW), lambda i: (0, i))],
            out_specs=[pl.BlockSpec((SC_WINDOW, w), lambda i: (i, 0))],
            core_axis_name=("core", "subcore"),
            dimension_semantics=(pltpu.PARALLEL,),
        )(i_hbm, o_hbm)

    return gather_rows(x, idx)


def _experts_kernel(te_ref, rows_ref, x_ref, wa_ref, wb_ref, wo_ref, o_ref, xb_ref, hm_ref, *, nf):
    del te_ref
    i = pl.program_id(0)
    f = pl.program_id(1)
    tr = xb_ref.shape[0]
    tf = wa_ref.shape[2]
    n_valid = rows_ref[i]

    @pl.when((n_valid > 0) & (f == 0))
    def _():
        x = _unpack_row([x_ref[c] for c in range(SUBROWS)])
        live = lax.broadcasted_iota(jnp.int32, (tr, 1), 0) < n_valid
        xb_ref[...] = jnp.where(live, x, 0.0).astype(BF16)

    for j in range(nf):
        @pl.when((n_valid > 0) & (f == j))
        def _(j=j):
            xb = xb_ref[...]
            a = _dot(xb, wa_ref[0])
            b = _dot(xb, wb_ref[0])
            hm_ref[:, j * tf:(j + 1) * tf] = (a * jax.nn.sigmoid(a) * b).astype(BF16)

    @pl.when((n_valid > 0) & (f == nf - 1))
    def _():
        for c, sub in enumerate(_pack_row(_dot(hm_ref[...], wo_ref[0]))):
            o_ref[c] = sub

    @pl.when((n_valid == 0) & (f == 0))
    def _():
        o_ref[...] = jnp.zeros_like(o_ref)


def _experts(tile_expert, tile_rows, xs, w_in, w_out, *, tr, tf):
    _, p, sw = xs.shape
    d = w_out.shape[2]
    de = w_out.shape[1]
    nf = de // tf
    rows = pl.BlockSpec((SUBROWS, tr, sw), lambda i, f, te, tv: (0, i, 0))
    return pl.pallas_call(
        functools.partial(_experts_kernel, nf=nf),
        out_shape=jax.ShapeDtypeStruct((SUBROWS, p, sw), jnp.uint32),
        grid_spec=pltpu.PrefetchScalarGridSpec(
            num_scalar_prefetch=2,
            grid=(p // tr, nf),
            in_specs=[
                rows,
                pl.BlockSpec((1, d, tf), lambda i, f, te, tv: (te[i], 0, f)),
                pl.BlockSpec((1, d, tf), lambda i, f, te, tv: (te[i], 0, nf + f)),
                pl.BlockSpec((1, de, d), lambda i, f, te, tv: (te[i], 0, 0),
                             pipeline_mode=pl.Buffered(1)),
            ],
            out_specs=rows,
            scratch_shapes=[pltpu.VMEM((tr, d), BF16), pltpu.VMEM((tr, de), BF16)],
        ),
        compiler_params=_params(("arbitrary", "arbitrary")),
        name="experts",
    )(tile_expert, tile_rows, xs, w_in, w_in, w_out)


def _combine_kernel(g_ref, h_ref, w_ref, o_ref):
    w = w_ref[...]
    y = [_unpack_row([g_ref[s, c] for c in range(SUBROWS)]) for s in range(TOP_K)]
    o_ref[...] = h_ref[...] + (w[:, 0:1] * y[0] + w[:, 1:2] * y[1])


def _combine(g, h, w, *, tc):
    n, d = h.shape
    sw = g.shape[3]
    tok = pl.BlockSpec((tc, d), lambda i: (i, 0))
    return pl.pallas_call(
        _combine_kernel,
        out_shape=jax.ShapeDtypeStruct((n, d), F32),
        grid=(n // tc,),
        in_specs=[pl.BlockSpec((TOP_K, SUBROWS, tc, sw), lambda i: (0, 0, i, 0)), tok,
                  pl.BlockSpec((tc, TOP_K), lambda i: (i, 0))],
        out_specs=tok,
        compiler_params=_params(("parallel",)),
        name="combine",
    )(g, h, w)


def _tiles(n, seq):
    def pick(limit, of):
        t = limit
        while of % t:
            t //= 2
        return t
    return dict(
        tm_a=pick(512, seq), tm_f=pick(512, n), tm_qkv=pick(512, seq), blk=pick(256, seq),
        tm_o=pick(512, n), tr=pick(1024, TOP_K * n), tc=pick(512, n))


def kernel(x, a_norm_g, a_w_in, a_v_norm_g, a_w_spatial, a_b_spatial, a_w_out, f_norm_g, f_w_in, f_w_out, kv_norm_g, kv_w, kv_b_f, k_norm_g, b_norm_g, b_w_in, q_norm_g, b_w_out, m_norm_g, m_w_router, m_w_in, m_w_out):
    batch, seq, d = x.shape
    n = batch * seq
    nh = d // HEAD_DIM
    ne = m_w_router.shape[-1]
    assert a_w_in.shape[0] == 1 and b_w_in.shape[0] == 1 and f_w_in.shape[0] == 1 and m_w_in.shape[0] == 1
    assert seq % GMLP_CHUNK == 0 and d % LANES == 0 and (SUBROWS * n) % SC_WINDOW == 0
    t = _tiles(n, seq)
    row = lambda g: g.reshape(1, -1)

    h = x.reshape(n, d)
    h = _mixer_a(h, row(a_norm_g[0]), a_w_in[0].astype(BF16), row(a_v_norm_g[0]), a_w_spatial[0],
                 a_b_spatial[0].T, a_w_out[0].astype(BF16), tm=t["tm_a"])
    h = _swiglu(h, row(f_norm_g[0]), f_w_in[0].astype(BF16), f_w_out[0].astype(BF16), tm=t["tm_f"])

    head = jnp.arange(d, dtype=jnp.int32) // HEAD_DIM
    hsum = (head[:, None] == head[None, :]).astype(BF16)
    q, k, vt, sg, fp = _qkv(
        h, row(kv_norm_g), row(b_norm_g[0]), kv_w[:, :d].astype(BF16),
        kv_w[:, d:2 * d].T.astype(BF16), kv_w[:, 2 * d:].T.astype(BF16), kv_b_f.reshape(nh, 1),
        row(jnp.tile(k_norm_g, nh)), row(jnp.tile(q_norm_g[0], nh)), b_w_in[0].astype(BF16), hsum,
        tm=t["tm_qkv"], batch=batch, seq=seq)
    og = _attention(q, k, vt, jnp.swapaxes(fp, 1, 2), sg, batch=batch, seq=seq, blk=t["blk"])

    h2, xp, meta, cnt = _out_router(og, h, b_w_out[0].astype(BF16), row(m_norm_g[0]),
                                    m_w_router[0].T, tm=t["tm_o"])

    tr = t["tr"]
    n_tiles = TOP_K * n // tr + ne
    p = n_tiles * tr
    counts = cnt[:, 0].astype(jnp.int32)
    tiles_per_expert = (counts + tr - 1) // tr
    tile_end = jnp.cumsum(tiles_per_expert)
    tile_start = tile_end - tiles_per_expert
    expert_ids = jnp.arange(ne, dtype=jnp.int32)
    idx = meta[0:TOP_K].astype(jnp.int32)
    start_of = jnp.sum(jnp.where(idx[:, :, None] == expert_ids, tile_start * tr, 0), axis=-1)
    dest = start_of + meta[4:4 + TOP_K].astype(jnp.int32)
    tile_ids = jnp.arange(n_tiles, dtype=jnp.int32)
    tile_expert = jnp.minimum(
        jnp.sum((tile_ids[:, None] >= tile_end[None, :]).astype(jnp.int32), axis=1), ne - 1)
    mine = tile_expert[:, None] == expert_ids[None, :]
    tile_rows = jnp.clip(
        jnp.sum(jnp.where(mine, counts - (tile_ids[:, None] - tile_start) * tr, 0), axis=1), 0, tr)
    tile_rows = jnp.where(tile_ids < tile_end[-1], tile_rows, 0).astype(jnp.int32)

    sub = (jnp.arange(SUBROWS, dtype=jnp.int32) * p)[None, :, None]
    sub_dest = sub + dest[:, None, :]
    sw = xp.shape[2]
    xs = _sc_scatter(xp.reshape(SUBROWS * n, sw), sub_dest.reshape(TOP_K, SUBROWS * n), SUBROWS * p)
    eo = _experts(tile_expert, tile_rows, xs.reshape(SUBROWS, p, sw), m_w_in[0].astype(BF16),
                  m_w_out[0].astype(BF16), tr=tr, tf=m_w_out.shape[2] // 4)
    g = _sc_gather(eo.reshape(SUBROWS * p, sw), sub_dest.reshape(1, TOP_K * SUBROWS * n))
    out = _combine(g.reshape(TOP_K, SUBROWS, n, sw), h2, meta[2:2 + TOP_K].T, tc=t["tc"])
    return out.reshape(batch, seq, d)
```

```python
import functools

import jax
import jax.numpy as jnp
from jax import lax
from jax.experimental import pallas as pl
from jax.experimental.pallas import tpu as pltpu
from jax.experimental.pallas import tpu_sc as plsc

RMS_EPS = 1e-6
NEG_INF = -1e30
LOG2E = 1.4426950408889634
N_PIECES = 3
SC_WINDOW = 128
SUBROWS = 2
EXPERT_STEPS = 2
EXPERT_SUBSLABS = 2
GMLP_CHUNK = 128
CAUSAL_CHUNK = 64
A_GROUPS = 8
HEAD_DIM = 64
LANES = 128
TOP_K = 2
VMEM_LIMIT = 56 * 1024 * 1024

BF16 = jnp.bfloat16
F32 = jnp.float32


def _dot(a, b, **kw):
    return jnp.dot(a, b, preferred_element_type=F32, **kw)


def _dot_nt(a, b, **kw):
    return lax.dot_general(a, b, (((1,), (1,)), ((), ())), preferred_element_type=F32, **kw)


def _rms(x, g):
    return x * lax.rsqrt(jnp.mean(x * x, axis=-1, keepdims=True) + RMS_EPS) * g


def _gelu(x):
    return 0.5 * x * (1.0 + lax.erf(x * (2.0 ** -0.5)))


def _const_spec(shape):
    nd = len(shape)
    return pl.BlockSpec(shape, lambda *_: (0,) * nd, pipeline_mode=pl.Buffered(1))


def _params(sem):
    return pltpu.CompilerParams(dimension_semantics=sem, vmem_limit_bytes=VMEM_LIMIT)


def _mixer_a_kernel(x_ref, g_ref, win_ref, gv_ref, ws_ref, bs_ref, wout_ref, o_ref, z_ref):
    tm = x_ref.shape[0]
    half = wout_ref.shape[0]
    gd = half // A_GROUPS
    x = x_ref[...]
    xb = _rms(x, g_ref[...]).astype(BF16)
    v = _gelu(_dot(xb, win_ref[:, half:]))
    v = (_rms(v, gv_ref[...])).astype(BF16)
    u = _gelu(_dot(xb, win_ref[:, :half]))
    row = lax.broadcasted_iota(jnp.int32, (GMLP_CHUNK, GMLP_CHUNK), 0)
    col = lax.broadcasted_iota(jnp.int32, (GMLP_CHUNK, GMLP_CHUNK), 1)
    keep = (col // CAUSAL_CHUNK) <= (row // CAUSAL_CHUNK)
    bs = bs_ref[...]
    for g in range(A_GROUPS):
        wg = jnp.where(keep, ws_ref[g], 0.0).astype(BF16)
        bg = bs[:, g:g + 1]
        for c in range(tm // GMLP_CHUNK):
            rs = slice(c * GMLP_CHUNK, (c + 1) * GMLP_CHUNK)
            cs = slice(g * gd, (g + 1) * gd)
            sv = _dot(wg, v[rs, cs]) + bg
            z_ref[rs, cs] = (u[rs, cs] * sv).astype(BF16)
    o_ref[...] = x + _dot(z_ref[...], wout_ref[...])


def _mixer_a(h, g, w_in, gv, ws, bs_t, w_out, *, tm):
    n, d = h.shape
    half = w_out.shape[0]
    return pl.pallas_call(
        _mixer_a_kernel,
        out_shape=jax.ShapeDtypeStruct((n, d), F32),
        grid=(n // tm,),
        in_specs=[
            pl.BlockSpec((tm, d), lambda i: (i, 0)),
            _const_spec(g.shape), _const_spec(w_in.shape), _const_spec(gv.shape),
            _const_spec(ws.shape), _const_spec(bs_t.shape), _const_spec(w_out.shape),
        ],
        out_specs=pl.BlockSpec((tm, d), lambda i: (i, 0)),
        scratch_shapes=[pltpu.VMEM((tm, half), BF16)],
        compiler_params=_params(("parallel",)),
        name="mixer_a",
    )(h, g, w_in, gv, ws, bs_t, w_out)


def _swiglu_kernel(x_ref, g_ref, win_ref, wout_ref, o_ref):
    f = wout_ref.shape[0]
    x = x_ref[...]
    xb = _rms(x, g_ref[...]).astype(BF16)
    a = _dot(xb, win_ref[:, :f])
    b = _dot(xb, win_ref[:, f:])
    hm = (a * jax.nn.sigmoid(a) * b).astype(BF16)
    o_ref[...] = x + _dot(hm, wout_ref[...])


def _swiglu(h, g, w_in, w_out, *, tm):
    n, d = h.shape
    return pl.pallas_call(
        _swiglu_kernel,
        out_shape=jax.ShapeDtypeStruct((n, d), F32),
        grid=(n // tm,),
        in_specs=[
            pl.BlockSpec((tm, d), lambda i: (i, 0)),
            _const_spec(g.shape), _const_spec(w_in.shape), _const_spec(w_out.shape),
        ],
        out_specs=pl.BlockSpec((tm, d), lambda i: (i, 0)),
        compiler_params=_params(("parallel",)),
        name="swiglu",
    )(h, g, w_in, w_out)


def _lane_cumsum(x):
    n = x.shape[-1]
    lane = lax.broadcasted_iota(jnp.int32, x.shape, x.ndim - 1)
    sh = 1
    while sh < n:
        x = x + jnp.where(lane >= sh, pltpu.roll(x, sh, axis=x.ndim - 1), 0.0)
        sh *= 2
    return x


def _split_bf16(x):
    pieces = []
    for _ in range(N_PIECES):
        p = x.astype(BF16)
        pieces.append(p)
        x = x - p.astype(F32)
    return pieces


def _qkv_kernel(x_ref, gkv_ref, gb_ref, wk_ref, wvt_ref, wft_ref, bf_ref, gk_ref, gq_ref, wqg_ref,
                hsum_ref, q_ref, k_ref, vt_ref, sg_ref, fp_ref, carry_ref, *, tiles_per_seq):
    d = x_ref.shape[1]
    i = pl.program_id(0)

    @pl.when(i == 0)
    def _():
        carry_ref[...] = jnp.zeros_like(carry_ref)

    x = x_ref[...]
    y = x * lax.rsqrt(jnp.mean(x * x, axis=-1, keepdims=True) + RMS_EPS)
    skv = (y * gkv_ref[...]).astype(BF16)
    sb = (y * gb_ref[...]).astype(BF16)
    hsum = hsum_ref[...]

    def head_norm(t, gain):
        ssq = _dot((t * t).astype(BF16), hsum)
        return t * lax.rsqrt(ssq * (1.0 / HEAD_DIM) + RMS_EPS) * gain

    f = _dot_nt(wft_ref[...], skv) + bf_ref[...]
    logf = jax.nn.log_sigmoid(f)
    carry = jnp.where(i % tiles_per_seq == 0, 0.0, carry_ref[:, 0:1])
    cum = _lane_cumsum(logf) + carry
    carry_ref[...] = jnp.broadcast_to(cum[:, -1:], carry_ref.shape)
    fp_ref[0] = jnp.concatenate(_split_bf16(cum * LOG2E), axis=0)

    k = _dot(skv, wk_ref[...])
    k_ref[...] = head_norm(k, gk_ref[...]).astype(BF16)
    vt_ref[0] = _dot_nt(wvt_ref[...], skv).astype(BF16)
    q = _dot(sb, wqg_ref[:, :d])
    q_ref[...] = (head_norm(q, gq_ref[...]) * (LOG2E * HEAD_DIM ** -0.5)).astype(BF16)
    sg_ref[...] = jax.nn.sigmoid(_dot(sb, wqg_ref[:, d:])).astype(BF16)


def _qkv(h, gkv, gb, wk, wvt, wft, bf, gk, gq, wqg, hsum, *, tm, batch, seq):
    n, d = h.shape
    nh = wft.shape[0]
    tps = seq // tm
    tok = pl.BlockSpec((tm, d), lambda i: (i, 0))
    tok_bf = jax.ShapeDtypeStruct((n, d), BF16)
    seq_map = lambda i: (i // tps, 0, i % tps)
    consts = (gkv, gb, wk, wvt, wft, bf, gk, gq, wqg, hsum)
    return pl.pallas_call(
        functools.partial(_qkv_kernel, tiles_per_seq=tps),
        out_shape=[tok_bf, tok_bf, jax.ShapeDtypeStruct((batch, d, seq), BF16), tok_bf,
                   jax.ShapeDtypeStruct((batch, N_PIECES * nh, seq), BF16)],
        grid=(n // tm,),
        in_specs=[tok] + [_const_spec(a.shape) for a in consts],
        out_specs=[tok, tok, pl.BlockSpec((1, d, tm), seq_map), tok,
                   pl.BlockSpec((1, N_PIECES * nh, tm), seq_map)],
        scratch_shapes=[pltpu.VMEM((nh, LANES), F32)],
        compiler_params=_params(("arbitrary",)),
        name="qkv",
    )(h, *consts)


def _attention_kernel(q_ref, k_ref, vt_ref, fp_ref, sg_ref, *rest, blk, nq, n_riders):
    rider_in, (o_ref, *rider_out) = rest[:n_riders], rest[n_riders:2 * n_riders + 1]
    kaug_ref, qaug_ref, vsum_ref = rest[2 * n_riders + 1:]
    for w_ref, c_ref in zip(rider_in, rider_out):
        c_ref[...] = w_ref[...].astype(BF16)
    hp = pl.program_id(1)
    nh = fp_ref.shape[2] // N_PIECES
    lane = lax.broadcasted_iota(jnp.int32, (1, LANES), 1)
    own = (lane < HEAD_DIM, lane >= HEAD_DIM)
    spare = (HEAD_DIM, 0)
    q_lanes = [(lane >= s) & (lane < s + N_PIECES) for s in spare]
    k_lanes = [(lane >= s + N_PIECES) & (lane < s + 2 * N_PIECES) for s in spare]

    er = lax.broadcasted_iota(jnp.int32, (N_PIECES * nh, LANES), 0)
    ec = lax.broadcasted_iota(jnp.int32, (N_PIECES * nh, LANES), 1)
    place = jnp.zeros((N_PIECES * nh, LANES), F32)
    for h in range(2):
        for j in range(N_PIECES):
            src = er == j * nh + 2 * hp + h
            place = (place + (src & (ec == spare[h] + j)).astype(F32)
                     - (src & (ec == spare[h] + N_PIECES + j)).astype(F32))
    g = _dot(fp_ref[0], place.astype(BF16))
    for h in range(2):
        k_fill = jnp.where(k_lanes[h], g, q_lanes[h].astype(F32)).astype(BF16)
        kaug_ref[h] = jnp.where(own[h], k_ref[...], k_fill)
        q_fill = jnp.where(q_lanes[h], g, k_lanes[h].astype(F32)).astype(BF16)
        qaug_ref[h] = jnp.where(own[h], q_ref[...], q_fill)

    first_rows = lax.broadcasted_iota(jnp.int32, (LANES, 1), 0) < HEAD_DIM
    vsum_ref[0] = jnp.where(first_rows, vt_ref[0], jnp.ones_like(vt_ref[0]))
    vsum_ref[1] = jnp.where(first_rows, jnp.ones_like(vt_ref[0]), vt_ref[0])
    key = lax.broadcasted_iota(jnp.int32, (blk, blk), 0)
    qry = lax.broadcasted_iota(jnp.int32, (blk, blk), 1)
    causal = qry >= key

    def qk(i):
        lo, hi = i * blk, (i + 1) * blk
        return [_dot_nt(kaug_ref[h, 0:hi, :], qaug_ref[h, lo:hi, :]) for h in range(2)]

    scores = qk(0)
    for i in range(nq):
        lo, hi = i * blk, (i + 1) * blk
        next_scores = qk(i + 1) if i + 1 < nq else None
        probs = []
        for s in scores:
            diag = jnp.where(causal, s[lo:hi], NEG_INF)
            s = diag if i == 0 else jnp.concatenate([s[0:lo], diag], axis=0)
            probs.append(jnp.exp2(s - jnp.max(s, axis=0, keepdims=True)).astype(BF16))
        outs = [_dot(vsum_ref[h, :, 0:hi], probs[h]) for h in range(2)]
        o0 = outs[0] / outs[0][HEAD_DIM:HEAD_DIM + 1, :]
        o1 = outs[1] / outs[1][0:1, :]
        o = jnp.where(first_rows, o0, o1).T
        o_ref[lo:hi, :] = (o * sg_ref[lo:hi, :].astype(F32)).astype(BF16)
        scores = next_scores


def _attention(q, k, vt, fp, sg, riders, *, batch, seq, blk):
    n, d = q.shape
    n_pairs = d // LANES
    steps = batch * n_pairs
    pair = pl.BlockSpec((seq, LANES), lambda b, hp: (b, hp))
    slabs = [w.reshape(steps, -1, w.shape[-1]) for w in riders]
    slab_specs = [pl.BlockSpec((1,) + s.shape[1:], lambda b, hp: (b * n_pairs + hp, 0, 0))
                  for s in slabs]
    out = pl.pallas_call(
        functools.partial(_attention_kernel, blk=blk, nq=seq // blk, n_riders=len(riders)),
        out_shape=[jax.ShapeDtypeStruct((n, d), BF16)]
        + [jax.ShapeDtypeStruct(s.shape, BF16) for s in slabs],
        grid=(batch, n_pairs),
        in_specs=[
            pair, pair,
            pl.BlockSpec((1, LANES, seq), lambda b, hp: (b, hp, 0)),
            pl.BlockSpec((1, seq, fp.shape[2]), lambda b, hp: (b, 0, 0)),
            pair,
        ] + slab_specs,
        out_specs=[pair] + slab_specs,
        scratch_shapes=[pltpu.VMEM((2, seq, LANES), BF16)] * 2 + [pltpu.VMEM((2, LANES, seq), BF16)],
        compiler_params=_params(("parallel", "parallel")),
        name="attention",
    )(q, k, vt, fp, sg, *slabs)
    return out[0], [c.reshape(w.shape) for c, w in zip(out[1:], riders)]


def _pack_bf16_pair(lo, hi):
    ulo = lax.bitcast_convert_type(lo.astype(BF16).astype(F32), jnp.uint32)
    uhi = lax.bitcast_convert_type(hi.astype(BF16).astype(F32), jnp.uint32)
    return (ulo >> 16) | uhi


def _pack_row(x):
    half = x.shape[1] // 2
    w = _pack_bf16_pair(x[:, :half], x[:, half:])
    sw = half // SUBROWS
    return [w[:, c * sw:(c + 1) * sw] for c in range(SUBROWS)]


def _unpack_row(subrows):
    lo = [lax.bitcast_convert_type(p << 16, F32) for p in subrows]
    hi = [lax.bitcast_convert_type(p & jnp.uint32(0xFFFF0000), F32) for p in subrows]
    return jnp.concatenate(lo + hi, axis=1)


def _out_router_kernel(og_ref, h_ref, wo_ref, gm_ref, wrt_ref, h2_ref, xp_ref, meta_ref, cnt_ref,
                       carry_ref):
    i = pl.program_id(0)
    ne, tm = meta_ref.shape
    h2 = h_ref[...] + _dot(og_ref[...], wo_ref[...])
    h2_ref[...] = h2
    xn = _rms(h2, gm_ref[...])
    for c, sub in enumerate(_pack_row(xn)):
        xp_ref[c] = sub
    logits = _dot_nt(wrt_ref[...], xn, precision=lax.Precision.HIGHEST)
    row = lax.broadcasted_iota(jnp.int32, (ne, tm), 0).astype(F32)
    v1 = jnp.max(logits, axis=0, keepdims=True)
    i1 = jnp.min(jnp.where(logits == v1, row, ne), axis=0, keepdims=True)
    rest = jnp.where(row == i1, -jnp.inf, logits)
    v2 = jnp.max(rest, axis=0, keepdims=True)
    i2 = jnp.min(jnp.where(rest == v2, row, ne), axis=0, keepdims=True)
    e = jnp.exp(v2 - v1)
    w1 = 1.0 / (1.0 + e)
    w2 = e / (1.0 + e)
    sel1 = row == i1
    sel2 = row == i2
    oh = (sel1 | sel2).astype(F32)

    @pl.when(i == 0)
    def _():
        carry_ref[...] = jnp.zeros_like(carry_ref)

    base = carry_ref[:, 0:1]
    rank = base + (_lane_cumsum(oh) - oh)
    r1 = jnp.sum(jnp.where(sel1, rank, 0.0), axis=0, keepdims=True)
    r2 = jnp.sum(jnp.where(sel2, rank, 0.0), axis=0, keepdims=True)
    total = base + jnp.sum(oh, axis=1, keepdims=True)
    carry_ref[...] = jnp.broadcast_to(total, carry_ref.shape)
    cnt_ref[...] = jnp.broadcast_to(total, cnt_ref.shape)
    zeros = jnp.zeros_like(w1)
    meta_ref[...] = jnp.concatenate(
        [i1.astype(F32), i2.astype(F32), w1, w2, r1, r2, zeros, zeros], axis=0)


def _out_router(og, h, wo, gm, wrt, *, tm):
    n, d = h.shape
    ne = wrt.shape[0]
    sw = d // 2 // SUBROWS
    tok = pl.BlockSpec((tm, d), lambda i: (i, 0))
    return pl.pallas_call(
        _out_router_kernel,
        out_shape=[jax.ShapeDtypeStruct((n, d), F32), jax.ShapeDtypeStruct((SUBROWS, n, sw), jnp.uint32),
                   jax.ShapeDtypeStruct((ne, n), F32), jax.ShapeDtypeStruct((ne, LANES), F32)],
        grid=(n // tm,),
        in_specs=[tok, tok, _const_spec(wo.shape), _const_spec(gm.shape), _const_spec(wrt.shape)],
        out_specs=[tok, pl.BlockSpec((SUBROWS, tm, sw), lambda i: (0, i, 0)),
                   pl.BlockSpec((ne, tm), lambda i: (0, i)),
                   pl.BlockSpec((ne, LANES), lambda i: (0, 0))],
        scratch_shapes=[pltpu.VMEM((ne, LANES), F32)],
        compiler_params=_params(("arbitrary",)),
        name="out_router",
    )(og, h, wo, gm, wrt)


def _sc_mesh():
    return plsc.VectorSubcoreMesh(core_axis_name="core", subcore_axis_name="subcore")


def _sc_scatter(x, idx, out_rows):
    rows, w = x.shape
    n_idx = idx.shape[0]

    @pl.kernel(out_type=jax.ShapeDtypeStruct((out_rows, w), x.dtype), mesh=_sc_mesh(),
               scratch_types=[])
    def scatter_rows(x_hbm, i_hbm, o_hbm):
        def body(x_vmem, i_vmem):
            for s in range(n_idx):
                pltpu.sync_copy(x_vmem, o_hbm.at[i_vmem.at[s]])

        pltpu.emit_pipeline(
            body, grid=(rows // SC_WINDOW,),
            in_specs=[pl.BlockSpec((SC_WINDOW, w), lambda i: (i, 0)),
                      pl.BlockSpec((n_idx, SC_WINDOW), lambda i: (0, i))],
            out_specs=[],
            core_axis_name=("core", "subcore"),
            dimension_semantics=(pltpu.PARALLEL,),
        )(x_hbm, i_hbm)

    return scatter_rows(x, idx)


def _sc_gather(x, idx):
    n = idx.shape[1]
    w = x.shape[1]

    @pl.kernel(out_type=jax.ShapeDtypeStruct((n, w), x.dtype), mesh=_sc_mesh(), scratch_types=[])
    def gather_rows(x_hbm, i_hbm, o_hbm):
        def body(i_vmem, o_vmem):
            pltpu.sync_copy(x_hbm.at[i_vmem.at[0]], o_vmem)

        pltpu.emit_pipeline(
            body, grid=(n // SC_WINDOW,),
            in_specs=[pl.BlockSpec((1, SC_WINDOW), lambda i: (0, i))],
            out_specs=[pl.BlockSpec((SC_WINDOW, w), lambda i: (i, 0))],
            core_axis_name=("core", "subcore"),
            dimension_semantics=(pltpu.PARALLEL,),
        )(i_hbm, o_hbm)

    return gather_rows(x, idx)


def _experts_kernel(te_ref, rows_ref, x_ref, wa_ref, wb_ref, wo_ref, o_ref, xb_ref, hm_ref, *, nf):
    del te_ref
    i = pl.program_id(0)
    f = pl.program_id(1)
    tr = xb_ref.shape[0]
    tf = wa_ref.shape[2]
    n_valid = rows_ref[i]

    @pl.when((n_valid > 0) & (f == 0))
    def _():
        x = _unpack_row([x_ref[c] for c in range(SUBROWS)])
        live = lax.broadcasted_iota(jnp.int32, (tr, 1), 0) < n_valid
        xb_ref[...] = jnp.where(live, x, 0.0).astype(BF16)

    ts = tf // EXPERT_SUBSLABS
    for j in range(nf):
        @pl.when((n_valid > 0) & (f == j))
        def _(j=j):
            xb = xb_ref[...]
            for c in range(EXPERT_SUBSLABS):
                a = _dot(xb, wa_ref[0, :, c * ts:(c + 1) * ts])
                b = _dot(xb, wb_ref[0, :, c * ts:(c + 1) * ts])
                col = j * tf + c * ts
                hm_ref[:, col:col + ts] = (a * jax.nn.sigmoid(a) * b).astype(BF16)

    @pl.when((n_valid > 0) & (f == nf - 1))
    def _():
        for c, sub in enumerate(_pack_row(_dot(hm_ref[...], wo_ref[0]))):
            o_ref[c] = sub

    @pl.when((n_valid == 0) & (f == 0))
    def _():
        o_ref[...] = jnp.zeros_like(o_ref)


def _experts(tile_expert, tile_rows, xs, w_in, w_out, *, tr, tf):
    _, p, sw = xs.shape
    d = w_out.shape[2]
    de = w_out.shape[1]
    nf = de // tf
    rows = pl.BlockSpec((SUBROWS, tr, sw), lambda i, f, te, tv: (0, i, 0))
    return pl.pallas_call(
        functools.partial(_experts_kernel, nf=nf),
        out_shape=jax.ShapeDtypeStruct((SUBROWS, p, sw), jnp.uint32),
        grid_spec=pltpu.PrefetchScalarGridSpec(
            num_scalar_prefetch=2,
            grid=(p // tr, nf),
            in_specs=[
                rows,
                pl.BlockSpec((1, d, tf), lambda i, f, te, tv: (te[i], 0, f)),
                pl.BlockSpec((1, d, tf), lambda i, f, te, tv: (te[i], 0, nf + f)),
                pl.BlockSpec((1, de, d), lambda i, f, te, tv: (te[i], 0, 0),
                             pipeline_mode=pl.Buffered(1)),
            ],
            out_specs=rows,
            scratch_shapes=[pltpu.VMEM((tr, d), BF16), pltpu.VMEM((tr, de), BF16)],
        ),
        compiler_params=_params(("arbitrary", "arbitrary")),
        name="experts",
    )(tile_expert, tile_rows, xs, w_in, w_in, w_out)


def _combine_kernel(g_ref, h_ref, w_ref, o_ref):
    w = w_ref[...]
    y = [_unpack_row([g_ref[s, c] for c in range(SUBROWS)]) for s in range(TOP_K)]
    o_ref[...] = h_ref[...] + (w[:, 0:1] * y[0] + w[:, 1:2] * y[1])


def _combine(g, h, w, *, tc):
    n, d = h.shape
    sw = g.shape[3]
    tok = pl.BlockSpec((tc, d), lambda i: (i, 0))
    return pl.pallas_call(
        _combine_kernel,
        out_shape=jax.ShapeDtypeStruct((n, d), F32),
        grid=(n // tc,),
        in_specs=[pl.BlockSpec((TOP_K, SUBROWS, tc, sw), lambda i: (0, 0, i, 0)), tok,
                  pl.BlockSpec((tc, TOP_K), lambda i: (i, 0))],
        out_specs=tok,
        compiler_params=_params(("parallel",)),
        name="combine",
    )(g, h, w)


def _tiles(n, seq):
    def pick(limit, of):
        t = limit
        while of % t:
            t //= 2
        return t
    return dict(
        tm_a=pick(512, seq), tm_f=pick(512, n), tm_qkv=pick(512, seq), blk=pick(256, seq),
        tm_o=pick(512, n), tr=pick(1024, TOP_K * n), tc=pick(512, n))


def kernel(x, a_norm_g, a_w_in, a_v_norm_g, a_w_spatial, a_b_spatial, a_w_out, f_norm_g, f_w_in, f_w_out, kv_norm_g, kv_w, kv_b_f, k_norm_g, b_norm_g, b_w_in, q_norm_g, b_w_out, m_norm_g, m_w_router, m_w_in, m_w_out):
    batch, seq, d = x.shape
    n = batch * seq
    nh = d // HEAD_DIM
    ne = m_w_router.shape[-1]
    assert a_w_in.shape[0] == 1 and b_w_in.shape[0] == 1 and f_w_in.shape[0] == 1 and m_w_in.shape[0] == 1
    assert seq % GMLP_CHUNK == 0 and d % LANES == 0 and (SUBROWS * n) % SC_WINDOW == 0
    t = _tiles(n, seq)
    row = lambda g: g.reshape(1, -1)

    h = x.reshape(n, d)
    h = _mixer_a(h, row(a_norm_g[0]), a_w_in[0].astype(BF16), row(a_v_norm_g[0]), a_w_spatial[0],
                 a_b_spatial[0].T, a_w_out[0].astype(BF16), tm=t["tm_a"])
    h = _swiglu(h, row(f_norm_g[0]), f_w_in[0].astype(BF16), f_w_out[0].astype(BF16), tm=t["tm_f"])

    head = jnp.arange(d, dtype=jnp.int32) // HEAD_DIM
    hsum = (head[:, None] == head[None, :]).astype(BF16)
    q, k, vt, sg, fp = _qkv(
        h, row(kv_norm_g), row(b_norm_g[0]), kv_w[:, :d].astype(BF16),
        kv_w[:, d:2 * d].T.astype(BF16), kv_w[:, 2 * d:].T.astype(BF16), kv_b_f.reshape(nh, 1),
        row(jnp.tile(k_norm_g, nh)), row(jnp.tile(q_norm_g[0], nh)), b_w_in[0].astype(BF16), hsum,
        tm=t["tm_qkv"], batch=batch, seq=seq)
    og, (moe_w_in, moe_w_out) = _attention(q, k, vt, jnp.swapaxes(fp, 1, 2), sg,
                                           (m_w_in[0], m_w_out[0]), batch=batch, seq=seq, blk=t["blk"])

    h2, xp, meta, cnt = _out_router(og, h, b_w_out[0].astype(BF16), row(m_norm_g[0]),
                                    m_w_router[0].T, tm=t["tm_o"])

    tr = t["tr"]
    n_tiles = TOP_K * n // tr + ne
    p = n_tiles * tr
    counts = cnt[:, 0].astype(jnp.int32)
    tiles_per_expert = (counts + tr - 1) // tr
    tile_end = jnp.cumsum(tiles_per_expert)
    tile_start = tile_end - tiles_per_expert
    expert_ids = jnp.arange(ne, dtype=jnp.int32)
    idx = meta[0:TOP_K].astype(jnp.int32)
    start_of = jnp.sum(jnp.where(idx[:, :, None] == expert_ids, tile_start * tr, 0), axis=-1)
    dest = start_of + meta[4:4 + TOP_K].astype(jnp.int32)
    tile_ids = jnp.arange(n_tiles, dtype=jnp.int32)
    tile_expert = jnp.minimum(
        jnp.sum((tile_ids[:, None] >= tile_end[None, :]).astype(jnp.int32), axis=1), ne - 1)
    mine = tile_expert[:, None] == expert_ids[None, :]
    tile_rows = jnp.clip(
        jnp.sum(jnp.where(mine, counts - (tile_ids[:, None] - tile_start) * tr, 0), axis=1), 0, tr)
    tile_rows = jnp.where(tile_ids < tile_end[-1], tile_rows, 0).astype(jnp.int32)

    sub = (jnp.arange(SUBROWS, dtype=jnp.int32) * p)[None, :, None]
    sub_dest = sub + dest[:, None, :]
    sw = xp.shape[2]
    xs = _sc_scatter(xp.reshape(SUBROWS * n, sw), sub_dest.reshape(TOP_K, SUBROWS * n), SUBROWS * p)
    eo = _experts(tile_expert, tile_rows, xs.reshape(SUBROWS, p, sw), moe_w_in, moe_w_out,
                  tr=tr, tf=m_w_out.shape[2] // EXPERT_STEPS)
    g = _sc_gather(eo.reshape(SUBROWS * p, sw), sub_dest.reshape(1, TOP_K * SUBROWS * n))
    out = _combine(g.reshape(TOP_K, SUBROWS, n, sw), h2, meta[2:2 + TOP_K].T, tc=t["tc"])
    return out.reshape(batch, seq, d)
```

```python
import functools

import jax
import jax.numpy as jnp
from jax import lax
from jax.experimental import pallas as pl
from jax.experimental.pallas import tpu as pltpu
from jax.experimental.pallas import tpu_sc as plsc

RMS_EPS = 1e-6
NEG_INF = -1e30
LOG2E = 1.4426950408889634
N_PIECES = 3
SC_WINDOW = 128
SUBROWS = 2
EXPERT_STEPS = 2
EXPERT_SUBSLABS = 2
GMLP_CHUNK = 128
CAUSAL_CHUNK = 64
A_GROUPS = 8
HEAD_DIM = 64
LANES = 128
TOP_K = 2
VMEM_LIMIT = 56 * 1024 * 1024

BF16 = jnp.bfloat16
F32 = jnp.float32


def _dot(a, b, **kw):
    return jnp.dot(a, b, preferred_element_type=F32, **kw)


def _dot_nt(a, b, **kw):
    return lax.dot_general(a, b, (((1,), (1,)), ((), ())), preferred_element_type=F32, **kw)


def _rms(x, g):
    return x * lax.rsqrt(jnp.mean(x * x, axis=-1, keepdims=True) + RMS_EPS) * g


def _gelu(x):
    return 0.5 * x * (1.0 + lax.erf(x * (2.0 ** -0.5)))


def _const_spec(shape):
    nd = len(shape)
    return pl.BlockSpec(shape, lambda *_: (0,) * nd, pipeline_mode=pl.Buffered(1))


def _params(sem):
    return pltpu.CompilerParams(dimension_semantics=sem, vmem_limit_bytes=VMEM_LIMIT)


def _mixer_a_kernel(x_ref, g_ref, win_ref, gv_ref, ws_ref, bs_ref, wout_ref, o_ref, z_ref):
    tm = x_ref.shape[0]
    half = wout_ref.shape[0]
    gd = half // A_GROUPS
    x = x_ref[...]
    xb = _rms(x, g_ref[...]).astype(BF16)
    v = _gelu(_dot(xb, win_ref[:, half:]))
    v = (_rms(v, gv_ref[...])).astype(BF16)
    u = _gelu(_dot(xb, win_ref[:, :half]))
    row = lax.broadcasted_iota(jnp.int32, (GMLP_CHUNK, GMLP_CHUNK), 0)
    col = lax.broadcasted_iota(jnp.int32, (GMLP_CHUNK, GMLP_CHUNK), 1)
    keep = (col // CAUSAL_CHUNK) <= (row // CAUSAL_CHUNK)
    bs = bs_ref[...]
    for g in range(A_GROUPS):
        wg = jnp.where(keep, ws_ref[g], 0.0).astype(BF16)
        bg = bs[:, g:g + 1]
        for c in range(tm // GMLP_CHUNK):
            rs = slice(c * GMLP_CHUNK, (c + 1) * GMLP_CHUNK)
            cs = slice(g * gd, (g + 1) * gd)
            sv = _dot(wg, v[rs, cs]) + bg
            z_ref[rs, cs] = (u[rs, cs] * sv).astype(BF16)
    o_ref[...] = x + _dot(z_ref[...], wout_ref[...])


def _mixer_a(h, g, w_in, gv, ws, bs_t, w_out, *, tm):
    n, d = h.shape
    half = w_out.shape[0]
    return pl.pallas_call(
        _mixer_a_kernel,
        out_shape=jax.ShapeDtypeStruct((n, d), F32),
        grid=(n // tm,),
        in_specs=[
            pl.BlockSpec((tm, d), lambda i: (i, 0)),
            _const_spec(g.shape), _const_spec(w_in.shape), _const_spec(gv.shape),
            _const_spec(ws.shape), _const_spec(bs_t.shape), _const_spec(w_out.shape),
        ],
        out_specs=pl.BlockSpec((tm, d), lambda i: (i, 0)),
        scratch_shapes=[pltpu.VMEM((tm, half), BF16)],
        compiler_params=_params(("parallel",)),
        name="mixer_a",
    )(h, g, w_in, gv, ws, bs_t, w_out)


def _swiglu_kernel(x_ref, g_ref, win_ref, wout_ref, o_ref):
    f = wout_ref.shape[0]
    x = x_ref[...]
    xb = _rms(x, g_ref[...]).astype(BF16)
    a = _dot(xb, win_ref[:, :f])
    b = _dot(xb, win_ref[:, f:])
    hm = (a * jax.nn.sigmoid(a) * b).astype(BF16)
    o_ref[...] = x + _dot(hm, wout_ref[...])


def _swiglu(h, g, w_in, w_out, *, tm):
    n, d = h.shape
    return pl.pallas_call(
        _swiglu_kernel,
        out_shape=jax.ShapeDtypeStruct((n, d), F32),
        grid=(n // tm,),
        in_specs=[
            pl.BlockSpec((tm, d), lambda i: (i, 0)),
            _const_spec(g.shape), _const_spec(w_in.shape), _const_spec(w_out.shape),
        ],
        out_specs=pl.BlockSpec((tm, d), lambda i: (i, 0)),
        compiler_params=_params(("parallel",)),
        name="swiglu",
    )(h, g, w_in, w_out)


def _lane_cumsum(x):
    n = x.shape[-1]
    lane = lax.broadcasted_iota(jnp.int32, x.shape, x.ndim - 1)
    sh = 1
    while sh < n:
        x = x + jnp.where(lane >= sh, pltpu.roll(x, sh, axis=x.ndim - 1), 0.0)
        sh *= 2
    return x


def _split_bf16(x):
    pieces = []
    for _ in range(N_PIECES):
        p = x.astype(BF16)
        pieces.append(p)
        x = x - p.astype(F32)
    return pieces


def _qkv_kernel(x_ref, gkv_ref, gb_ref, wk_ref, wvt_ref, wft_ref, bf_ref, gk_ref, gq_ref, wqg_ref,
                hsum_ref, q_ref, k_ref, vt_ref, sg_ref, fp_ref, carry_ref, *, tiles_per_seq):
    d = x_ref.shape[1]
    i = pl.program_id(0)

    @pl.when(i == 0)
    def _():
        carry_ref[...] = jnp.zeros_like(carry_ref)

    x = x_ref[...]
    y = x * lax.rsqrt(jnp.mean(x * x, axis=-1, keepdims=True) + RMS_EPS)
    skv = (y * gkv_ref[...]).astype(BF16)
    sb = (y * gb_ref[...]).astype(BF16)
    hsum = hsum_ref[...]

    def head_norm(t, gain):
        ssq = _dot((t * t).astype(BF16), hsum)
        return t * lax.rsqrt(ssq * (1.0 / HEAD_DIM) + RMS_EPS) * gain

    f = _dot_nt(wft_ref[...], skv) + bf_ref[...]
    logf = jax.nn.log_sigmoid(f)
    carry = jnp.where(i % tiles_per_seq == 0, 0.0, carry_ref[:, 0:1])
    cum = _lane_cumsum(logf) + carry
    carry_ref[...] = jnp.broadcast_to(cum[:, -1:], carry_ref.shape)
    fp_ref[0] = jnp.concatenate(_split_bf16(cum * LOG2E), axis=0)

    k = _dot(skv, wk_ref[...])
    k_ref[...] = head_norm(k, gk_ref[...]).astype(BF16)
    vt_ref[0] = _dot_nt(wvt_ref[...], skv).astype(BF16)
    q = _dot(sb, wqg_ref[:, :d])
    q_ref[...] = (head_norm(q, gq_ref[...]) * (LOG2E * HEAD_DIM ** -0.5)).astype(BF16)
    sg_ref[...] = jax.nn.sigmoid(_dot(sb, wqg_ref[:, d:])).astype(BF16)


def _qkv(h, gkv, gb, wk, wvt, wft, bf, gk, gq, wqg, hsum, *, tm, batch, seq):
    n, d = h.shape
    nh = wft.shape[0]
    tps = seq // tm
    tok = pl.BlockSpec((tm, d), lambda i: (i, 0))
    tok_bf = jax.ShapeDtypeStruct((n, d), BF16)
    seq_map = lambda i: (i // tps, 0, i % tps)
    consts = (gkv, gb, wk, wvt, wft, bf, gk, gq, wqg, hsum)
    return pl.pallas_call(
        functools.partial(_qkv_kernel, tiles_per_seq=tps),
        out_shape=[tok_bf, tok_bf, jax.ShapeDtypeStruct((batch, d, seq), BF16), tok_bf,
                   jax.ShapeDtypeStruct((batch, N_PIECES * nh, seq), BF16)],
        grid=(n // tm,),
        in_specs=[tok] + [_const_spec(a.shape) for a in consts],
        out_specs=[tok, tok, pl.BlockSpec((1, d, tm), seq_map), tok,
                   pl.BlockSpec((1, N_PIECES * nh, tm), seq_map)],
        scratch_shapes=[pltpu.VMEM((nh, LANES), F32)],
        compiler_params=_params(("arbitrary",)),
        name="qkv",
    )(h, *consts)


def _attention_kernel(q_ref, k_ref, vt_ref, fp_ref, sg_ref, *rest, blk, nq, n_riders):
    rider_in, (o_ref, *rider_out) = rest[:n_riders], rest[n_riders:2 * n_riders + 1]
    kaug_ref, qaug_ref, vsum_ref = rest[2 * n_riders + 1:]
    for w_ref, c_ref in zip(rider_in, rider_out):
        c_ref[...] = w_ref[...].astype(BF16)
    hp = pl.program_id(1)
    nh = fp_ref.shape[2] // N_PIECES
    lane = lax.broadcasted_iota(jnp.int32, (1, LANES), 1)
    own = (lane < HEAD_DIM, lane >= HEAD_DIM)
    spare = (HEAD_DIM, 0)
    q_lanes = [(lane >= s) & (lane < s + N_PIECES) for s in spare]
    k_lanes = [(lane >= s + N_PIECES) & (lane < s + 2 * N_PIECES) for s in spare]

    er = lax.broadcasted_iota(jnp.int32, (N_PIECES * nh, LANES), 0)
    ec = lax.broadcasted_iota(jnp.int32, (N_PIECES * nh, LANES), 1)
    place = jnp.zeros((N_PIECES * nh, LANES), F32)
    for h in range(2):
        for j in range(N_PIECES):
            src = er == j * nh + 2 * hp + h
            place = (place + (src & (ec == spare[h] + j)).astype(F32)
                     - (src & (ec == spare[h] + N_PIECES + j)).astype(F32))
    g = _dot(fp_ref[0], place.astype(BF16))
    for h in range(2):
        k_fill = jnp.where(k_lanes[h], g, q_lanes[h].astype(F32)).astype(BF16)
        kaug_ref[h] = jnp.where(own[h], k_ref[...], k_fill)
        q_fill = jnp.where(q_lanes[h], g, k_lanes[h].astype(F32)).astype(BF16)
        qaug_ref[h] = jnp.where(own[h], q_ref[...], q_fill)

    first_rows = lax.broadcasted_iota(jnp.int32, (LANES, 1), 0) < HEAD_DIM
    vsum_ref[0] = jnp.where(first_rows, vt_ref[0], jnp.ones_like(vt_ref[0]))
    vsum_ref[1] = jnp.where(first_rows, jnp.ones_like(vt_ref[0]), vt_ref[0])
    key = lax.broadcasted_iota(jnp.int32, (blk, blk), 0)
    qry = lax.broadcasted_iota(jnp.int32, (blk, blk), 1)
    causal = qry >= key

    def qk(i):
        lo, hi = i * blk, (i + 1) * blk
        return [_dot_nt(kaug_ref[h, 0:hi, :], qaug_ref[h, lo:hi, :]) for h in range(2)]

    scores = qk(0)
    for i in range(nq):
        lo, hi = i * blk, (i + 1) * blk
        next_scores = qk(i + 1) if i + 1 < nq else None
        probs = []
        for s in scores:
            diag = jnp.where(causal, s[lo:hi], NEG_INF)
            s = diag if i == 0 else jnp.concatenate([s[0:lo], diag], axis=0)
            probs.append(jnp.exp2(s - jnp.max(s, axis=0, keepdims=True)).astype(BF16))
        outs = [_dot(vsum_ref[h, :, 0:hi], probs[h]) for h in range(2)]
        o0 = outs[0] / outs[0][HEAD_DIM:HEAD_DIM + 1, :]
        o1 = outs[1] / outs[1][0:1, :]
        o = jnp.where(first_rows, o0, o1).T
        o_ref[lo:hi, :] = (o * sg_ref[lo:hi, :].astype(F32)).astype(BF16)
        scores = next_scores


def _attention(q, k, vt, fp, sg, riders, *, batch, seq, blk):
    n, d = q.shape
    n_pairs = d // LANES
    steps = batch * n_pairs
    pair = pl.BlockSpec((seq, LANES), lambda b, hp: (b, hp))
    slabs = [w.reshape(steps, -1, w.shape[-1]) for w in riders]
    slab_specs = [pl.BlockSpec((1,) + s.shape[1:], lambda b, hp: (b * n_pairs + hp, 0, 0))
                  for s in slabs]
    out = pl.pallas_call(
        functools.partial(_attention_kernel, blk=blk, nq=seq // blk, n_riders=len(riders)),
        out_shape=[jax.ShapeDtypeStruct((n, d), BF16)]
        + [jax.ShapeDtypeStruct(s.shape, BF16) for s in slabs],
        grid=(batch, n_pairs),
        in_specs=[
            pair, pair,
            pl.BlockSpec((1, LANES, seq), lambda b, hp: (b, hp, 0)),
            pl.BlockSpec((1, seq, fp.shape[2]), lambda b, hp: (b, 0, 0)),
            pair,
        ] + slab_specs,
        out_specs=[pair] + slab_specs,
        scratch_shapes=[pltpu.VMEM((2, seq, LANES), BF16)] * 2 + [pltpu.VMEM((2, LANES, seq), BF16)],
        compiler_params=_params(("parallel", "parallel")),
        name="attention",
    )(q, k, vt, fp, sg, *slabs)
    return out[0], [c.reshape(w.shape) for c, w in zip(out[1:], riders)]


def _pack_bf16_pair(lo, hi):
    ulo = lax.bitcast_convert_type(lo.astype(BF16).astype(F32), jnp.uint32)
    uhi = lax.bitcast_convert_type(hi.astype(BF16).astype(F32), jnp.uint32)
    return (ulo >> 16) | uhi


def _pack_row(x):
    half = x.shape[1] // 2
    w = _pack_bf16_pair(x[:, :half], x[:, half:])
    sw = half // SUBROWS
    return [w[:, c * sw:(c + 1) * sw] for c in range(SUBROWS)]


def _unpack_row(subrows):
    lo = [lax.bitcast_convert_type(p << 16, F32) for p in subrows]
    hi = [lax.bitcast_convert_type(p & jnp.uint32(0xFFFF0000), F32) for p in subrows]
    return jnp.concatenate(lo + hi, axis=1)


def _out_router_kernel(og_ref, h_ref, wo_ref, gm_ref, wrt_ref, h2_ref, xp_ref, meta_ref, cnt_ref,
                       carry_ref):
    i = pl.program_id(0)
    ne, tm = meta_ref.shape
    h2 = h_ref[...] + _dot(og_ref[...], wo_ref[...])
    h2_ref[...] = h2
    xn = _rms(h2, gm_ref[...])
    for c, sub in enumerate(_pack_row(xn)):
        xp_ref[c] = sub
    xh = xn.astype(BF16)
    xl = (xn - xh.astype(F32)).astype(BF16)
    wr = wrt_ref[...]
    wh = wr.astype(BF16)
    wl = (wr - wh.astype(F32)).astype(BF16)
    logits = _dot_nt(wh, xh) + (_dot_nt(wh, xl) + _dot_nt(wl, xh))
    row = lax.broadcasted_iota(jnp.int32, (ne, tm), 0).astype(F32)
    v1 = jnp.max(logits, axis=0, keepdims=True)
    i1 = jnp.min(jnp.where(logits == v1, row, ne), axis=0, keepdims=True)
    rest = jnp.where(row == i1, -jnp.inf, logits)
    v2 = jnp.max(rest, axis=0, keepdims=True)
    i2 = jnp.min(jnp.where(rest == v2, row, ne), axis=0, keepdims=True)
    e = jnp.exp(v2 - v1)
    w1 = 1.0 / (1.0 + e)
    w2 = e / (1.0 + e)
    sel1 = row == i1
    sel2 = row == i2
    oh = (sel1 | sel2).astype(F32)

    @pl.when(i == 0)
    def _():
        carry_ref[...] = jnp.zeros_like(carry_ref)

    base = carry_ref[:, 0:1]
    rank = base + (_lane_cumsum(oh) - oh)
    r1 = jnp.sum(jnp.where(sel1, rank, 0.0), axis=0, keepdims=True)
    r2 = jnp.sum(jnp.where(sel2, rank, 0.0), axis=0, keepdims=True)
    total = base + jnp.sum(oh, axis=1, keepdims=True)
    carry_ref[...] = jnp.broadcast_to(total, carry_ref.shape)
    cnt_ref[...] = jnp.broadcast_to(total, cnt_ref.shape)
    zeros = jnp.zeros_like(w1)
    meta_ref[...] = jnp.concatenate(
        [i1.astype(F32), i2.astype(F32), w1, w2, r1, r2, zeros, zeros], axis=0)


def _out_router(og, h, wo, gm, wrt, *, tm):
    n, d = h.shape
    ne = wrt.shape[0]
    sw = d // 2 // SUBROWS
    tok = pl.BlockSpec((tm, d), lambda i: (i, 0))
    return pl.pallas_call(
        _out_router_kernel,
        out_shape=[jax.ShapeDtypeStruct((n, d), F32), jax.ShapeDtypeStruct((SUBROWS, n, sw), jnp.uint32),
                   jax.ShapeDtypeStruct((ne, n), F32), jax.ShapeDtypeStruct((ne, LANES), F32)],
        grid=(n // tm,),
        in_specs=[tok, tok, _const_spec(wo.shape), _const_spec(gm.shape), _const_spec(wrt.shape)],
        out_specs=[tok, pl.BlockSpec((SUBROWS, tm, sw), lambda i: (0, i, 0)),
                   pl.BlockSpec((ne, tm), lambda i: (0, i)),
                   pl.BlockSpec((ne, LANES), lambda i: (0, 0))],
        scratch_shapes=[pltpu.VMEM((ne, LANES), F32)],
        compiler_params=_params(("arbitrary",)),
        name="out_router",
    )(og, h, wo, gm, wrt)


def _sc_mesh():
    return plsc.VectorSubcoreMesh(core_axis_name="core", subcore_axis_name="subcore")


def _sc_scatter(x, idx, out_rows):
    rows, w = x.shape
    n_idx = idx.shape[0]

    @pl.kernel(out_type=jax.ShapeDtypeStruct((out_rows, w), x.dtype), mesh=_sc_mesh(),
               scratch_types=[])
    def scatter_rows(x_hbm, i_hbm, o_hbm):
        def body(x_vmem, i_vmem):
            for s in range(n_idx):
                pltpu.sync_copy(x_vmem, o_hbm.at[i_vmem.at[s]])

        pltpu.emit_pipeline(
            body, grid=(rows // SC_WINDOW,),
            in_specs=[pl.BlockSpec((SC_WINDOW, w), lambda i: (i, 0)),
                      pl.BlockSpec((n_idx, SC_WINDOW), lambda i: (0, i))],
            out_specs=[],
            core_axis_name=("core", "subcore"),
            dimension_semantics=(pltpu.PARALLEL,),
        )(x_hbm, i_hbm)

    return scatter_rows(x, idx)


def _sc_gather(x, idx):
    n = idx.shape[1]
    w = x.shape[1]

    @pl.kernel(out_type=jax.ShapeDtypeStruct((n, w), x.dtype), mesh=_sc_mesh(), scratch_types=[])
    def gather_rows(x_hbm, i_hbm, o_hbm):
        def body(i_vmem, o_vmem):
            pltpu.sync_copy(x_hbm.at[i_vmem.at[0]], o_vmem)

        pltpu.emit_pipeline(
            body, grid=(n // SC_WINDOW,),
            in_specs=[pl.BlockSpec((1, SC_WINDOW), lambda i: (0, i))],
            out_specs=[pl.BlockSpec((SC_WINDOW, w), lambda i: (i, 0))],
            core_axis_name=("core", "subcore"),
            dimension_semantics=(pltpu.PARALLEL,),
        )(i_hbm, o_hbm)

    return gather_rows(x, idx)


def _experts_kernel(te_ref, rows_ref, x_ref, wa_ref, wb_ref, wo_ref, o_ref, xb_ref, hm_ref, *, nf):
    del te_ref
    i = pl.program_id(0)
    f = pl.program_id(1)
    tr = xb_ref.shape[0]
    tf = wa_ref.shape[2]
    n_valid = rows_ref[i]

    @pl.when((n_valid > 0) & (f == 0))
    def _():
        x = _unpack_row([x_ref[c] for c in range(SUBROWS)])
        live = lax.broadcasted_iota(jnp.int32, (tr, 1), 0) < n_valid
        xb_ref[...] = jnp.where(live, x, 0.0).astype(BF16)

    ts = tf // EXPERT_SUBSLABS
    for j in range(nf):
        @pl.when((n_valid > 0) & (f == j))
        def _(j=j):
            xb = xb_ref[...]
            for c in range(EXPERT_SUBSLABS):
                a = _dot(xb, wa_ref[0, :, c * ts:(c + 1) * ts])
                b = _dot(xb, wb_ref[0, :, c * ts:(c + 1) * ts])
                col = j * tf + c * ts
                hm_ref[:, col:col + ts] = (a * jax.nn.sigmoid(a) * b).astype(BF16)

    @pl.when((n_valid > 0) & (f == nf - 1))
    def _():
        for c, sub in enumerate(_pack_row(_dot(hm_ref[...], wo_ref[0]))):
            o_ref[c] = sub

    @pl.when((n_valid == 0) & (f == 0))
    def _():
        o_ref[...] = jnp.zeros_like(o_ref)


def _experts(tile_expert, tile_rows, xs, w_in, w_out, *, tr, tf):
    _, p, sw = xs.shape
    d = w_out.shape[2]
    de = w_out.shape[1]
    nf = de // tf
    rows = pl.BlockSpec((SUBROWS, tr, sw), lambda i, f, te, tv: (0, i, 0))
    return pl.pallas_call(
        functools.partial(_experts_kernel, nf=nf),
        out_shape=jax.ShapeDtypeStruct((SUBROWS, p, sw), jnp.uint32),
        grid_spec=pltpu.PrefetchScalarGridSpec(
            num_scalar_prefetch=2,
            grid=(p // tr, nf),
            in_specs=[
                rows,
                pl.BlockSpec((1, d, tf), lambda i, f, te, tv: (te[i], 0, f)),
                pl.BlockSpec((1, d, tf), lambda i, f, te, tv: (te[i], 0, nf + f)),
                pl.BlockSpec((1, de, d), lambda i, f, te, tv: (te[i], 0, 0),
                             pipeline_mode=pl.Buffered(1)),
            ],
            out_specs=rows,
            scratch_shapes=[pltpu.VMEM((tr, d), BF16), pltpu.VMEM((tr, de), BF16)],
        ),
        compiler_params=_params(("arbitrary", "arbitrary")),
        name="experts",
    )(tile_expert, tile_rows, xs, w_in, w_in, w_out)


def _combine_kernel(g_ref, h_ref, w_ref, o_ref):
    w = w_ref[...]
    y = [_unpack_row([g_ref[s, c] for c in range(SUBROWS)]) for s in range(TOP_K)]
    o_ref[...] = h_ref[...] + (w[:, 0:1] * y[0] + w[:, 1:2] * y[1])


def _combine(g, h, w, *, tc):
    n, d = h.shape
    sw = g.shape[3]
    tok = pl.BlockSpec((tc, d), lambda i: (i, 0))
    return pl.pallas_call(
        _combine_kernel,
        out_shape=jax.ShapeDtypeStruct((n, d), F32),
        grid=(n // tc,),
        in_specs=[pl.BlockSpec((TOP_K, SUBROWS, tc, sw), lambda i: (0, 0, i, 0)), tok,
                  pl.BlockSpec((tc, TOP_K), lambda i: (i, 0))],
        out_specs=tok,
        compiler_params=_params(("parallel",)),
        name="combine",
    )(g, h, w)


def _tiles(n, seq):
    def pick(limit, of):
        t = limit
        while of % t:
            t //= 2
        return t
    return dict(
        tm_a=pick(512, seq), tm_f=pick(512, n), tm_qkv=pick(512, seq), blk=pick(256, seq),
        tm_o=pick(512, n), tr=pick(1024, TOP_K * n), tc=pick(512, n))


def kernel(x, a_norm_g, a_w_in, a_v_norm_g, a_w_spatial, a_b_spatial, a_w_out, f_norm_g, f_w_in, f_w_out, kv_norm_g, kv_w, kv_b_f, k_norm_g, b_norm_g, b_w_in, q_norm_g, b_w_out, m_norm_g, m_w_router, m_w_in, m_w_out):
    batch, seq, d = x.shape
    n = batch * seq
    nh = d // HEAD_DIM
    ne = m_w_router.shape[-1]
    assert a_w_in.shape[0] == 1 and b_w_in.shape[0] == 1 and f_w_in.shape[0] == 1 and m_w_in.shape[0] == 1
    assert seq % GMLP_CHUNK == 0 and d % LANES == 0 and (SUBROWS * n) % SC_WINDOW == 0
    t = _tiles(n, seq)
    row = lambda g: g.reshape(1, -1)

    h = x.reshape(n, d)
    h = _mixer_a(h, row(a_norm_g[0]), a_w_in[0].astype(BF16), row(a_v_norm_g[0]), a_w_spatial[0],
                 a_b_spatial[0].T, a_w_out[0].astype(BF16), tm=t["tm_a"])
    h = _swiglu(h, row(f_norm_g[0]), f_w_in[0].astype(BF16), f_w_out[0].astype(BF16), tm=t["tm_f"])

    head = jnp.arange(d, dtype=jnp.int32) // HEAD_DIM
    hsum = (head[:, None] == head[None, :]).astype(BF16)
    q, k, vt, sg, fp = _qkv(
        h, row(kv_norm_g), row(b_norm_g[0]), kv_w[:, :d].astype(BF16),
        kv_w[:, d:2 * d].T.astype(BF16), kv_w[:, 2 * d:].T.astype(BF16), kv_b_f.reshape(nh, 1),
        row(jnp.tile(k_norm_g, nh)), row(jnp.tile(q_norm_g[0], nh)), b_w_in[0].astype(BF16), hsum,
        tm=t["tm_qkv"], batch=batch, seq=seq)
    og, (moe_w_in, moe_w_out) = _attention(q, k, vt, jnp.swapaxes(fp, 1, 2), sg,
                                           (m_w_in[0], m_w_out[0]), batch=batch, seq=seq, blk=t["blk"])

    h2, xp, meta, cnt = _out_router(og, h, b_w_out[0].astype(BF16), row(m_norm_g[0]),
                                    m_w_router[0].T, tm=t["tm_o"])

    tr = t["tr"]
    n_tiles = TOP_K * n // tr + ne
    p = n_tiles * tr
    counts = cnt[:, 0].astype(jnp.int32)
    tiles_per_expert = (counts + tr - 1) // tr
    tile_end = jnp.cumsum(tiles_per_expert)
    tile_start = tile_end - tiles_per_expert
    expert_ids = jnp.arange(ne, dtype=jnp.int32)
    idx = meta[0:TOP_K].astype(jnp.int32)
    start_of = jnp.sum(jnp.where(idx[:, :, None] == expert_ids, tile_start * tr, 0), axis=-1)
    dest = start_of + meta[4:4 + TOP_K].astype(jnp.int32)
    tile_ids = jnp.arange(n_tiles, dtype=jnp.int32)
    tile_expert = jnp.minimum(
        jnp.sum((tile_ids[:, None] >= tile_end[None, :]).astype(jnp.int32), axis=1), ne - 1)
    mine = tile_expert[:, None] == expert_ids[None, :]
    tile_rows = jnp.clip(
        jnp.sum(jnp.where(mine, counts - (tile_ids[:, None] - tile_start) * tr, 0), axis=1), 0, tr)
    tile_rows = jnp.where(tile_ids < tile_end[-1], tile_rows, 0).astype(jnp.int32)

    sub = (jnp.arange(SUBROWS, dtype=jnp.int32) * p)[None, :, None]
    sub_dest = sub + dest[:, None, :]
    sw = xp.shape[2]
    xs = _sc_scatter(xp.reshape(SUBROWS * n, sw), sub_dest.reshape(TOP_K, SUBROWS * n), SUBROWS * p)
    eo = _experts(tile_expert, tile_rows, xs.reshape(SUBROWS, p, sw), moe_w_in, moe_w_out,
                  tr=tr, tf=m_w_out.shape[2] // EXPERT_STEPS)
    g = _sc_gather(eo.reshape(SUBROWS * p, sw), sub_dest.reshape(1, TOP_K * SUBROWS * n))
    out = _combine(g.reshape(TOP_K, SUBROWS, n, sw), h2, meta[2:2 + TOP_K].T, tc=t["tc"])
    return out.reshape(batch, seq, d)
```

```python
import functools

import jax
import jax.numpy as jnp
from jax import lax
from jax.experimental import pallas as pl
from jax.experimental.pallas import tpu as pltpu
from jax.experimental.pallas import tpu_sc as plsc

RMS_EPS = 1e-6
NEG_INF = -1e30
LOG2E = 1.4426950408889634
N_PIECES = 3
SC_WINDOW = 128
SUBROWS = 2
EXPERT_STEPS = 2
EXPERT_SUBSLABS = 2
ROW_BUCKET = 256
GMLP_CHUNK = 128
CAUSAL_CHUNK = 64
A_GROUPS = 8
HEAD_DIM = 64
LANES = 128
TOP_K = 2
VMEM_LIMIT = 56 * 1024 * 1024

BF16 = jnp.bfloat16
F32 = jnp.float32


def _dot(a, b, **kw):
    return jnp.dot(a, b, preferred_element_type=F32, **kw)


def _dot_nt(a, b, **kw):
    return lax.dot_general(a, b, (((1,), (1,)), ((), ())), preferred_element_type=F32, **kw)


def _rms(x, g):
    return x * lax.rsqrt(jnp.mean(x * x, axis=-1, keepdims=True) + RMS_EPS) * g


def _gelu(x):
    return 0.5 * x * (1.0 + lax.erf(x * (2.0 ** -0.5)))


def _const_spec(shape):
    nd = len(shape)
    return pl.BlockSpec(shape, lambda *_: (0,) * nd, pipeline_mode=pl.Buffered(1))


def _params(sem):
    return pltpu.CompilerParams(dimension_semantics=sem, vmem_limit_bytes=VMEM_LIMIT)


def _mixer_a_kernel(x_ref, g_ref, win_ref, gv_ref, ws_ref, bs_ref, wout_ref, o_ref, z_ref):
    tm = x_ref.shape[0]
    half = wout_ref.shape[0]
    gd = half // A_GROUPS
    x = x_ref[...]
    xb = _rms(x, g_ref[...]).astype(BF16)
    v = _gelu(_dot(xb, win_ref[:, half:]))
    v = (_rms(v, gv_ref[...])).astype(BF16)
    u = _gelu(_dot(xb, win_ref[:, :half]))
    row = lax.broadcasted_iota(jnp.int32, (GMLP_CHUNK, GMLP_CHUNK), 0)
    col = lax.broadcasted_iota(jnp.int32, (GMLP_CHUNK, GMLP_CHUNK), 1)
    keep = (col // CAUSAL_CHUNK) <= (row // CAUSAL_CHUNK)
    bs = bs_ref[...]
    for g in range(A_GROUPS):
        wg = jnp.where(keep, ws_ref[g], 0.0).astype(BF16)
        bg = bs[:, g:g + 1]
        for c in range(tm // GMLP_CHUNK):
            rs = slice(c * GMLP_CHUNK, (c + 1) * GMLP_CHUNK)
            cs = slice(g * gd, (g + 1) * gd)
            sv = _dot(wg, v[rs, cs]) + bg
            z_ref[rs, cs] = (u[rs, cs] * sv).astype(BF16)
    o_ref[...] = x + _dot(z_ref[...], wout_ref[...])


def _mixer_a(h, g, w_in, gv, ws, bs_t, w_out, *, tm):
    n, d = h.shape
    half = w_out.shape[0]
    return pl.pallas_call(
        _mixer_a_kernel,
        out_shape=jax.ShapeDtypeStruct((n, d), F32),
        grid=(n // tm,),
        in_specs=[
            pl.BlockSpec((tm, d), lambda i: (i, 0)),
            _const_spec(g.shape), _const_spec(w_in.shape), _const_spec(gv.shape),
            _const_spec(ws.shape), _const_spec(bs_t.shape), _const_spec(w_out.shape),
        ],
        out_specs=pl.BlockSpec((tm, d), lambda i: (i, 0)),
        scratch_shapes=[pltpu.VMEM((tm, half), BF16)],
        compiler_params=_params(("parallel",)),
        name="mixer_a",
    )(h, g, w_in, gv, ws, bs_t, w_out)


def _swiglu_kernel(x_ref, g_ref, win_ref, wout_ref, o_ref):
    f = wout_ref.shape[0]
    x = x_ref[...]
    xb = _rms(x, g_ref[...]).astype(BF16)
    a = _dot(xb, win_ref[:, :f])
    b = _dot(xb, win_ref[:, f:])
    hm = (a * jax.nn.sigmoid(a) * b).astype(BF16)
    o_ref[...] = x + _dot(hm, wout_ref[...])


def _swiglu(h, g, w_in, w_out, *, tm):
    n, d = h.shape
    return pl.pallas_call(
        _swiglu_kernel,
        out_shape=jax.ShapeDtypeStruct((n, d), F32),
        grid=(n // tm,),
        in_specs=[
            pl.BlockSpec((tm, d), lambda i: (i, 0)),
            _const_spec(g.shape), _const_spec(w_in.shape), _const_spec(w_out.shape),
        ],
        out_specs=pl.BlockSpec((tm, d), lambda i: (i, 0)),
        compiler_params=_params(("parallel",)),
        name="swiglu",
    )(h, g, w_in, w_out)


def _lane_cumsum(x):
    n = x.shape[-1]
    lane = lax.broadcasted_iota(jnp.int32, x.shape, x.ndim - 1)
    sh = 1
    while sh < n:
        x = x + jnp.where(lane >= sh, pltpu.roll(x, sh, axis=x.ndim - 1), 0.0)
        sh *= 2
    return x


def _split_bf16(x):
    pieces = []
    for _ in range(N_PIECES):
        p = x.astype(BF16)
        pieces.append(p)
        x = x - p.astype(F32)
    return pieces


def _qkv_kernel(x_ref, gkv_ref, gb_ref, wk_ref, wvt_ref, wft_ref, bf_ref, gk_ref, gq_ref, wqg_ref,
                hsum_ref, q_ref, k_ref, vt_ref, sg_ref, fp_ref, carry_ref, *, tiles_per_seq):
    d = x_ref.shape[1]
    i = pl.program_id(0)

    @pl.when(i == 0)
    def _():
        carry_ref[...] = jnp.zeros_like(carry_ref)

    x = x_ref[...]
    y = x * lax.rsqrt(jnp.mean(x * x, axis=-1, keepdims=True) + RMS_EPS)
    skv = (y * gkv_ref[...]).astype(BF16)
    sb = (y * gb_ref[...]).astype(BF16)
    hsum = hsum_ref[...]

    def head_norm(t, gain):
        ssq = _dot((t * t).astype(BF16), hsum)
        return t * lax.rsqrt(ssq * (1.0 / HEAD_DIM) + RMS_EPS) * gain

    f = _dot_nt(wft_ref[...], skv) + bf_ref[...]
    logf = jax.nn.log_sigmoid(f)
    carry = jnp.where(i % tiles_per_seq == 0, 0.0, carry_ref[:, 0:1])
    cum = _lane_cumsum(logf) + carry
    carry_ref[...] = jnp.broadcast_to(cum[:, -1:], carry_ref.shape)
    fp_ref[0] = jnp.concatenate(_split_bf16(cum * LOG2E), axis=0)

    k = _dot(skv, wk_ref[...])
    k_ref[...] = head_norm(k, gk_ref[...]).astype(BF16)
    vt_ref[0] = _dot_nt(wvt_ref[...], skv).astype(BF16)
    q = _dot(sb, wqg_ref[:, :d])
    q_ref[...] = (head_norm(q, gq_ref[...]) * (LOG2E * HEAD_DIM ** -0.5)).astype(BF16)
    sg_ref[...] = jax.nn.sigmoid(_dot(sb, wqg_ref[:, d:])).astype(BF16)


def _qkv(h, gkv, gb, wk, wvt, wft, bf, gk, gq, wqg, hsum, *, tm, batch, seq):
    n, d = h.shape
    nh = wft.shape[0]
    tps = seq // tm
    tok = pl.BlockSpec((tm, d), lambda i: (i, 0))
    tok_bf = jax.ShapeDtypeStruct((n, d), BF16)
    seq_map = lambda i: (i // tps, 0, i % tps)
    consts = (gkv, gb, wk, wvt, wft, bf, gk, gq, wqg, hsum)
    return pl.pallas_call(
        functools.partial(_qkv_kernel, tiles_per_seq=tps),
        out_shape=[tok_bf, tok_bf, jax.ShapeDtypeStruct((batch, d, seq), BF16), tok_bf,
                   jax.ShapeDtypeStruct((batch, N_PIECES * nh, seq), BF16)],
        grid=(n // tm,),
        in_specs=[tok] + [_const_spec(a.shape) for a in consts],
        out_specs=[tok, tok, pl.BlockSpec((1, d, tm), seq_map), tok,
                   pl.BlockSpec((1, N_PIECES * nh, tm), seq_map)],
        scratch_shapes=[pltpu.VMEM((nh, LANES), F32)],
        compiler_params=_params(("arbitrary",)),
        name="qkv",
    )(h, *consts)


def _attention_kernel(q_ref, k_ref, vt_ref, fp_ref, sg_ref, *rest, blk, nq, n_riders):
    rider_in, (o_ref, *rider_out) = rest[:n_riders], rest[n_riders:2 * n_riders + 1]
    kaug_ref, qaug_ref, vsum_ref = rest[2 * n_riders + 1:]
    for w_ref, c_ref in zip(rider_in, rider_out):
        c_ref[...] = w_ref[...].astype(BF16)
    hp = pl.program_id(1)
    nh = fp_ref.shape[2] // N_PIECES
    lane = lax.broadcasted_iota(jnp.int32, (1, LANES), 1)
    own = (lane < HEAD_DIM, lane >= HEAD_DIM)
    spare = (HEAD_DIM, 0)
    q_lanes = [(lane >= s) & (lane < s + N_PIECES) for s in spare]
    k_lanes = [(lane >= s + N_PIECES) & (lane < s + 2 * N_PIECES) for s in spare]

    er = lax.broadcasted_iota(jnp.int32, (N_PIECES * nh, LANES), 0)
    ec = lax.broadcasted_iota(jnp.int32, (N_PIECES * nh, LANES), 1)
    place = jnp.zeros((N_PIECES * nh, LANES), F32)
    for h in range(2):
        for j in range(N_PIECES):
            src = er == j * nh + 2 * hp + h
            place = (place + (src & (ec == spare[h] + j)).astype(F32)
                     - (src & (ec == spare[h] + N_PIECES + j)).astype(F32))
    g = _dot(fp_ref[0], place.astype(BF16))
    for h in range(2):
        k_fill = jnp.where(k_lanes[h], g, q_lanes[h].astype(F32)).astype(BF16)
        kaug_ref[h] = jnp.where(own[h], k_ref[...], k_fill)
        q_fill = jnp.where(q_lanes[h], g, k_lanes[h].astype(F32)).astype(BF16)
        qaug_ref[h] = jnp.where(own[h], q_ref[...], q_fill)

    first_rows = lax.broadcasted_iota(jnp.int32, (LANES, 1), 0) < HEAD_DIM
    vsum_ref[0] = jnp.where(first_rows, vt_ref[0], jnp.ones_like(vt_ref[0]))
    vsum_ref[1] = jnp.where(first_rows, jnp.ones_like(vt_ref[0]), vt_ref[0])
    key = lax.broadcasted_iota(jnp.int32, (blk, blk), 0)
    qry = lax.broadcasted_iota(jnp.int32, (blk, blk), 1)
    causal = qry >= key

    def qk(i):
        lo, hi = i * blk, (i + 1) * blk
        return [_dot_nt(kaug_ref[h, 0:hi, :], qaug_ref[h, lo:hi, :]) for h in range(2)]

    scores = qk(0)
    for i in range(nq):
        lo, hi = i * blk, (i + 1) * blk
        next_scores = qk(i + 1) if i + 1 < nq else None
        probs = []
        for s in scores:
            diag = jnp.where(causal, s[lo:hi], NEG_INF)
            s = diag if i == 0 else jnp.concatenate([s[0:lo], diag], axis=0)
            probs.append(jnp.exp2(s - jnp.max(s, axis=0, keepdims=True)).astype(BF16))
        outs = [_dot(vsum_ref[h, :, 0:hi], probs[h]) for h in range(2)]
        o0 = outs[0] / outs[0][HEAD_DIM:HEAD_DIM + 1, :]
        o1 = outs[1] / outs[1][0:1, :]
        o = jnp.where(first_rows, o0, o1).T
        o_ref[lo:hi, :] = (o * sg_ref[lo:hi, :].astype(F32)).astype(BF16)
        scores = next_scores


def _attention(q, k, vt, fp, sg, riders, *, batch, seq, blk):
    n, d = q.shape
    n_pairs = d // LANES
    steps = batch * n_pairs
    pair = pl.BlockSpec((seq, LANES), lambda b, hp: (b, hp))
    slabs = [w.reshape(steps, -1, w.shape[-1]) for w in riders]
    slab_specs = [pl.BlockSpec((1,) + s.shape[1:], lambda b, hp: (b * n_pairs + hp, 0, 0))
                  for s in slabs]
    out = pl.pallas_call(
        functools.partial(_attention_kernel, blk=blk, nq=seq // blk, n_riders=len(riders)),
        out_shape=[jax.ShapeDtypeStruct((n, d), BF16)]
        + [jax.ShapeDtypeStruct(s.shape, BF16) for s in slabs],
        grid=(batch, n_pairs),
        in_specs=[
            pair, pair,
            pl.BlockSpec((1, LANES, seq), lambda b, hp: (b, hp, 0)),
            pl.BlockSpec((1, seq, fp.shape[2]), lambda b, hp: (b, 0, 0)),
            pair,
        ] + slab_specs,
        out_specs=[pair] + slab_specs,
        scratch_shapes=[pltpu.VMEM((2, seq, LANES), BF16)] * 2 + [pltpu.VMEM((2, LANES, seq), BF16)],
        compiler_params=_params(("parallel", "parallel")),
        name="attention",
    )(q, k, vt, fp, sg, *slabs)
    return out[0], [c.reshape(w.shape) for c, w in zip(out[1:], riders)]


def _pack_bf16_pair(lo, hi):
    ulo = lax.bitcast_convert_type(lo.astype(BF16).astype(F32), jnp.uint32)
    uhi = lax.bitcast_convert_type(hi.astype(BF16).astype(F32), jnp.uint32)
    return (ulo >> 16) | uhi


def _pack_row(x):
    half = x.shape[1] // 2
    w = _pack_bf16_pair(x[:, :half], x[:, half:])
    sw = half // SUBROWS
    return [w[:, c * sw:(c + 1) * sw] for c in range(SUBROWS)]


def _unpack_row(subrows):
    lo = [lax.bitcast_convert_type(p << 16, F32) for p in subrows]
    hi = [lax.bitcast_convert_type(p & jnp.uint32(0xFFFF0000), F32) for p in subrows]
    return jnp.concatenate(lo + hi, axis=1)


def _out_router_kernel(og_ref, h_ref, wo_ref, gm_ref, wrt_ref, h2_ref, xp_ref, meta_ref, cnt_ref,
                       carry_ref):
    i = pl.program_id(0)
    ne, tm = meta_ref.shape
    h2 = h_ref[...] + _dot(og_ref[...], wo_ref[...])
    h2_ref[...] = h2
    xn = _rms(h2, gm_ref[...])
    for c, sub in enumerate(_pack_row(xn)):
        xp_ref[c] = sub
    xh = xn.astype(BF16)
    xl = (xn - xh.astype(F32)).astype(BF16)
    wr = wrt_ref[...]
    wh = wr.astype(BF16)
    wl = (wr - wh.astype(F32)).astype(BF16)
    logits = _dot_nt(wh, xh) + (_dot_nt(wh, xl) + _dot_nt(wl, xh))
    row = lax.broadcasted_iota(jnp.int32, (ne, tm), 0).astype(F32)
    v1 = jnp.max(logits, axis=0, keepdims=True)
    i1 = jnp.min(jnp.where(logits == v1, row, ne), axis=0, keepdims=True)
    rest = jnp.where(row == i1, -jnp.inf, logits)
    v2 = jnp.max(rest, axis=0, keepdims=True)
    i2 = jnp.min(jnp.where(rest == v2, row, ne), axis=0, keepdims=True)
    e = jnp.exp(v2 - v1)
    w1 = 1.0 / (1.0 + e)
    w2 = e / (1.0 + e)
    sel1 = row == i1
    sel2 = row == i2
    oh = (sel1 | sel2).astype(F32)

    @pl.when(i == 0)
    def _():
        carry_ref[...] = jnp.zeros_like(carry_ref)

    base = carry_ref[:, 0:1]
    rank = base + (_lane_cumsum(oh) - oh)
    r1 = jnp.sum(jnp.where(sel1, rank, 0.0), axis=0, keepdims=True)
    r2 = jnp.sum(jnp.where(sel2, rank, 0.0), axis=0, keepdims=True)
    total = base + jnp.sum(oh, axis=1, keepdims=True)
    carry_ref[...] = jnp.broadcast_to(total, carry_ref.shape)
    cnt_ref[...] = jnp.broadcast_to(total, cnt_ref.shape)
    zeros = jnp.zeros_like(w1)
    meta_ref[...] = jnp.concatenate(
        [i1.astype(F32), i2.astype(F32), w1, w2, r1, r2, zeros, zeros], axis=0)


def _out_router(og, h, wo, gm, wrt, *, tm):
    n, d = h.shape
    ne = wrt.shape[0]
    sw = d // 2 // SUBROWS
    tok = pl.BlockSpec((tm, d), lambda i: (i, 0))
    return pl.pallas_call(
        _out_router_kernel,
        out_shape=[jax.ShapeDtypeStruct((n, d), F32), jax.ShapeDtypeStruct((SUBROWS, n, sw), jnp.uint32),
                   jax.ShapeDtypeStruct((ne, n), F32), jax.ShapeDtypeStruct((ne, LANES), F32)],
        grid=(n // tm,),
        in_specs=[tok, tok, _const_spec(wo.shape), _const_spec(gm.shape), _const_spec(wrt.shape)],
        out_specs=[tok, pl.BlockSpec((SUBROWS, tm, sw), lambda i: (0, i, 0)),
                   pl.BlockSpec((ne, tm), lambda i: (0, i)),
                   pl.BlockSpec((ne, LANES), lambda i: (0, 0))],
        scratch_shapes=[pltpu.VMEM((ne, LANES), F32)],
        compiler_params=_params(("arbitrary",)),
        name="out_router",
    )(og, h, wo, gm, wrt)


def _sc_mesh():
    return plsc.VectorSubcoreMesh(core_axis_name="core", subcore_axis_name="subcore")


def _sc_scatter(x, idx, out_rows):
    rows, w = x.shape
    n_idx = idx.shape[0]

    @pl.kernel(out_type=jax.ShapeDtypeStruct((out_rows, w), x.dtype), mesh=_sc_mesh(),
               scratch_types=[])
    def scatter_rows(x_hbm, i_hbm, o_hbm):
        def body(x_vmem, i_vmem):
            for s in range(n_idx):
                pltpu.sync_copy(x_vmem, o_hbm.at[i_vmem.at[s]])

        pltpu.emit_pipeline(
            body, grid=(rows // SC_WINDOW,),
            in_specs=[pl.BlockSpec((SC_WINDOW, w), lambda i: (i, 0)),
                      pl.BlockSpec((n_idx, SC_WINDOW), lambda i: (0, i))],
            out_specs=[],
            core_axis_name=("core", "subcore"),
            dimension_semantics=(pltpu.PARALLEL,),
        )(x_hbm, i_hbm)

    return scatter_rows(x, idx)


def _sc_gather(x, idx):
    n = idx.shape[1]
    w = x.shape[1]

    @pl.kernel(out_type=jax.ShapeDtypeStruct((n, w), x.dtype), mesh=_sc_mesh(), scratch_types=[])
    def gather_rows(x_hbm, i_hbm, o_hbm):
        def body(i_vmem, o_vmem):
            pltpu.sync_copy(x_hbm.at[i_vmem.at[0]], o_vmem)

        pltpu.emit_pipeline(
            body, grid=(n // SC_WINDOW,),
            in_specs=[pl.BlockSpec((1, SC_WINDOW), lambda i: (0, i))],
            out_specs=[pl.BlockSpec((SC_WINDOW, w), lambda i: (i, 0))],
            core_axis_name=("core", "subcore"),
            dimension_semantics=(pltpu.PARALLEL,),
        )(i_hbm, o_hbm)

    return gather_rows(x, idx)


def _experts_kernel(te_ref, rows_ref, x_ref, wa_ref, wb_ref, wo_ref, o_ref, xb_ref, hm_ref, *, nf):
    del te_ref
    i = pl.program_id(0)
    f = pl.program_id(1)
    tr = xb_ref.shape[0]
    tf = wa_ref.shape[2]
    n_valid = rows_ref[i]

    @pl.when((n_valid > 0) & (f == 0))
    def _():
        x = _unpack_row([x_ref[c] for c in range(SUBROWS)])
        live = lax.broadcasted_iota(jnp.int32, (tr, 1), 0) < n_valid
        xb_ref[...] = jnp.where(live, x, 0.0).astype(BF16)

    ts = tf // EXPERT_SUBSLABS
    bucket = (n_valid + ROW_BUCKET - 1) // ROW_BUCKET
    for rows in range(ROW_BUCKET, tr + 1, ROW_BUCKET):
        for j in range(nf):
            @pl.when((bucket == rows // ROW_BUCKET) & (f == j))
            def _(j=j, rows=rows):
                xb = xb_ref[0:rows, :]
                for c in range(EXPERT_SUBSLABS):
                    a = _dot(xb, wa_ref[0, :, c * ts:(c + 1) * ts])
                    b = _dot(xb, wb_ref[0, :, c * ts:(c + 1) * ts])
                    col = j * tf + c * ts
                    hm_ref[0:rows, col:col + ts] = (a * jax.nn.sigmoid(a) * b).astype(BF16)
                if j == nf - 1:
                    y = _dot(hm_ref[0:rows, :], wo_ref[0])
                    if rows < tr:
                        y = jnp.concatenate([y, jnp.zeros((tr - rows, y.shape[1]), F32)], axis=0)
                    for c, sub in enumerate(_pack_row(y)):
                        o_ref[c] = sub

    @pl.when((n_valid == 0) & (f == 0))
    def _():
        o_ref[...] = jnp.zeros_like(o_ref)


def _experts(tile_expert, tile_rows, xs, w_in, w_out, *, tr, tf):
    _, p, sw = xs.shape
    d = w_out.shape[2]
    de = w_out.shape[1]
    nf = de // tf
    rows = pl.BlockSpec((SUBROWS, tr, sw), lambda i, f, te, tv: (0, i, 0))
    return pl.pallas_call(
        functools.partial(_experts_kernel, nf=nf),
        out_shape=jax.ShapeDtypeStruct((SUBROWS, p, sw), jnp.uint32),
        grid_spec=pltpu.PrefetchScalarGridSpec(
            num_scalar_prefetch=2,
            grid=(p // tr, nf),
            in_specs=[
                rows,
                pl.BlockSpec((1, d, tf), lambda i, f, te, tv: (te[i], 0, f)),
                pl.BlockSpec((1, d, tf), lambda i, f, te, tv: (te[i], 0, nf + f)),
                pl.BlockSpec((1, de, d), lambda i, f, te, tv: (te[i], 0, 0),
                             pipeline_mode=pl.Buffered(1)),
            ],
            out_specs=rows,
            scratch_shapes=[pltpu.VMEM((tr, d), BF16), pltpu.VMEM((tr, de), BF16)],
        ),
        compiler_params=_params(("arbitrary", "arbitrary")),
        name="experts",
    )(tile_expert, tile_rows, xs, w_in, w_in, w_out)


def _combine_kernel(g_ref, h_ref, w_ref, o_ref):
    w = w_ref[...]
    y = [_unpack_row([g_ref[s, c] for c in range(SUBROWS)]) for s in range(TOP_K)]
    o_ref[...] = h_ref[...] + (w[:, 0:1] * y[0] + w[:, 1:2] * y[1])


def _combine(g, h, w, *, tc):
    n, d = h.shape
    sw = g.shape[3]
    tok = pl.BlockSpec((tc, d), lambda i: (i, 0))
    return pl.pallas_call(
        _combine_kernel,
        out_shape=jax.ShapeDtypeStruct((n, d), F32),
        grid=(n // tc,),
        in_specs=[pl.BlockSpec((TOP_K, SUBROWS, tc, sw), lambda i: (0, 0, i, 0)), tok,
                  pl.BlockSpec((tc, TOP_K), lambda i: (i, 0))],
        out_specs=tok,
        compiler_params=_params(("parallel",)),
        name="combine",
    )(g, h, w)


def _tiles(n, seq):
    def pick(limit, of):
        t = limit
        while of % t:
            t //= 2
        return t
    return dict(
        tm_a=pick(512, seq), tm_f=pick(512, n), tm_qkv=pick(1024, seq), blk=pick(256, seq),
        tm_o=pick(1024, n), tr=pick(1024, TOP_K * n), tc=pick(1024, n))


def kernel(x, a_norm_g, a_w_in, a_v_norm_g, a_w_spatial, a_b_spatial, a_w_out, f_norm_g, f_w_in, f_w_out, kv_norm_g, kv_w, kv_b_f, k_norm_g, b_norm_g, b_w_in, q_norm_g, b_w_out, m_norm_g, m_w_router, m_w_in, m_w_out):
    batch, seq, d = x.shape
    n = batch * seq
    nh = d // HEAD_DIM
    ne = m_w_router.shape[-1]
    assert a_w_in.shape[0] == 1 and b_w_in.shape[0] == 1 and f_w_in.shape[0] == 1 and m_w_in.shape[0] == 1
    assert seq % GMLP_CHUNK == 0 and d % LANES == 0 and (SUBROWS * n) % SC_WINDOW == 0
    t = _tiles(n, seq)
    row = lambda g: g.reshape(1, -1)

    h = x.reshape(n, d)
    h = _mixer_a(h, row(a_norm_g[0]), a_w_in[0].astype(BF16), row(a_v_norm_g[0]), a_w_spatial[0],
                 a_b_spatial[0].T, a_w_out[0].astype(BF16), tm=t["tm_a"])
    h = _swiglu(h, row(f_norm_g[0]), f_w_in[0].astype(BF16), f_w_out[0].astype(BF16), tm=t["tm_f"])

    head = jnp.arange(d, dtype=jnp.int32) // HEAD_DIM
    hsum = (head[:, None] == head[None, :]).astype(BF16)
    q, k, vt, sg, fp = _qkv(
        h, row(kv_norm_g), row(b_norm_g[0]), kv_w[:, :d].astype(BF16),
        kv_w[:, d:2 * d].T.astype(BF16), kv_w[:, 2 * d:].T.astype(BF16), kv_b_f.reshape(nh, 1),
        row(jnp.tile(k_norm_g, nh)), row(jnp.tile(q_norm_g[0], nh)), b_w_in[0].astype(BF16), hsum,
        tm=t["tm_qkv"], batch=batch, seq=seq)
    og, (moe_w_in, moe_w_out) = _attention(q, k, vt, jnp.swapaxes(fp, 1, 2), sg,
                                           (m_w_in[0], m_w_out[0]), batch=batch, seq=seq, blk=t["blk"])

    h2, xp, meta, cnt = _out_router(og, h, b_w_out[0].astype(BF16), row(m_norm_g[0]),
                                    m_w_router[0].T, tm=t["tm_o"])

    tr = t["tr"]
    n_tiles = TOP_K * n // tr + ne
    p = n_tiles * tr
    counts = cnt[:, 0].astype(jnp.int32)
    tiles_per_expert = (counts + tr - 1) // tr
    tile_end = jnp.cumsum(tiles_per_expert)
    tile_start = tile_end - tiles_per_expert
    expert_ids = jnp.arange(ne, dtype=jnp.int32)
    idx = meta[0:TOP_K].astype(jnp.int32)
    start_of = jnp.sum(jnp.where(idx[:, :, None] == expert_ids, tile_start * tr, 0), axis=-1)
    dest = start_of + meta[4:4 + TOP_K].astype(jnp.int32)
    tile_ids = jnp.arange(n_tiles, dtype=jnp.int32)
    tile_expert = jnp.minimum(
        jnp.sum((tile_ids[:, None] >= tile_end[None, :]).astype(jnp.int32), axis=1), ne - 1)
    mine = tile_expert[:, None] == expert_ids[None, :]
    tile_rows = jnp.clip(
        jnp.sum(jnp.where(mine, counts - (tile_ids[:, None] - tile_start) * tr, 0), axis=1), 0, tr)
    tile_rows = jnp.where(tile_ids < tile_end[-1], tile_rows, 0).astype(jnp.int32)

    sub = (jnp.arange(SUBROWS, dtype=jnp.int32) * p)[None, :, None]
    sub_dest = sub + dest[:, None, :]
    sw = xp.shape[2]
    xs = _sc_scatter(xp.reshape(SUBROWS * n, sw), sub_dest.reshape(TOP_K, SUBROWS * n), SUBROWS * p)
    eo = _experts(tile_expert, tile_rows, xs.reshape(SUBROWS, p, sw), moe_w_in, moe_w_out,
                  tr=tr, tf=m_w_out.shape[2] // EXPERT_STEPS)
    g = _sc_gather(eo.reshape(SUBROWS * p, sw), sub_dest.reshape(1, TOP_K * SUBROWS * n))
    out = _combine(g.reshape(TOP_K, SUBROWS, n, sw), h2, meta[2:2 + TOP_K].T, tc=t["tc"])
    return out.reshape(batch, seq, d)
```

```python
import functools

import jax
import jax.numpy as jnp
from jax import lax
from jax.experimental import pallas as pl
from jax.experimental.pallas import tpu as pltpu
from jax.experimental.pallas import tpu_sc as plsc

RMS_EPS = 1e-6
NEG_INF = -1e30
LOG2E = 1.4426950408889634
N_PIECES = 3
SC_WINDOW = 128
SUBROWS = 2
EXPERT_STEPS = 2
EXPERT_SUBSLABS = 2
GMLP_CHUNK = 128
CAUSAL_CHUNK = 64
A_GROUPS = 8
HEAD_DIM = 64
LANES = 128
TOP_K = 2
VMEM_LIMIT = 56 * 1024 * 1024

BF16 = jnp.bfloat16
F32 = jnp.float32


def _dot(a, b, **kw):
    return jnp.dot(a, b, preferred_element_type=F32, **kw)


def _dot_nt(a, b, **kw):
    return lax.dot_general(a, b, (((1,), (1,)), ((), ())), preferred_element_type=F32, **kw)


def _rms(x, g):
    return x * lax.rsqrt(jnp.mean(x * x, axis=-1, keepdims=True) + RMS_EPS) * g


def _gelu(x):
    return 0.5 * x * (1.0 + lax.erf(x * (2.0 ** -0.5)))


def _const_spec(shape):
    nd = len(shape)
    return pl.BlockSpec(shape, lambda *_: (0,) * nd, pipeline_mode=pl.Buffered(1))


def _params(sem):
    return pltpu.CompilerParams(dimension_semantics=sem, vmem_limit_bytes=VMEM_LIMIT)


def _rider_slabs(riders, steps, step_index):
    slabs = [w.reshape(steps, -1, w.shape[-1]) for w in riders]
    specs = [pl.BlockSpec((1,) + s.shape[1:], lambda *g: (step_index(*g), 0, 0)) for s in slabs]
    return slabs, specs


def _split_rider_refs(rest, n_riders):
    return rest[:n_riders], rest[n_riders], rest[n_riders + 1:2 * n_riders + 1], rest[2 * n_riders + 1:]


def _convert_riders(rider_in, rider_out):
    for w_ref, c_ref in zip(rider_in, rider_out):
        c_ref[...] = w_ref[...].astype(BF16)


def _unslab(outs, riders):
    return [c.reshape(w.shape) for c, w in zip(outs, riders)]


def _mixer_a_kernel(x_ref, g_ref, win_ref, gv_ref, ws_ref, bs_ref, wout_ref, *rest, n_riders):
    rider_in, o_ref, rider_out, (z_ref,) = _split_rider_refs(rest, n_riders)
    _convert_riders(rider_in, rider_out)
    tm = x_ref.shape[0]
    half = wout_ref.shape[0]
    gd = half // A_GROUPS
    x = x_ref[...]
    xb = _rms(x, g_ref[...]).astype(BF16)
    v = _gelu(_dot(xb, win_ref[:, half:]))
    v = (_rms(v, gv_ref[...])).astype(BF16)
    u = _gelu(_dot(xb, win_ref[:, :half]))
    row = lax.broadcasted_iota(jnp.int32, (GMLP_CHUNK, GMLP_CHUNK), 0)
    col = lax.broadcasted_iota(jnp.int32, (GMLP_CHUNK, GMLP_CHUNK), 1)
    keep = (col // CAUSAL_CHUNK) <= (row // CAUSAL_CHUNK)
    bs = bs_ref[...]
    for g in range(A_GROUPS):
        wg = jnp.where(keep, ws_ref[g], 0.0).astype(BF16)
        bg = bs[:, g:g + 1]
        for c in range(tm // GMLP_CHUNK):
            rs = slice(c * GMLP_CHUNK, (c + 1) * GMLP_CHUNK)
            cs = slice(g * gd, (g + 1) * gd)
            sv = _dot(wg, v[rs, cs]) + bg
            z_ref[rs, cs] = (u[rs, cs] * sv).astype(BF16)
    o_ref[...] = x + _dot(z_ref[...], wout_ref[...])


def _mixer_a(h, g, w_in, gv, ws, bs_t, w_out, riders, *, tm):
    n, d = h.shape
    half = w_out.shape[0]
    slabs, slab_specs = _rider_slabs(riders, n // tm, lambda i: i)
    tok = pl.BlockSpec((tm, d), lambda i: (i, 0))
    out = pl.pallas_call(
        functools.partial(_mixer_a_kernel, n_riders=len(riders)),
        out_shape=[jax.ShapeDtypeStruct((n, d), F32)]
        + [jax.ShapeDtypeStruct(sl.shape, BF16) for sl in slabs],
        grid=(n // tm,),
        in_specs=[
            tok,
            _const_spec(g.shape), _const_spec(w_in.shape), _const_spec(gv.shape),
            _const_spec(ws.shape), _const_spec(bs_t.shape), _const_spec(w_out.shape),
        ] + slab_specs,
        out_specs=[tok] + slab_specs,
        scratch_shapes=[pltpu.VMEM((tm, half), BF16)],
        compiler_params=_params(("parallel",)),
        name="mixer_a",
    )(h, g, w_in, gv, ws, bs_t, w_out, *slabs)
    return out[0], _unslab(out[1:], riders)


def _swiglu_kernel(x_ref, g_ref, win_ref, wout_ref, *rest, n_riders):
    rider_in, o_ref, rider_out, _ = _split_rider_refs(rest, n_riders)
    _convert_riders(rider_in, rider_out)
    f = wout_ref.shape[0]
    x = x_ref[...]
    xb = _rms(x, g_ref[...]).astype(BF16)
    a = _dot(xb, win_ref[:, :f])
    b = _dot(xb, win_ref[:, f:])
    hm = (a * jax.nn.sigmoid(a) * b).astype(BF16)
    o_ref[...] = x + _dot(hm, wout_ref[...])


def _swiglu(h, g, w_in, w_out, riders, *, tm):
    n, d = h.shape
    slabs, slab_specs = _rider_slabs(riders, n // tm, lambda i: i)
    tok = pl.BlockSpec((tm, d), lambda i: (i, 0))
    out = pl.pallas_call(
        functools.partial(_swiglu_kernel, n_riders=len(riders)),
        out_shape=[jax.ShapeDtypeStruct((n, d), F32)]
        + [jax.ShapeDtypeStruct(sl.shape, BF16) for sl in slabs],
        grid=(n // tm,),
        in_specs=[tok, _const_spec(g.shape), _const_spec(w_in.shape), _const_spec(w_out.shape)]
        + slab_specs,
        out_specs=[tok] + slab_specs,
        compiler_params=_params(("parallel",)),
        name="swiglu",
    )(h, g, w_in, w_out, *slabs)
    return out[0], _unslab(out[1:], riders)


def _lane_cumsum(x):
    n = x.shape[-1]
    lane = lax.broadcasted_iota(jnp.int32, x.shape, x.ndim - 1)
    sh = 1
    while sh < n:
        x = x + jnp.where(lane >= sh, pltpu.roll(x, sh, axis=x.ndim - 1), 0.0)
        sh *= 2
    return x


def _split_bf16(x):
    pieces = []
    for _ in range(N_PIECES):
        p = x.astype(BF16)
        pieces.append(p)
        x = x - p.astype(F32)
    return pieces


def _qkv_kernel(x_ref, gkv_ref, gb_ref, wk_ref, wvt_ref, wft_ref, bf_ref, gk_ref, gq_ref, wqg_ref,
                hsum_ref, q_ref, k_ref, vt_ref, sg_ref, fp_ref, carry_ref, *, tiles_per_seq):
    d = x_ref.shape[1]
    i = pl.program_id(0)

    @pl.when(i == 0)
    def _():
        carry_ref[...] = jnp.zeros_like(carry_ref)

    x = x_ref[...]
    y = x * lax.rsqrt(jnp.mean(x * x, axis=-1, keepdims=True) + RMS_EPS)
    skv = (y * gkv_ref[...]).astype(BF16)
    sb = (y * gb_ref[...]).astype(BF16)
    hsum = hsum_ref[...]

    def head_norm(t, gain):
        ssq = _dot((t * t).astype(BF16), hsum)
        return t * lax.rsqrt(ssq * (1.0 / HEAD_DIM) + RMS_EPS) * gain

    f = _dot_nt(wft_ref[...], skv) + bf_ref[...]
    logf = jax.nn.log_sigmoid(f)
    carry = jnp.where(i % tiles_per_seq == 0, 0.0, carry_ref[:, 0:1])
    cum = _lane_cumsum(logf) + carry
    carry_ref[...] = jnp.broadcast_to(cum[:, -1:], carry_ref.shape)
    fp_ref[0] = jnp.concatenate(_split_bf16(cum * LOG2E), axis=0)

    k = _dot(skv, wk_ref[...])
    k_ref[...] = head_norm(k, gk_ref[...]).astype(BF16)
    vt_ref[0] = _dot_nt(wvt_ref[...], skv).astype(BF16)
    q = _dot(sb, wqg_ref[:, :d])
    q_ref[...] = (head_norm(q, gq_ref[...]) * (LOG2E * HEAD_DIM ** -0.5)).astype(BF16)
    sg_ref[...] = jax.nn.sigmoid(_dot(sb, wqg_ref[:, d:])).astype(BF16)


def _qkv(h, gkv, gb, wk, wvt, wft, bf, gk, gq, wqg, hsum, *, tm, batch, seq):
    n, d = h.shape
    nh = wft.shape[0]
    tps = seq // tm
    tok = pl.BlockSpec((tm, d), lambda i: (i, 0))
    tok_bf = jax.ShapeDtypeStruct((n, d), BF16)
    seq_map = lambda i: (i // tps, 0, i % tps)
    consts = (gkv, gb, wk, wvt, wft, bf, gk, gq, wqg, hsum)
    return pl.pallas_call(
        functools.partial(_qkv_kernel, tiles_per_seq=tps),
        out_shape=[tok_bf, tok_bf, jax.ShapeDtypeStruct((batch, d, seq), BF16), tok_bf,
                   jax.ShapeDtypeStruct((batch, N_PIECES * nh, seq), BF16)],
        grid=(n // tm,),
        in_specs=[tok] + [_const_spec(a.shape) for a in consts],
        out_specs=[tok, tok, pl.BlockSpec((1, d, tm), seq_map), tok,
                   pl.BlockSpec((1, N_PIECES * nh, tm), seq_map)],
        scratch_shapes=[pltpu.VMEM((nh, LANES), F32)],
        compiler_params=_params(("arbitrary",)),
        name="qkv",
    )(h, *consts)


def _attention_kernel(q_ref, k_ref, vt_ref, fp_ref, sg_ref, *rest, blk, nq, n_riders):
    rider_in, o_ref, rider_out, (kaug_ref, qaug_ref, vsum_ref) = _split_rider_refs(rest, n_riders)
    _convert_riders(rider_in, rider_out)
    hp = pl.program_id(1)
    nh = fp_ref.shape[2] // N_PIECES
    lane = lax.broadcasted_iota(jnp.int32, (1, LANES), 1)
    own = (lane < HEAD_DIM, lane >= HEAD_DIM)
    spare = (HEAD_DIM, 0)
    q_lanes = [(lane >= s) & (lane < s + N_PIECES) for s in spare]
    k_lanes = [(lane >= s + N_PIECES) & (lane < s + 2 * N_PIECES) for s in spare]

    er = lax.broadcasted_iota(jnp.int32, (N_PIECES * nh, LANES), 0)
    ec = lax.broadcasted_iota(jnp.int32, (N_PIECES * nh, LANES), 1)
    place = jnp.zeros((N_PIECES * nh, LANES), F32)
    for h in range(2):
        for j in range(N_PIECES):
            src = er == j * nh + 2 * hp + h
            place = (place + (src & (ec == spare[h] + j)).astype(F32)
                     - (src & (ec == spare[h] + N_PIECES + j)).astype(F32))
    g = _dot(fp_ref[0], place.astype(BF16))
    for h in range(2):
        k_fill = jnp.where(k_lanes[h], g, q_lanes[h].astype(F32)).astype(BF16)
        kaug_ref[h] = jnp.where(own[h], k_ref[...], k_fill)
        q_fill = jnp.where(q_lanes[h], g, k_lanes[h].astype(F32)).astype(BF16)
        qaug_ref[h] = jnp.where(own[h], q_ref[...], q_fill)

    first_rows = lax.broadcasted_iota(jnp.int32, (LANES, 1), 0) < HEAD_DIM
    vsum_ref[0] = jnp.where(first_rows, vt_ref[0], jnp.ones_like(vt_ref[0]))
    vsum_ref[1] = jnp.where(first_rows, jnp.ones_like(vt_ref[0]), vt_ref[0])
    key = lax.broadcasted_iota(jnp.int32, (blk, blk), 0)
    qry = lax.broadcasted_iota(jnp.int32, (blk, blk), 1)
    causal = qry >= key

    def qk(i):
        lo, hi = i * blk, (i + 1) * blk
        return [_dot_nt(kaug_ref[h, 0:hi, :], qaug_ref[h, lo:hi, :]) for h in range(2)]

    scores = qk(0)
    for i in range(nq):
        lo, hi = i * blk, (i + 1) * blk
        next_scores = qk(i + 1) if i + 1 < nq else None
        probs = []
        for s in scores:
            diag = jnp.where(causal, s[lo:hi], NEG_INF)
            s = diag if i == 0 else jnp.concatenate([s[0:lo], diag], axis=0)
            probs.append(jnp.exp2(s - jnp.max(s, axis=0, keepdims=True)).astype(BF16))
        outs = [_dot(vsum_ref[h, :, 0:hi], probs[h]) for h in range(2)]
        o0 = outs[0] / outs[0][HEAD_DIM:HEAD_DIM + 1, :]
        o1 = outs[1] / outs[1][0:1, :]
        o = jnp.where(first_rows, o0, o1).T
        o_ref[lo:hi, :] = (o * sg_ref[lo:hi, :].astype(F32)).astype(BF16)
        scores = next_scores


def _attention(q, k, vt, fp, sg, riders, *, batch, seq, blk):
    n, d = q.shape
    n_pairs = d // LANES
    pair = pl.BlockSpec((seq, LANES), lambda b, hp: (b, hp))
    slabs, slab_specs = _rider_slabs(riders, batch * n_pairs, lambda b, hp: b * n_pairs + hp)
    out = pl.pallas_call(
        functools.partial(_attention_kernel, blk=blk, nq=seq // blk, n_riders=len(riders)),
        out_shape=[jax.ShapeDtypeStruct((n, d), BF16)]
        + [jax.ShapeDtypeStruct(s.shape, BF16) for s in slabs],
        grid=(batch, n_pairs),
        in_specs=[
            pair, pair,
            pl.BlockSpec((1, LANES, seq), lambda b, hp: (b, hp, 0)),
            pl.BlockSpec((1, seq, fp.shape[2]), lambda b, hp: (b, 0, 0)),
            pair,
        ] + slab_specs,
        out_specs=[pair] + slab_specs,
        scratch_shapes=[pltpu.VMEM((2, seq, LANES), BF16)] * 2 + [pltpu.VMEM((2, LANES, seq), BF16)],
        compiler_params=_params(("parallel", "parallel")),
        name="attention",
    )(q, k, vt, fp, sg, *slabs)
    return out[0], _unslab(out[1:], riders)


def _pack_bf16_pair(lo, hi):
    ulo = lax.bitcast_convert_type(lo.astype(BF16).astype(F32), jnp.uint32)
    uhi = lax.bitcast_convert_type(hi.astype(BF16).astype(F32), jnp.uint32)
    return (ulo >> 16) | uhi


def _pack_row(x):
    half = x.shape[1] // 2
    w = _pack_bf16_pair(x[:, :half], x[:, half:])
    sw = half // SUBROWS
    return [w[:, c * sw:(c + 1) * sw] for c in range(SUBROWS)]


def _unpack_row(subrows):
    lo = [lax.bitcast_convert_type(p << 16, F32) for p in subrows]
    hi = [lax.bitcast_convert_type(p & jnp.uint32(0xFFFF0000), F32) for p in subrows]
    return jnp.concatenate(lo + hi, axis=1)


def _out_router_kernel(og_ref, h_ref, wo_ref, gm_ref, wrt_ref, h2_ref, xp_ref, meta_ref, cnt_ref,
                       carry_ref):
    i = pl.program_id(0)
    ne, tm = meta_ref.shape
    h2 = h_ref[...] + _dot(og_ref[...], wo_ref[...])
    h2_ref[...] = h2
    xn = _rms(h2, gm_ref[...])
    for c, sub in enumerate(_pack_row(xn)):
        xp_ref[c] = sub
    xh = xn.astype(BF16)
    xl = (xn - xh.astype(F32)).astype(BF16)
    wr = wrt_ref[...]
    wh = wr.astype(BF16)
    wl = (wr - wh.astype(F32)).astype(BF16)
    logits = _dot_nt(wh, xh) + (_dot_nt(wh, xl) + _dot_nt(wl, xh))
    row = lax.broadcasted_iota(jnp.int32, (ne, tm), 0).astype(F32)
    v1 = jnp.max(logits, axis=0, keepdims=True)
    i1 = jnp.min(jnp.where(logits == v1, row, ne), axis=0, keepdims=True)
    rest = jnp.where(row == i1, -jnp.inf, logits)
    v2 = jnp.max(rest, axis=0, keepdims=True)
    i2 = jnp.min(jnp.where(rest == v2, row, ne), axis=0, keepdims=True)
    e = jnp.exp(v2 - v1)
    w1 = 1.0 / (1.0 + e)
    w2 = e / (1.0 + e)
    sel1 = row == i1
    sel2 = row == i2
    oh = (sel1 | sel2).astype(F32)

    @pl.when(i == 0)
    def _():
        carry_ref[...] = jnp.zeros_like(carry_ref)

    base = carry_ref[:, 0:1]
    rank = base + (_lane_cumsum(oh) - oh)
    r1 = jnp.sum(jnp.where(sel1, rank, 0.0), axis=0, keepdims=True)
    r2 = jnp.sum(jnp.where(sel2, rank, 0.0), axis=0, keepdims=True)
    total = base + jnp.sum(oh, axis=1, keepdims=True)
    carry_ref[...] = jnp.broadcast_to(total, carry_ref.shape)
    cnt_ref[...] = jnp.broadcast_to(total, cnt_ref.shape)
    zeros = jnp.zeros_like(w1)
    meta_ref[...] = jnp.concatenate(
        [i1.astype(F32), i2.astype(F32), w1, w2, r1, r2, zeros, zeros], axis=0)


def _out_router(og, h, wo, gm, wrt, *, tm):
    n, d = h.shape
    ne = wrt.shape[0]
    sw = d // 2 // SUBROWS
    tok = pl.BlockSpec((tm, d), lambda i: (i, 0))
    return pl.pallas_call(
        _out_router_kernel,
        out_shape=[jax.ShapeDtypeStruct((n, d), F32), jax.ShapeDtypeStruct((SUBROWS, n, sw), jnp.uint32),
                   jax.ShapeDtypeStruct((ne, n), F32), jax.ShapeDtypeStruct((ne, LANES), F32)],
        grid=(n // tm,),
        in_specs=[tok, tok, _const_spec(wo.shape), _const_spec(gm.shape), _const_spec(wrt.shape)],
        out_specs=[tok, pl.BlockSpec((SUBROWS, tm, sw), lambda i: (0, i, 0)),
                   pl.BlockSpec((ne, tm), lambda i: (0, i)),
                   pl.BlockSpec((ne, LANES), lambda i: (0, 0))],
        scratch_shapes=[pltpu.VMEM((ne, LANES), F32)],
        compiler_params=_params(("arbitrary",)),
        name="out_router",
    )(og, h, wo, gm, wrt)


def _sc_mesh():
    return plsc.VectorSubcoreMesh(core_axis_name="core", subcore_axis_name="subcore")


def _sc_scatter(x, idx, out_rows):
    rows, w = x.shape
    n_idx = idx.shape[0]

    @pl.kernel(out_type=jax.ShapeDtypeStruct((out_rows, w), x.dtype), mesh=_sc_mesh(),
               scratch_types=[])
    def scatter_rows(x_hbm, i_hbm, o_hbm):
        def body(x_vmem, i_vmem):
            for s in range(n_idx):
                pltpu.sync_copy(x_vmem, o_hbm.at[i_vmem.at[s]])

        pltpu.emit_pipeline(
            body, grid=(rows // SC_WINDOW,),
            in_specs=[pl.BlockSpec((SC_WINDOW, w), lambda i: (i, 0)),
                      pl.BlockSpec((n_idx, SC_WINDOW), lambda i: (0, i))],
            out_specs=[],
            core_axis_name=("core", "subcore"),
            dimension_semantics=(pltpu.PARALLEL,),
        )(x_hbm, i_hbm)

    return scatter_rows(x, idx)


def _sc_gather(x, idx):
    n = idx.shape[1]
    w = x.shape[1]

    @pl.kernel(out_type=jax.ShapeDtypeStruct((n, w), x.dtype), mesh=_sc_mesh(), scratch_types=[])
    def gather_rows(x_hbm, i_hbm, o_hbm):
        def body(i_vmem, o_vmem):
            pltpu.sync_copy(x_hbm.at[i_vmem.at[0]], o_vmem)

        pltpu.emit_pipeline(
            body, grid=(n // SC_WINDOW,),
            in_specs=[pl.BlockSpec((1, SC_WINDOW), lambda i: (0, i))],
            out_specs=[pl.BlockSpec((SC_WINDOW, w), lambda i: (i, 0))],
            core_axis_name=("core", "subcore"),
            dimension_semantics=(pltpu.PARALLEL,),
        )(i_hbm, o_hbm)

    return gather_rows(x, idx)


def _experts_kernel(te_ref, rows_ref, x_ref, wa_ref, wb_ref, wo_ref, o_ref, xb_ref, hm_ref, *, nf):
    del te_ref
    i = pl.program_id(0)
    f = pl.program_id(1)
    tr = xb_ref.shape[0]
    tf = wa_ref.shape[2]
    n_valid = rows_ref[i]

    @pl.when((n_valid > 0) & (f == 0))
    def _():
        x = _unpack_row([x_ref[c] for c in range(SUBROWS)])
        live = lax.broadcasted_iota(jnp.int32, (tr, 1), 0) < n_valid
        xb_ref[...] = jnp.where(live, x, 0.0).astype(BF16)

    ts = tf // EXPERT_SUBSLABS
    for j in range(nf):
        @pl.when((n_valid > 0) & (f == j))
        def _(j=j):
            xb = xb_ref[...]
            for c in range(EXPERT_SUBSLABS):
                a = _dot(xb, wa_ref[0, :, c * ts:(c + 1) * ts])
                b = _dot(xb, wb_ref[0, :, c * ts:(c + 1) * ts])
                col = j * tf + c * ts
                hm_ref[:, col:col + ts] = (a * jax.nn.sigmoid(a) * b).astype(BF16)

    @pl.when((n_valid > 0) & (f == nf - 1))
    def _():
        for c, sub in enumerate(_pack_row(_dot(hm_ref[...], wo_ref[0]))):
            o_ref[c] = sub

    @pl.when((n_valid == 0) & (f == 0))
    def _():
        o_ref[...] = jnp.zeros_like(o_ref)


def _experts(tile_expert, tile_rows, xs, w_in, w_out, *, tr, tf):
    _, p, sw = xs.shape
    d = w_out.shape[2]
    de = w_out.shape[1]
    nf = de // tf
    rows = pl.BlockSpec((SUBROWS, tr, sw), lambda i, f, te, tv: (0, i, 0))
    return pl.pallas_call(
        functools.partial(_experts_kernel, nf=nf),
        out_shape=jax.ShapeDtypeStruct((SUBROWS, p, sw), jnp.uint32),
        grid_spec=pltpu.PrefetchScalarGridSpec(
            num_scalar_prefetch=2,
            grid=(p // tr, nf),
            in_specs=[
                rows,
                pl.BlockSpec((1, d, tf), lambda i, f, te, tv: (te[i], 0, f)),
                pl.BlockSpec((1, d, tf), lambda i, f, te, tv: (te[i], 0, nf + f)),
                pl.BlockSpec((1, de, d), lambda i, f, te, tv: (te[i], 0, 0),
                             pipeline_mode=pl.Buffered(1)),
            ],
            out_specs=rows,
            scratch_shapes=[pltpu.VMEM((tr, d), BF16), pltpu.VMEM((tr, de), BF16)],
        ),
        compiler_params=_params(("arbitrary", "arbitrary")),
        name="experts",
    )(tile_expert, tile_rows, xs, w_in, w_in, w_out)


def _combine_kernel(g_ref, h_ref, w_ref, o_ref):
    w = w_ref[...]
    y = [_unpack_row([g_ref[s, c] for c in range(SUBROWS)]) for s in range(TOP_K)]
    o_ref[...] = h_ref[...] + (w[:, 0:1] * y[0] + w[:, 1:2] * y[1])


def _combine(g, h, w, *, tc):
    n, d = h.shape
    sw = g.shape[3]
    tok = pl.BlockSpec((tc, d), lambda i: (i, 0))
    return pl.pallas_call(
        _combine_kernel,
        out_shape=jax.ShapeDtypeStruct((n, d), F32),
        grid=(n // tc,),
        in_specs=[pl.BlockSpec((TOP_K, SUBROWS, tc, sw), lambda i: (0, 0, i, 0)), tok,
                  pl.BlockSpec((tc, TOP_K), lambda i: (i, 0))],
        out_specs=tok,
        compiler_params=_params(("parallel",)),
        name="combine",
    )(g, h, w)


def _tiles(n, seq):
    def pick(limit, of):
        t = limit
        while of % t:
            t //= 2
        return t
    return dict(
        tm_a=pick(512, seq), tm_f=pick(512, n), tm_qkv=pick(1024, seq), blk=pick(256, seq),
        tm_o=pick(1024, n), tr=pick(1024, TOP_K * n), tc=pick(1024, n))


def kernel(x, a_norm_g, a_w_in, a_v_norm_g, a_w_spatial, a_b_spatial, a_w_out, f_norm_g, f_w_in, f_w_out, kv_norm_g, kv_w, kv_b_f, k_norm_g, b_norm_g, b_w_in, q_norm_g, b_w_out, m_norm_g, m_w_router, m_w_in, m_w_out):
    batch, seq, d = x.shape
    n = batch * seq
    nh = d // HEAD_DIM
    ne = m_w_router.shape[-1]
    assert a_w_in.shape[0] == 1 and b_w_in.shape[0] == 1 and f_w_in.shape[0] == 1 and m_w_in.shape[0] == 1
    assert seq % GMLP_CHUNK == 0 and d % LANES == 0 and (SUBROWS * n) % SC_WINDOW == 0
    t = _tiles(n, seq)
    row = lambda g: g.reshape(1, -1)

    h = x.reshape(n, d)
    h, (f_in, f_out) = _mixer_a(
        h, row(a_norm_g[0]), a_w_in[0].astype(BF16), row(a_v_norm_g[0]), a_w_spatial[0],
        a_b_spatial[0].T, a_w_out[0].astype(BF16), (f_w_in[0], f_w_out[0]), tm=t["tm_a"])
    h, (kv, b_in, b_out) = _swiglu(h, row(f_norm_g[0]), f_in, f_out,
                                   (kv_w, b_w_in[0], b_w_out[0]), tm=t["tm_f"])

    head = jnp.arange(d, dtype=jnp.int32) // HEAD_DIM
    hsum = (head[:, None] == head[None, :]).astype(BF16)
    q, k, vt, sg, fp = _qkv(
        h, row(kv_norm_g), row(b_norm_g[0]), kv[:, :d], kv[:, d:2 * d].T, kv[:, 2 * d:].T,
        kv_b_f.reshape(nh, 1), row(jnp.tile(k_norm_g, nh)), row(jnp.tile(q_norm_g[0], nh)), b_in, hsum,
        tm=t["tm_qkv"], batch=batch, seq=seq)
    og, (moe_w_in, moe_w_out) = _attention(q, k, vt, jnp.swapaxes(fp, 1, 2), sg,
                                           (m_w_in[0], m_w_out[0]), batch=batch, seq=seq, blk=t["blk"])

    h2, xp, meta, cnt = _out_router(og, h, b_out, row(m_norm_g[0]),
                                    m_w_router[0].T, tm=t["tm_o"])

    tr = t["tr"]
    n_tiles = TOP_K * n // tr + ne
    p = n_tiles * tr
    counts = cnt[:, 0].astype(jnp.int32)
    tiles_per_expert = (counts + tr - 1) // tr
    tile_end = jnp.cumsum(tiles_per_expert)
    tile_start = tile_end - tiles_per_expert
    expert_ids = jnp.arange(ne, dtype=jnp.int32)
    idx = meta[0:TOP_K].astype(jnp.int32)
    start_of = jnp.sum(jnp.where(idx[:, :, None] == expert_ids, tile_start * tr, 0), axis=-1)
    dest = start_of + meta[4:4 + TOP_K].astype(jnp.int32)
    tile_ids = jnp.arange(n_tiles, dtype=jnp.int32)
    tile_expert = jnp.minimum(
        jnp.sum((tile_ids[:, None] >= tile_end[None, :]).astype(jnp.int32), axis=1), ne - 1)
    mine = tile_expert[:, None] == expert_ids[None, :]
    tile_rows = jnp.clip(
        jnp.sum(jnp.where(mine, counts - (tile_ids[:, None] - tile_start) * tr, 0), axis=1), 0, tr)
    tile_rows = jnp.where(tile_ids < tile_end[-1], tile_rows, 0).astype(jnp.int32)

    sub = (jnp.arange(SUBROWS, dtype=jnp.int32) * p)[None, :, None]
    sub_dest = sub + dest[:, None, :]
    sw = xp.shape[2]
    xs = _sc_scatter(xp.reshape(SUBROWS * n, sw), sub_dest.reshape(TOP_K, SUBROWS * n), SUBROWS * p)
    eo = _experts(tile_expert, tile_rows, xs.reshape(SUBROWS, p, sw), moe_w_in, moe_w_out,
                  tr=tr, tf=m_w_out.shape[2] // EXPERT_STEPS)
    g = _sc_gather(eo.reshape(SUBROWS * p, sw), sub_dest.reshape(1, TOP_K * SUBROWS * n))
    out = _combine(g.reshape(TOP_K, SUBROWS, n, sw), h2, meta[2:2 + TOP_K].T, tc=t["tc"])
    return out.reshape(batch, seq, d)
```

```python
import functools

import jax
import jax.numpy as jnp
from jax import lax
from jax.experimental import pallas as pl
from jax.experimental.pallas import tpu as pltpu
from jax.experimental.pallas import tpu_sc as plsc

RMS_EPS = 1e-6
NEG_INF = -1e30
LOG2E = 1.4426950408889634
N_PIECES = 3
SC_WINDOW = 128
SUBROWS = 2
EXPERT_STEPS = 2
EXPERT_SUBSLABS = 2
GMLP_CHUNK = 128
CAUSAL_CHUNK = 64
A_GROUPS = 8
HEAD_DIM = 64
LANES = 128
TOP_K = 2
VMEM_LIMIT = 56 * 1024 * 1024

BF16 = jnp.bfloat16
F32 = jnp.float32


def _dot(a, b, **kw):
    return jnp.dot(a, b, preferred_element_type=F32, **kw)


def _dot_nt(a, b, **kw):
    return lax.dot_general(a, b, (((1,), (1,)), ((), ())), preferred_element_type=F32, **kw)


def _rms(x, g):
    return x * lax.rsqrt(jnp.mean(x * x, axis=-1, keepdims=True) + RMS_EPS) * g


def _gelu(x):
    return 0.5 * x * (1.0 + lax.erf(x * (2.0 ** -0.5)))


def _const_spec(shape):
    nd = len(shape)
    return pl.BlockSpec(shape, lambda *_: (0,) * nd, pipeline_mode=pl.Buffered(1))


def _params(sem):
    return pltpu.CompilerParams(dimension_semantics=sem, vmem_limit_bytes=VMEM_LIMIT)


def _mixer_a_kernel(x_ref, g_ref, win_ref, gv_ref, ws_ref, bs_ref, wout_ref, o_ref, z_ref):
    tm = x_ref.shape[0]
    half = wout_ref.shape[0]
    gd = half // A_GROUPS
    x = x_ref[...]
    xb = _rms(x, g_ref[...]).astype(BF16)
    v = _gelu(_dot(xb, win_ref[:, half:]))
    v = (_rms(v, gv_ref[...])).astype(BF16)
    u = _gelu(_dot(xb, win_ref[:, :half]))
    row = lax.broadcasted_iota(jnp.int32, (GMLP_CHUNK, GMLP_CHUNK), 0)
    col = lax.broadcasted_iota(jnp.int32, (GMLP_CHUNK, GMLP_CHUNK), 1)
    keep = (col // CAUSAL_CHUNK) <= (row // CAUSAL_CHUNK)
    bs = bs_ref[...]
    for g in range(A_GROUPS):
        wg = jnp.where(keep, ws_ref[g], 0.0).astype(BF16)
        bg = bs[:, g:g + 1]
        for c in range(tm // GMLP_CHUNK):
            rs = slice(c * GMLP_CHUNK, (c + 1) * GMLP_CHUNK)
            cs = slice(g * gd, (g + 1) * gd)
            sv = _dot(wg, v[rs, cs]) + bg
            z_ref[rs, cs] = (u[rs, cs] * sv).astype(BF16)
    o_ref[...] = x + _dot(z_ref[...], wout_ref[...])


def _mixer_a(h, g, w_in, gv, ws, bs_t, w_out, *, tm):
    n, d = h.shape
    half = w_out.shape[0]
    return pl.pallas_call(
        _mixer_a_kernel,
        out_shape=jax.ShapeDtypeStruct((n, d), F32),
        grid=(n // tm,),
        in_specs=[
            pl.BlockSpec((tm, d), lambda i: (i, 0)),
            _const_spec(g.shape), _const_spec(w_in.shape), _const_spec(gv.shape),
            _const_spec(ws.shape), _const_spec(bs_t.shape), _const_spec(w_out.shape),
        ],
        out_specs=pl.BlockSpec((tm, d), lambda i: (i, 0)),
        scratch_shapes=[pltpu.VMEM((tm, half), BF16)],
        compiler_params=_params(("parallel",)),
        name="mixer_a",
    )(h, g, w_in, gv, ws, bs_t, w_out)


def _swiglu_kernel(x_ref, g_ref, win_ref, wout_ref, o_ref):
    f = wout_ref.shape[0]
    x = x_ref[...]
    xb = _rms(x, g_ref[...]).astype(BF16)
    a = _dot(xb, win_ref[:, :f])
    b = _dot(xb, win_ref[:, f:])
    hm = (a * jax.nn.sigmoid(a) * b).astype(BF16)
    o_ref[...] = x + _dot(hm, wout_ref[...])


def _swiglu(h, g, w_in, w_out, *, tm):
    n, d = h.shape
    return pl.pallas_call(
        _swiglu_kernel,
        out_shape=jax.ShapeDtypeStruct((n, d), F32),
        grid=(n // tm,),
        in_specs=[
            pl.BlockSpec((tm, d), lambda i: (i, 0)),
            _const_spec(g.shape), _const_spec(w_in.shape), _const_spec(w_out.shape),
        ],
        out_specs=pl.BlockSpec((tm, d), lambda i: (i, 0)),
        compiler_params=_params(("parallel",)),
        name="swiglu",
    )(h, g, w_in, w_out)


def _lane_cumsum(x):
    n = x.shape[-1]
    lane = lax.broadcasted_iota(jnp.int32, x.shape, x.ndim - 1)
    sh = 1
    while sh < n:
        x = x + jnp.where(lane >= sh, pltpu.roll(x, sh, axis=x.ndim - 1), 0.0)
        sh *= 2
    return x


def _split_bf16(x):
    pieces = []
    for _ in range(N_PIECES):
        p = x.astype(BF16)
        pieces.append(p)
        x = x - p.astype(F32)
    return pieces


def _qkv_kernel(x_ref, gkv_ref, gb_ref, wk_ref, wvt_ref, wft_ref, bf_ref, wqg_ref,
                q_ref, k_ref, vt_ref, sg_ref, fp_ref, carry_ref, *, tiles_per_seq):
    d = x_ref.shape[1]
    i = pl.program_id(0)

    @pl.when(i == 0)
    def _():
        carry_ref[...] = jnp.zeros_like(carry_ref)

    x = x_ref[...]
    y = x * lax.rsqrt(jnp.mean(x * x, axis=-1, keepdims=True) + RMS_EPS)
    skv = (y * gkv_ref[...]).astype(BF16)
    sb = (y * gb_ref[...]).astype(BF16)

    f = _dot_nt(wft_ref[...], skv) + bf_ref[...]
    logf = jax.nn.log_sigmoid(f)
    carry = jnp.where(i % tiles_per_seq == 0, 0.0, carry_ref[:, 0:1])
    cum = _lane_cumsum(logf) + carry
    carry_ref[...] = jnp.broadcast_to(cum[:, -1:], carry_ref.shape)
    fp_ref[0] = jnp.concatenate(_split_bf16(cum * LOG2E), axis=0)

    k_ref[...] = _dot(skv, wk_ref[...]).astype(BF16)
    vt_ref[0] = _dot_nt(wvt_ref[...], skv).astype(BF16)
    q_ref[...] = _dot(sb, wqg_ref[:, :d]).astype(BF16)
    sg_ref[...] = jax.nn.sigmoid(_dot(sb, wqg_ref[:, d:])).astype(BF16)


def _qkv(h, gkv, gb, wk, wvt, wft, bf, wqg, *, tm, batch, seq):
    n, d = h.shape
    nh = wft.shape[0]
    tps = seq // tm
    tok = pl.BlockSpec((tm, d), lambda i: (i, 0))
    tok_bf = jax.ShapeDtypeStruct((n, d), BF16)
    seq_map = lambda i: (i // tps, 0, i % tps)
    consts = (gkv, gb, wk, wvt, wft, bf, wqg)
    return pl.pallas_call(
        functools.partial(_qkv_kernel, tiles_per_seq=tps),
        out_shape=[tok_bf, tok_bf, jax.ShapeDtypeStruct((batch, d, seq), BF16), tok_bf,
                   jax.ShapeDtypeStruct((batch, N_PIECES * nh, seq), BF16)],
        grid=(n // tm,),
        in_specs=[tok] + [_const_spec(a.shape) for a in consts],
        out_specs=[tok, tok, pl.BlockSpec((1, d, tm), seq_map), tok,
                   pl.BlockSpec((1, N_PIECES * nh, tm), seq_map)],
        scratch_shapes=[pltpu.VMEM((nh, LANES), F32)],
        compiler_params=_params(("arbitrary",)),
        name="qkv",
    )(h, *consts)


def _attention_kernel(q_ref, k_ref, vt_ref, fp_ref, sg_ref, gq_ref, gk_ref, *rest,
                      blk, nq, n_riders):
    rider_in, (o_ref, *rider_out) = rest[:n_riders], rest[n_riders:2 * n_riders + 1]
    kaug_ref, qaug_ref, vsum_ref = rest[2 * n_riders + 1:]
    for w_ref, c_ref in zip(rider_in, rider_out):
        c_ref[...] = w_ref[...].astype(BF16)
    hp = pl.program_id(1)
    nh = fp_ref.shape[2] // N_PIECES
    lane = lax.broadcasted_iota(jnp.int32, (1, LANES), 1)
    own = (lane < HEAD_DIM, lane >= HEAD_DIM)
    spare = (HEAD_DIM, 0)
    q_lanes = [(lane >= s) & (lane < s + N_PIECES) for s in spare]
    k_lanes = [(lane >= s + N_PIECES) & (lane < s + 2 * N_PIECES) for s in spare]

    er = lax.broadcasted_iota(jnp.int32, (N_PIECES * nh, LANES), 0)
    ec = lax.broadcasted_iota(jnp.int32, (N_PIECES * nh, LANES), 1)
    place = jnp.zeros((N_PIECES * nh, LANES), F32)
    for h in range(2):
        for j in range(N_PIECES):
            src = er == j * nh + 2 * hp + h
            place = (place + (src & (ec == spare[h] + j)).astype(F32)
                     - (src & (ec == spare[h] + N_PIECES + j)).astype(F32))
    g = _dot(fp_ref[0], place.astype(BF16))
    hr = lax.broadcasted_iota(jnp.int32, (LANES, LANES), 0) // HEAD_DIM
    hc = lax.broadcasted_iota(jnp.int32, (LANES, LANES), 1) // HEAD_DIM
    same_head = (hr == hc).astype(BF16)

    def head_norm(t, gain):
        t = t.astype(F32)
        ssq = _dot((t * t).astype(BF16), same_head)
        return (t * lax.rsqrt(ssq * (1.0 / HEAD_DIM) + RMS_EPS) * gain).astype(BF16)

    kn = head_norm(k_ref[...], gk_ref[...])
    qn = head_norm(q_ref[...], gq_ref[...] * (LOG2E * HEAD_DIM ** -0.5))
    for h in range(2):
        k_fill = jnp.where(k_lanes[h], g, q_lanes[h].astype(F32)).astype(BF16)
        kaug_ref[h] = jnp.where(own[h], kn, k_fill)
        q_fill = jnp.where(q_lanes[h], g, k_lanes[h].astype(F32)).astype(BF16)
        qaug_ref[h] = jnp.where(own[h], qn, q_fill)

    first_rows = lax.broadcasted_iota(jnp.int32, (LANES, 1), 0) < HEAD_DIM
    vsum_ref[0] = jnp.where(first_rows, vt_ref[0], jnp.ones_like(vt_ref[0]))
    vsum_ref[1] = jnp.where(first_rows, jnp.ones_like(vt_ref[0]), vt_ref[0])
    key = lax.broadcasted_iota(jnp.int32, (blk, blk), 0)
    qry = lax.broadcasted_iota(jnp.int32, (blk, blk), 1)
    causal = qry >= key

    def qk(i):
        lo, hi = i * blk, (i + 1) * blk
        return [_dot_nt(kaug_ref[h, 0:hi, :], qaug_ref[h, lo:hi, :]) for h in range(2)]

    scores = qk(0)
    for i in range(nq):
        lo, hi = i * blk, (i + 1) * blk
        next_scores = qk(i + 1) if i + 1 < nq else None
        probs = []
        for s in scores:
            diag = jnp.where(causal, s[lo:hi], NEG_INF)
            s = diag if i == 0 else jnp.concatenate([s[0:lo], diag], axis=0)
            probs.append(jnp.exp2(s - jnp.max(s, axis=0, keepdims=True)).astype(BF16))
        outs = [_dot(vsum_ref[h, :, 0:hi], probs[h]) for h in range(2)]
        o0 = outs[0] / outs[0][HEAD_DIM:HEAD_DIM + 1, :]
        o1 = outs[1] / outs[1][0:1, :]
        o = jnp.where(first_rows, o0, o1).T
        o_ref[lo:hi, :] = (o * sg_ref[lo:hi, :].astype(F32)).astype(BF16)
        scores = next_scores


def _attention(q, k, vt, fp, sg, gq, gk, riders, *, batch, seq, blk):
    n, d = q.shape
    n_pairs = d // LANES
    steps = batch * n_pairs
    pair = pl.BlockSpec((seq, LANES), lambda b, hp: (b, hp))
    slabs = [w.reshape(steps, -1, w.shape[-1]) for w in riders]
    slab_specs = [pl.BlockSpec((1,) + s.shape[1:], lambda b, hp: (b * n_pairs + hp, 0, 0))
                  for s in slabs]
    out = pl.pallas_call(
        functools.partial(_attention_kernel, blk=blk, nq=seq // blk, n_riders=len(riders)),
        out_shape=[jax.ShapeDtypeStruct((n, d), BF16)]
        + [jax.ShapeDtypeStruct(s.shape, BF16) for s in slabs],
        grid=(batch, n_pairs),
        in_specs=[
            pair, pair,
            pl.BlockSpec((1, LANES, seq), lambda b, hp: (b, hp, 0)),
            pl.BlockSpec((1, seq, fp.shape[2]), lambda b, hp: (b, 0, 0)),
            pair, _const_spec(gq.shape), _const_spec(gk.shape),
        ] + slab_specs,
        out_specs=[pair] + slab_specs,
        scratch_shapes=[pltpu.VMEM((2, seq, LANES), BF16)] * 2 + [pltpu.VMEM((2, LANES, seq), BF16)],
        compiler_params=_params(("parallel", "parallel")),
        name="attention",
    )(q, k, vt, fp, sg, gq, gk, *slabs)
    return out[0], [c.reshape(w.shape) for c, w in zip(out[1:], riders)]


def _pack_bf16_pair(lo, hi):
    ulo = lax.bitcast_convert_type(lo.astype(BF16).astype(F32), jnp.uint32)
    uhi = lax.bitcast_convert_type(hi.astype(BF16).astype(F32), jnp.uint32)
    return (ulo >> 16) | uhi


def _pack_row(x):
    half = x.shape[1] // 2
    w = _pack_bf16_pair(x[:, :half], x[:, half:])
    sw = half // SUBROWS
    return [w[:, c * sw:(c + 1) * sw] for c in range(SUBROWS)]


def _unpack_row(subrows):
    lo = [lax.bitcast_convert_type(p << 16, F32) for p in subrows]
    hi = [lax.bitcast_convert_type(p & jnp.uint32(0xFFFF0000), F32) for p in subrows]
    return jnp.concatenate(lo + hi, axis=1)


def _out_router_kernel(og_ref, h_ref, wo_ref, gm_ref, wrt_ref, h2_ref, xp_ref, meta_ref, cnt_ref,
                       carry_ref):
    i = pl.program_id(0)
    ne, tm = meta_ref.shape
    h2 = h_ref[...] + _dot(og_ref[...], wo_ref[...])
    h2_ref[...] = h2
    xn = _rms(h2, gm_ref[...])
    for c, sub in enumerate(_pack_row(xn)):
        xp_ref[c] = sub
    xh = xn.astype(BF16)
    xl = (xn - xh.astype(F32)).astype(BF16)
    wr = wrt_ref[...]
    wh = wr.astype(BF16)
    wl = (wr - wh.astype(F32)).astype(BF16)
    logits = _dot_nt(wh, xh) + (_dot_nt(wh, xl) + _dot_nt(wl, xh))
    row = lax.broadcasted_iota(jnp.int32, (ne, tm), 0).astype(F32)
    v1 = jnp.max(logits, axis=0, keepdims=True)
    i1 = jnp.min(jnp.where(logits == v1, row, ne), axis=0, keepdims=True)
    rest = jnp.where(row == i1, -jnp.inf, logits)
    v2 = jnp.max(rest, axis=0, keepdims=True)
    i2 = jnp.min(jnp.where(rest == v2, row, ne), axis=0, keepdims=True)
    e = jnp.exp(v2 - v1)
    w1 = 1.0 / (1.0 + e)
    w2 = e / (1.0 + e)
    sel1 = row == i1
    sel2 = row == i2
    oh = (sel1 | sel2).astype(F32)

    @pl.when(i == 0)
    def _():
        carry_ref[...] = jnp.zeros_like(carry_ref)

    base = carry_ref[:, 0:1]
    rank = base + (_lane_cumsum(oh) - oh)
    r1 = jnp.sum(jnp.where(sel1, rank, 0.0), axis=0, keepdims=True)
    r2 = jnp.sum(jnp.where(sel2, rank, 0.0), axis=0, keepdims=True)
    total = base + jnp.sum(oh, axis=1, keepdims=True)
    carry_ref[...] = jnp.broadcast_to(total, carry_ref.shape)
    cnt_ref[...] = jnp.broadcast_to(total, cnt_ref.shape)
    zeros = jnp.zeros_like(w1)
    meta_ref[...] = jnp.concatenate(
        [i1.astype(F32), i2.astype(F32), w1, w2, r1, r2, zeros, zeros], axis=0)


def _out_router(og, h, wo, gm, wrt, *, tm):
    n, d = h.shape
    ne = wrt.shape[0]
    sw = d // 2 // SUBROWS
    tok = pl.BlockSpec((tm, d), lambda i: (i, 0))
    return pl.pallas_call(
        _out_router_kernel,
        out_shape=[jax.ShapeDtypeStruct((n, d), F32), jax.ShapeDtypeStruct((SUBROWS, n, sw), jnp.uint32),
                   jax.ShapeDtypeStruct((ne, n), F32), jax.ShapeDtypeStruct((ne, LANES), F32)],
        grid=(n // tm,),
        in_specs=[tok, tok, _const_spec(wo.shape), _const_spec(gm.shape), _const_spec(wrt.shape)],
        out_specs=[tok, pl.BlockSpec((SUBROWS, tm, sw), lambda i: (0, i, 0)),
                   pl.BlockSpec((ne, tm), lambda i: (0, i)),
                   pl.BlockSpec((ne, LANES), lambda i: (0, 0))],
        scratch_shapes=[pltpu.VMEM((ne, LANES), F32)],
        compiler_params=_params(("arbitrary",)),
        name="out_router",
    )(og, h, wo, gm, wrt)


def _sc_mesh():
    return plsc.VectorSubcoreMesh(core_axis_name="core", subcore_axis_name="subcore")


def _sc_scatter(x, idx, out_rows):
    rows, w = x.shape
    n_idx = idx.shape[0]

    @pl.kernel(out_type=jax.ShapeDtypeStruct((out_rows, w), x.dtype), mesh=_sc_mesh(),
               scratch_types=[])
    def scatter_rows(x_hbm, i_hbm, o_hbm):
        def body(x_vmem, i_vmem):
            for s in range(n_idx):
                pltpu.sync_copy(x_vmem, o_hbm.at[i_vmem.at[s]])

        pltpu.emit_pipeline(
            body, grid=(rows // SC_WINDOW,),
            in_specs=[pl.BlockSpec((SC_WINDOW, w), lambda i: (i, 0)),
                      pl.BlockSpec((n_idx, SC_WINDOW), lambda i: (0, i))],
            out_specs=[],
            core_axis_name=("core", "subcore"),
            dimension_semantics=(pltpu.PARALLEL,),
        )(x_hbm, i_hbm)

    return scatter_rows(x, idx)


def _sc_gather(x, idx):
    n = idx.shape[1]
    w = x.shape[1]

    @pl.kernel(out_type=jax.ShapeDtypeStruct((n, w), x.dtype), mesh=_sc_mesh(), scratch_types=[])
    def gather_rows(x_hbm, i_hbm, o_hbm):
        def body(i_vmem, o_vmem):
            pltpu.sync_copy(x_hbm.at[i_vmem.at[0]], o_vmem)

        pltpu.emit_pipeline(
            body, grid=(n // SC_WINDOW,),
            in_specs=[pl.BlockSpec((1, SC_WINDOW), lambda i: (0, i))],
            out_specs=[pl.BlockSpec((SC_WINDOW, w), lambda i: (i, 0))],
            core_axis_name=("core", "subcore"),
            dimension_semantics=(pltpu.PARALLEL,),
        )(i_hbm, o_hbm)

    return gather_rows(x, idx)


def _experts_kernel(te_ref, rows_ref, x_ref, wa_ref, wb_ref, wo_ref, o_ref, xb_ref, hm_ref, *, nf):
    del te_ref
    i = pl.program_id(0)
    f = pl.program_id(1)
    tr = xb_ref.shape[0]
    tf = wa_ref.shape[2]
    n_valid = rows_ref[i]

    @pl.when((n_valid > 0) & (f == 0))
    def _():
        x = _unpack_row([x_ref[c] for c in range(SUBROWS)])
        live = lax.broadcasted_iota(jnp.int32, (tr, 1), 0) < n_valid
        xb_ref[...] = jnp.where(live, x, 0.0).astype(BF16)

    ts = tf // EXPERT_SUBSLABS
    for j in range(nf):
        @pl.when((n_valid > 0) & (f == j))
        def _(j=j):
            xb = xb_ref[...]
            for c in range(EXPERT_SUBSLABS):
                a = _dot(xb, wa_ref[0, :, c * ts:(c + 1) * ts])
                b = _dot(xb, wb_ref[0, :, c * ts:(c + 1) * ts])
                col = j * tf + c * ts
                hm_ref[:, col:col + ts] = (a * jax.nn.sigmoid(a) * b).astype(BF16)

    @pl.when((n_valid > 0) & (f == nf - 1))
    def _():
        for c, sub in enumerate(_pack_row(_dot(hm_ref[...], wo_ref[0]))):
            o_ref[c] = sub

    @pl.when((n_valid == 0) & (f == 0))
    def _():
        o_ref[...] = jnp.zeros_like(o_ref)


def _experts(tile_expert, tile_rows, xs, w_in, w_out, *, tr, tf):
    _, p, sw = xs.shape
    d = w_out.shape[2]
    de = w_out.shape[1]
    nf = de // tf
    rows = pl.BlockSpec((SUBROWS, tr, sw), lambda i, f, te, tv: (0, i, 0))
    return pl.pallas_call(
        functools.partial(_experts_kernel, nf=nf),
        out_shape=jax.ShapeDtypeStruct((SUBROWS, p, sw), jnp.uint32),
        grid_spec=pltpu.PrefetchScalarGridSpec(
            num_scalar_prefetch=2,
            grid=(p // tr, nf),
            in_specs=[
                rows,
                pl.BlockSpec((1, d, tf), lambda i, f, te, tv: (te[i], 0, f)),
                pl.BlockSpec((1, d, tf), lambda i, f, te, tv: (te[i], 0, nf + f)),
                pl.BlockSpec((1, de, d), lambda i, f, te, tv: (te[i], 0, 0),
                             pipeline_mode=pl.Buffered(1)),
            ],
            out_specs=rows,
            scratch_shapes=[pltpu.VMEM((tr, d), BF16), pltpu.VMEM((tr, de), BF16)],
        ),
        compiler_params=_params(("arbitrary", "arbitrary")),
        name="experts",
    )(tile_expert, tile_rows, xs, w_in, w_in, w_out)


def _combine_kernel(g_ref, h_ref, w_ref, o_ref):
    w = w_ref[...]
    y = [_unpack_row([g_ref[s, c] for c in range(SUBROWS)]) for s in range(TOP_K)]
    o_ref[...] = h_ref[...] + (w[:, 0:1] * y[0] + w[:, 1:2] * y[1])


def _combine(g, h, w, *, tc):
    n, d = h.shape
    sw = g.shape[3]
    tok = pl.BlockSpec((tc, d), lambda i: (i, 0))
    return pl.pallas_call(
        _combine_kernel,
        out_shape=jax.ShapeDtypeStruct((n, d), F32),
        grid=(n // tc,),
        in_specs=[pl.BlockSpec((TOP_K, SUBROWS, tc, sw), lambda i: (0, 0, i, 0)), tok,
                  pl.BlockSpec((tc, TOP_K), lambda i: (i, 0))],
        out_specs=tok,
        compiler_params=_params(("parallel",)),
        name="combine",
    )(g, h, w)


def _tiles(n, seq):
    def pick(limit, of):
        t = limit
        while of % t:
            t //= 2
        return t
    return dict(
        tm_a=pick(512, seq), tm_f=pick(512, n), tm_qkv=pick(1024, seq), blk=pick(256, seq),
        tm_o=pick(1024, n), tr=pick(1024, TOP_K * n), tc=pick(1024, n))


def kernel(x, a_norm_g, a_w_in, a_v_norm_g, a_w_spatial, a_b_spatial, a_w_out, f_norm_g, f_w_in, f_w_out, kv_norm_g, kv_w, kv_b_f, k_norm_g, b_norm_g, b_w_in, q_norm_g, b_w_out, m_norm_g, m_w_router, m_w_in, m_w_out):
    batch, seq, d = x.shape
    n = batch * seq
    nh = d // HEAD_DIM
    ne = m_w_router.shape[-1]
    assert a_w_in.shape[0] == 1 and b_w_in.shape[0] == 1 and f_w_in.shape[0] == 1 and m_w_in.shape[0] == 1
    assert seq % GMLP_CHUNK == 0 and d % LANES == 0 and (SUBROWS * n) % SC_WINDOW == 0
    t = _tiles(n, seq)
    row = lambda g: g.reshape(1, -1)

    h = x.reshape(n, d)
    h = _mixer_a(h, row(a_norm_g[0]), a_w_in[0].astype(BF16), row(a_v_norm_g[0]), a_w_spatial[0],
                 a_b_spatial[0].T, a_w_out[0].astype(BF16), tm=t["tm_a"])
    h = _swiglu(h, row(f_norm_g[0]), f_w_in[0].astype(BF16), f_w_out[0].astype(BF16), tm=t["tm_f"])

    q, k, vt, sg, fp = _qkv(
        h, row(kv_norm_g), row(b_norm_g[0]), kv_w[:, :d].astype(BF16),
        kv_w[:, d:2 * d].T.astype(BF16), kv_w[:, 2 * d:].T.astype(BF16), kv_b_f.reshape(nh, 1),
        b_w_in[0].astype(BF16), tm=t["tm_qkv"], batch=batch, seq=seq)
    pair_gain = lambda g: row(jnp.tile(g, LANES // HEAD_DIM))
    og, (moe_w_in, moe_w_out) = _attention(
        q, k, vt, jnp.swapaxes(fp, 1, 2), sg, pair_gain(q_norm_g[0]), pair_gain(k_norm_g),
        (m_w_in[0], m_w_out[0]), batch=batch, seq=seq, blk=t["blk"])

    h2, xp, meta, cnt = _out_router(og, h, b_w_out[0].astype(BF16), row(m_norm_g[0]),
                                    m_w_router[0].T, tm=t["tm_o"])

    tr = t["tr"]
    n_tiles = TOP_K * n // tr + ne
    p = n_tiles * tr
    counts = cnt[:, 0].astype(jnp.int32)
    tiles_per_expert = (counts + tr - 1) // tr
    tile_end = jnp.cumsum(tiles_per_expert)
    tile_start = tile_end - tiles_per_expert
    expert_ids = jnp.arange(ne, dtype=jnp.int32)
    idx = meta[0:TOP_K].astype(jnp.int32)
    start_of = jnp.sum(jnp.where(idx[:, :, None] == expert_ids, tile_start * tr, 0), axis=-1)
    dest = start_of + meta[4:4 + TOP_K].astype(jnp.int32)
    tile_ids = jnp.arange(n_tiles, dtype=jnp.int32)
    tile_expert = jnp.minimum(
        jnp.sum((tile_ids[:, None] >= tile_end[None, :]).astype(jnp.int32), axis=1), ne - 1)
    mine = tile_expert[:, None] == expert_ids[None, :]
    tile_rows = jnp.clip(
        jnp.sum(jnp.where(mine, counts - (tile_ids[:, None] - tile_start) * tr, 0), axis=1), 0, tr)
    tile_rows = jnp.where(tile_ids < tile_end[-1], tile_rows, 0).astype(jnp.int32)

    sub = (jnp.arange(SUBROWS, dtype=jnp.int32) * p)[None, :, None]
    sub_dest = sub + dest[:, None, :]
    sw = xp.shape[2]
    xs = _sc_scatter(xp.reshape(SUBROWS * n, sw), sub_dest.reshape(TOP_K, SUBROWS * n), SUBROWS * p)
    eo = _experts(tile_expert, tile_rows, xs.reshape(SUBROWS, p, sw), moe_w_in, moe_w_out,
                  tr=tr, tf=m_w_out.shape[2] // EXPERT_STEPS)
    g = _sc_gather(eo.reshape(SUBROWS * p, sw), sub_dest.reshape(1, TOP_K * SUBROWS * n))
    out = _combine(g.reshape(TOP_K, SUBROWS, n, sw), h2, meta[2:2 + TOP_K].T, tc=t["tc"])
    return out.reshape(batch, seq, d)
```

```python
import functools

import jax
import jax.numpy as jnp
from jax import lax
from jax.experimental import pallas as pl
from jax.experimental.pallas import tpu as pltpu
from jax.experimental.pallas import tpu_sc as plsc

RMS_EPS = 1e-6
NEG_INF = -1e30
LOG2E = 1.4426950408889634
N_PIECES = 3
SC_WINDOW = 128
SUBROWS = 2
EXPERT_STEPS = 2
EXPERT_SUBSLABS = 2
GMLP_CHUNK = 128
CAUSAL_CHUNK = 64
A_GROUPS = 8
HEAD_DIM = 64
LANES = 128
TOP_K = 2
VMEM_LIMIT = 56 * 1024 * 1024

BF16 = jnp.bfloat16
F32 = jnp.float32


def _dot(a, b, **kw):
    return jnp.dot(a, b, preferred_element_type=F32, **kw)


def _dot_nt(a, b, **kw):
    return lax.dot_general(a, b, (((1,), (1,)), ((), ())), preferred_element_type=F32, **kw)


def _rms(x, g):
    return x * lax.rsqrt(jnp.mean(x * x, axis=-1, keepdims=True) + RMS_EPS) * g


def _gelu(x):
    return 0.5 * x * (1.0 + lax.erf(x * (2.0 ** -0.5)))


def _const_spec(shape):
    nd = len(shape)
    return pl.BlockSpec(shape, lambda *_: (0,) * nd, pipeline_mode=pl.Buffered(1))


def _params(sem):
    return pltpu.CompilerParams(dimension_semantics=sem, vmem_limit_bytes=VMEM_LIMIT)


def _mixer_a_kernel(x_ref, g_ref, win_ref, gv_ref, ws_ref, bs_ref, wout_ref, o_ref, z_ref):
    tm = x_ref.shape[0]
    half = wout_ref.shape[0]
    gd = half // A_GROUPS
    x = x_ref[...]
    xb = _rms(x, g_ref[...]).astype(BF16)
    v = _gelu(_dot(xb, win_ref[:, half:]))
    v = (_rms(v, gv_ref[...])).astype(BF16)
    u = _gelu(_dot(xb, win_ref[:, :half]))
    row = lax.broadcasted_iota(jnp.int32, (GMLP_CHUNK, GMLP_CHUNK), 0)
    col = lax.broadcasted_iota(jnp.int32, (GMLP_CHUNK, GMLP_CHUNK), 1)
    keep = (col // CAUSAL_CHUNK) <= (row // CAUSAL_CHUNK)
    bs = bs_ref[...]
    for g in range(A_GROUPS):
        wg = jnp.where(keep, ws_ref[g], 0.0).astype(BF16)
        bg = bs[:, g:g + 1]
        for c in range(tm // GMLP_CHUNK):
            rs = slice(c * GMLP_CHUNK, (c + 1) * GMLP_CHUNK)
            cs = slice(g * gd, (g + 1) * gd)
            sv = _dot(wg, v[rs, cs]) + bg
            z_ref[rs, cs] = (u[rs, cs] * sv).astype(BF16)
    o_ref[...] = x + _dot(z_ref[...], wout_ref[...])


def _mixer_a(h, g, w_in, gv, ws, bs_t, w_out, *, tm):
    n, d = h.shape
    half = w_out.shape[0]
    return pl.pallas_call(
        _mixer_a_kernel,
        out_shape=jax.ShapeDtypeStruct((n, d), F32),
        grid=(n // tm,),
        in_specs=[
            pl.BlockSpec((tm, d), lambda i: (i, 0)),
            _const_spec(g.shape), _const_spec(w_in.shape), _const_spec(gv.shape),
            _const_spec(ws.shape), _const_spec(bs_t.shape), _const_spec(w_out.shape),
        ],
        out_specs=pl.BlockSpec((tm, d), lambda i: (i, 0)),
        scratch_shapes=[pltpu.VMEM((tm, half), BF16)],
        compiler_params=_params(("parallel",)),
        name="mixer_a",
    )(h, g, w_in, gv, ws, bs_t, w_out)


def _swiglu_kernel(x_ref, g_ref, win_ref, wout_ref, o_ref):
    f = wout_ref.shape[0]
    x = x_ref[...]
    xb = _rms(x, g_ref[...]).astype(BF16)
    a = _dot(xb, win_ref[:, :f])
    b = _dot(xb, win_ref[:, f:])
    hm = (a * jax.nn.sigmoid(a) * b).astype(BF16)
    o_ref[...] = x + _dot(hm, wout_ref[...])


def _swiglu(h, g, w_in, w_out, *, tm):
    n, d = h.shape
    return pl.pallas_call(
        _swiglu_kernel,
        out_shape=jax.ShapeDtypeStruct((n, d), F32),
        grid=(n // tm,),
        in_specs=[
            pl.BlockSpec((tm, d), lambda i: (i, 0)),
            _const_spec(g.shape), _const_spec(w_in.shape), _const_spec(w_out.shape),
        ],
        out_specs=pl.BlockSpec((tm, d), lambda i: (i, 0)),
        compiler_params=_params(("parallel",)),
        name="swiglu",
    )(h, g, w_in, w_out)


def _lane_cumsum(x):
    n = x.shape[-1]
    lane = lax.broadcasted_iota(jnp.int32, x.shape, x.ndim - 1)
    sh = 1
    while sh < n:
        x = x + jnp.where(lane >= sh, pltpu.roll(x, sh, axis=x.ndim - 1), 0.0)
        sh *= 2
    return x


def _split_bf16(x):
    pieces = []
    for _ in range(N_PIECES):
        p = x.astype(BF16)
        pieces.append(p)
        x = x - p.astype(F32)
    return pieces


def _qkv_kernel(x_ref, gkv_ref, gb_ref, wk_ref, wvt_ref, wft_ref, bf_ref, gk_ref, gq_ref, wqg_ref,
                hsum_ref, q_ref, k_ref, vt_ref, sg_ref, fp_ref, carry_ref, *, tiles_per_seq):
    d = x_ref.shape[1]
    i = pl.program_id(0)

    @pl.when(i == 0)
    def _():
        carry_ref[...] = jnp.zeros_like(carry_ref)

    x = x_ref[...]
    y = x * lax.rsqrt(jnp.mean(x * x, axis=-1, keepdims=True) + RMS_EPS)
    skv = (y * gkv_ref[...]).astype(BF16)
    sb = (y * gb_ref[...]).astype(BF16)
    hsum = hsum_ref[...]

    def head_norm(t, gain):
        sq = (t * t).astype(BF16)
        ssq = jnp.concatenate([_dot(sq[:, j * LANES:(j + 1) * LANES], hsum)
                               for j in range(d // LANES)], axis=1)
        return t * lax.rsqrt(ssq * (1.0 / HEAD_DIM) + RMS_EPS) * gain

    f = _dot_nt(wft_ref[...], skv) + bf_ref[...]
    logf = jax.nn.log_sigmoid(f)
    carry = jnp.where(i % tiles_per_seq == 0, 0.0, carry_ref[:, 0:1])
    cum = _lane_cumsum(logf) + carry
    carry_ref[...] = jnp.broadcast_to(cum[:, -1:], carry_ref.shape)
    fp_ref[0] = jnp.concatenate(_split_bf16(cum * LOG2E), axis=0)

    k = _dot(skv, wk_ref[...])
    k_ref[...] = head_norm(k, gk_ref[...]).astype(BF16)
    vt_ref[0] = _dot_nt(wvt_ref[...], skv).astype(BF16)
    q = _dot(sb, wqg_ref[:, :d])
    q_ref[...] = (head_norm(q, gq_ref[...]) * (LOG2E * HEAD_DIM ** -0.5)).astype(BF16)
    sg_ref[...] = jax.nn.sigmoid(_dot(sb, wqg_ref[:, d:])).astype(BF16)


def _qkv(h, gkv, gb, wk, wvt, wft, bf, gk, gq, wqg, hsum, *, tm, batch, seq):
    n, d = h.shape
    nh = wft.shape[0]
    tps = seq // tm
    tok = pl.BlockSpec((tm, d), lambda i: (i, 0))
    tok_bf = jax.ShapeDtypeStruct((n, d), BF16)
    seq_map = lambda i: (i // tps, 0, i % tps)
    consts = (gkv, gb, wk, wvt, wft, bf, gk, gq, wqg, hsum)
    return pl.pallas_call(
        functools.partial(_qkv_kernel, tiles_per_seq=tps),
        out_shape=[tok_bf, tok_bf, jax.ShapeDtypeStruct((batch, d, seq), BF16), tok_bf,
                   jax.ShapeDtypeStruct((batch, N_PIECES * nh, seq), BF16)],
        grid=(n // tm,),
        in_specs=[tok] + [_const_spec(a.shape) for a in consts],
        out_specs=[tok, tok, pl.BlockSpec((1, d, tm), seq_map), tok,
                   pl.BlockSpec((1, N_PIECES * nh, tm), seq_map)],
        scratch_shapes=[pltpu.VMEM((nh, LANES), F32)],
        compiler_params=_params(("arbitrary",)),
        name="qkv",
    )(h, *consts)


def _attention_kernel(q_ref, k_ref, vt_ref, fp_ref, sg_ref, *rest, blk, nq, n_riders):
    rider_in, (o_ref, *rider_out) = rest[:n_riders], rest[n_riders:2 * n_riders + 1]
    kaug_ref, qaug_ref, vsum_ref = rest[2 * n_riders + 1:]
    for w_ref, c_ref in zip(rider_in, rider_out):
        c_ref[...] = w_ref[...].astype(BF16)
    hp = pl.program_id(1)
    nh = fp_ref.shape[2] // N_PIECES
    lane = lax.broadcasted_iota(jnp.int32, (1, LANES), 1)
    own = (lane < HEAD_DIM, lane >= HEAD_DIM)
    spare = (HEAD_DIM, 0)
    q_lanes = [(lane >= s) & (lane < s + N_PIECES) for s in spare]
    k_lanes = [(lane >= s + N_PIECES) & (lane < s + 2 * N_PIECES) for s in spare]

    er = lax.broadcasted_iota(jnp.int32, (N_PIECES * nh, LANES), 0)
    ec = lax.broadcasted_iota(jnp.int32, (N_PIECES * nh, LANES), 1)
    place = jnp.zeros((N_PIECES * nh, LANES), F32)
    for h in range(2):
        for j in range(N_PIECES):
            src = er == j * nh + 2 * hp + h
            place = (place + (src & (ec == spare[h] + j)).astype(F32)
                     - (src & (ec == spare[h] + N_PIECES + j)).astype(F32))
    g = _dot(fp_ref[0], place.astype(BF16))
    for h in range(2):
        k_fill = jnp.where(k_lanes[h], g, q_lanes[h].astype(F32)).astype(BF16)
        kaug_ref[h] = jnp.where(own[h], k_ref[...], k_fill)
        q_fill = jnp.where(q_lanes[h], g, k_lanes[h].astype(F32)).astype(BF16)
        qaug_ref[h] = jnp.where(own[h], q_ref[...], q_fill)

    first_rows = lax.broadcasted_iota(jnp.int32, (LANES, 1), 0) < HEAD_DIM
    vsum_ref[0] = jnp.where(first_rows, vt_ref[0], jnp.ones_like(vt_ref[0]))
    vsum_ref[1] = jnp.where(first_rows, jnp.ones_like(vt_ref[0]), vt_ref[0])
    key = lax.broadcasted_iota(jnp.int32, (blk, blk), 0)
    qry = lax.broadcasted_iota(jnp.int32, (blk, blk), 1)
    causal = qry >= key

    def qk(i):
        lo, hi = i * blk, (i + 1) * blk
        return [_dot_nt(kaug_ref[h, 0:hi, :], qaug_ref[h, lo:hi, :]) for h in range(2)]

    scores = qk(0)
    for i in range(nq):
        lo, hi = i * blk, (i + 1) * blk
        next_scores = qk(i + 1) if i + 1 < nq else None
        probs = []
        for s in scores:
            diag = jnp.where(causal, s[lo:hi], NEG_INF)
            s = diag if i == 0 else jnp.concatenate([s[0:lo], diag], axis=0)
            probs.append(jnp.exp2(s - jnp.max(s, axis=0, keepdims=True)).astype(BF16))
        outs = [_dot(vsum_ref[h, :, 0:hi], probs[h]) for h in range(2)]
        o0 = outs[0] / outs[0][HEAD_DIM:HEAD_DIM + 1, :]
        o1 = outs[1] / outs[1][0:1, :]
        o = jnp.where(first_rows, o0, o1).T
        o_ref[lo:hi, :] = (o * sg_ref[lo:hi, :].astype(F32)).astype(BF16)
        scores = next_scores


def _attention(q, k, vt, fp, sg, riders, *, batch, seq, blk):
    n, d = q.shape
    n_pairs = d // LANES
    steps = batch * n_pairs
    pair = pl.BlockSpec((seq, LANES), lambda b, hp: (b, hp))
    slabs = [w.reshape(steps, -1, w.shape[-1]) for w in riders]
    slab_specs = [pl.BlockSpec((1,) + s.shape[1:], lambda b, hp: (b * n_pairs + hp, 0, 0))
                  for s in slabs]
    out = pl.pallas_call(
        functools.partial(_attention_kernel, blk=blk, nq=seq // blk, n_riders=len(riders)),
        out_shape=[jax.ShapeDtypeStruct((n, d), BF16)]
        + [jax.ShapeDtypeStruct(s.shape, BF16) for s in slabs],
        grid=(batch, n_pairs),
        in_specs=[
            pair, pair,
            pl.BlockSpec((1, LANES, seq), lambda b, hp: (b, hp, 0)),
            pl.BlockSpec((1, seq, fp.shape[2]), lambda b, hp: (b, 0, 0)),
            pair,
        ] + slab_specs,
        out_specs=[pair] + slab_specs,
        scratch_shapes=[pltpu.VMEM((2, seq, LANES), BF16)] * 2 + [pltpu.VMEM((2, LANES, seq), BF16)],
        compiler_params=_params(("parallel", "parallel")),
        name="attention",
    )(q, k, vt, fp, sg, *slabs)
    return out[0], [c.reshape(w.shape) for c, w in zip(out[1:], riders)]


def _pack_bf16_pair(lo, hi):
    ulo = lax.bitcast_convert_type(lo.astype(BF16).astype(F32), jnp.uint32)
    uhi = lax.bitcast_convert_type(hi.astype(BF16).astype(F32), jnp.uint32)
    return (ulo >> 16) | uhi


def _pack_row(x):
    half = x.shape[1] // 2
    w = _pack_bf16_pair(x[:, :half], x[:, half:])
    sw = half // SUBROWS
    return [w[:, c * sw:(c + 1) * sw] for c in range(SUBROWS)]


def _unpack_row(subrows):
    lo = [lax.bitcast_convert_type(p << 16, F32) for p in subrows]
    hi = [lax.bitcast_convert_type(p & jnp.uint32(0xFFFF0000), F32) for p in subrows]
    return jnp.concatenate(lo + hi, axis=1)


def _out_router_kernel(og_ref, h_ref, wo_ref, gm_ref, wrt_ref, h2_ref, xp_ref, meta_ref, cnt_ref,
                       carry_ref):
    i = pl.program_id(0)
    ne, tm = meta_ref.shape
    h2 = h_ref[...] + _dot(og_ref[...], wo_ref[...])
    h2_ref[...] = h2
    xn = _rms(h2, gm_ref[...])
    for c, sub in enumerate(_pack_row(xn)):
        xp_ref[c] = sub
    xh = xn.astype(BF16)
    xl = (xn - xh.astype(F32)).astype(BF16)
    wr = wrt_ref[...]
    wh = wr.astype(BF16)
    wl = (wr - wh.astype(F32)).astype(BF16)
    logits = _dot_nt(wh, xh) + (_dot_nt(wh, xl) + _dot_nt(wl, xh))
    row = lax.broadcasted_iota(jnp.int32, (ne, tm), 0).astype(F32)
    v1 = jnp.max(logits, axis=0, keepdims=True)
    i1 = jnp.min(jnp.where(logits == v1, row, ne), axis=0, keepdims=True)
    rest = jnp.where(row == i1, -jnp.inf, logits)
    v2 = jnp.max(rest, axis=0, keepdims=True)
    i2 = jnp.min(jnp.where(rest == v2, row, ne), axis=0, keepdims=True)
    e = jnp.exp(v2 - v1)
    w1 = 1.0 / (1.0 + e)
    w2 = e / (1.0 + e)
    sel1 = row == i1
    sel2 = row == i2
    oh = (sel1 | sel2).astype(F32)

    @pl.when(i == 0)
    def _():
        carry_ref[...] = jnp.zeros_like(carry_ref)

    base = carry_ref[:, 0:1]
    rank = base + (_lane_cumsum(oh) - oh)
    r1 = jnp.sum(jnp.where(sel1, rank, 0.0), axis=0, keepdims=True)
    r2 = jnp.sum(jnp.where(sel2, rank, 0.0), axis=0, keepdims=True)
    total = base + jnp.sum(oh, axis=1, keepdims=True)
    carry_ref[...] = jnp.broadcast_to(total, carry_ref.shape)
    cnt_ref[...] = jnp.broadcast_to(total, cnt_ref.shape)
    zeros = jnp.zeros_like(w1)
    meta_ref[...] = jnp.concatenate(
        [i1.astype(F32), i2.astype(F32), w1, w2, r1, r2, zeros, zeros], axis=0)


def _out_router(og, h, wo, gm, wrt, *, tm):
    n, d = h.shape
    ne = wrt.shape[0]
    sw = d // 2 // SUBROWS
    tok = pl.BlockSpec((tm, d), lambda i: (i, 0))
    return pl.pallas_call(
        _out_router_kernel,
        out_shape=[jax.ShapeDtypeStruct((n, d), F32), jax.ShapeDtypeStruct((SUBROWS, n, sw), jnp.uint32),
                   jax.ShapeDtypeStruct((ne, n), F32), jax.ShapeDtypeStruct((ne, LANES), F32)],
        grid=(n // tm,),
        in_specs=[tok, tok, _const_spec(wo.shape), _const_spec(gm.shape), _const_spec(wrt.shape)],
        out_specs=[tok, pl.BlockSpec((SUBROWS, tm, sw), lambda i: (0, i, 0)),
                   pl.BlockSpec((ne, tm), lambda i: (0, i)),
                   pl.BlockSpec((ne, LANES), lambda i: (0, 0))],
        scratch_shapes=[pltpu.VMEM((ne, LANES), F32)],
        compiler_params=_params(("arbitrary",)),
        name="out_router",
    )(og, h, wo, gm, wrt)


def _sc_mesh():
    return plsc.VectorSubcoreMesh(core_axis_name="core", subcore_axis_name="subcore")


def _sc_scatter(x, idx, out_rows):
    rows, w = x.shape
    n_idx = idx.shape[0]

    @pl.kernel(out_type=jax.ShapeDtypeStruct((out_rows, w), x.dtype), mesh=_sc_mesh(),
               scratch_types=[])
    def scatter_rows(x_hbm, i_hbm, o_hbm):
        def body(x_vmem, i_vmem):
            for s in range(n_idx):
                pltpu.sync_copy(x_vmem, o_hbm.at[i_vmem.at[s]])

        pltpu.emit_pipeline(
            body, grid=(rows // SC_WINDOW,),
            in_specs=[pl.BlockSpec((SC_WINDOW, w), lambda i: (i, 0)),
                      pl.BlockSpec((n_idx, SC_WINDOW), lambda i: (0, i))],
            out_specs=[],
            core_axis_name=("core", "subcore"),
            dimension_semantics=(pltpu.PARALLEL,),
        )(x_hbm, i_hbm)

    return scatter_rows(x, idx)


def _sc_gather(x, idx):
    n = idx.shape[1]
    w = x.shape[1]

    @pl.kernel(out_type=jax.ShapeDtypeStruct((n, w), x.dtype), mesh=_sc_mesh(), scratch_types=[])
    def gather_rows(x_hbm, i_hbm, o_hbm):
        def body(i_vmem, o_vmem):
            pltpu.sync_copy(x_hbm.at[i_vmem.at[0]], o_vmem)

        pltpu.emit_pipeline(
            body, grid=(n // SC_WINDOW,),
            in_specs=[pl.BlockSpec((1, SC_WINDOW), lambda i: (0, i))],
            out_specs=[pl.BlockSpec((SC_WINDOW, w), lambda i: (i, 0))],
            core_axis_name=("core", "subcore"),
            dimension_semantics=(pltpu.PARALLEL,),
        )(i_hbm, o_hbm)

    return gather_rows(x, idx)


def _experts_kernel(te_ref, rows_ref, x_ref, wa_ref, wb_ref, wo_ref, o_ref, xb_ref, hm_ref, *, nf):
    del te_ref
    i = pl.program_id(0)
    f = pl.program_id(1)
    tr = xb_ref.shape[0]
    tf = wa_ref.shape[2]
    n_valid = rows_ref[i]

    @pl.when((n_valid > 0) & (f == 0))
    def _():
        x = _unpack_row([x_ref[c] for c in range(SUBROWS)])
        live = lax.broadcasted_iota(jnp.int32, (tr, 1), 0) < n_valid
        xb_ref[...] = jnp.where(live, x, 0.0).astype(BF16)

    ts = tf // EXPERT_SUBSLABS
    for j in range(nf):
        @pl.when((n_valid > 0) & (f == j))
        def _(j=j):
            xb = xb_ref[...]
            for c in range(EXPERT_SUBSLABS):
                a = _dot(xb, wa_ref[0, :, c * ts:(c + 1) * ts])
                b = _dot(xb, wb_ref[0, :, c * ts:(c + 1) * ts])
                col = j * tf + c * ts
                hm_ref[:, col:col + ts] = (a * jax.nn.sigmoid(a) * b).astype(BF16)

    @pl.when((n_valid > 0) & (f == nf - 1))
    def _():
        for c, sub in enumerate(_pack_row(_dot(hm_ref[...], wo_ref[0]))):
            o_ref[c] = sub

    @pl.when((n_valid == 0) & (f == 0))
    def _():
        o_ref[...] = jnp.zeros_like(o_ref)


def _experts(tile_expert, tile_rows, xs, w_in, w_out, *, tr, tf):
    _, p, sw = xs.shape
    d = w_out.shape[2]
    de = w_out.shape[1]
    nf = de // tf
    rows = pl.BlockSpec((SUBROWS, tr, sw), lambda i, f, te, tv: (0, i, 0))
    return pl.pallas_call(
        functools.partial(_experts_kernel, nf=nf),
        out_shape=jax.ShapeDtypeStruct((SUBROWS, p, sw), jnp.uint32),
        grid_spec=pltpu.PrefetchScalarGridSpec(
            num_scalar_prefetch=2,
            grid=(p // tr, nf),
            in_specs=[
                rows,
                pl.BlockSpec((1, d, tf), lambda i, f, te, tv: (te[i], 0, f)),
                pl.BlockSpec((1, d, tf), lambda i, f, te, tv: (te[i], 0, nf + f)),
                pl.BlockSpec((1, de, d), lambda i, f, te, tv: (te[i], 0, 0),
                             pipeline_mode=pl.Buffered(1)),
            ],
            out_specs=rows,
            scratch_shapes=[pltpu.VMEM((tr, d), BF16), pltpu.VMEM((tr, de), BF16)],
        ),
        compiler_params=_params(("arbitrary", "arbitrary")),
        name="experts",
    )(tile_expert, tile_rows, xs, w_in, w_in, w_out)


def _combine_kernel(g_ref, h_ref, w_ref, o_ref):
    w = w_ref[...]
    y = [_unpack_row([g_ref[s, c] for c in range(SUBROWS)]) for s in range(TOP_K)]
    o_ref[...] = h_ref[...] + (w[:, 0:1] * y[0] + w[:, 1:2] * y[1])


def _combine(g, h, w, *, tc):
    n, d = h.shape
    sw = g.shape[3]
    tok = pl.BlockSpec((tc, d), lambda i: (i, 0))
    return pl.pallas_call(
        _combine_kernel,
        out_shape=jax.ShapeDtypeStruct((n, d), F32),
        grid=(n // tc,),
        in_specs=[pl.BlockSpec((TOP_K, SUBROWS, tc, sw), lambda i: (0, 0, i, 0)), tok,
                  pl.BlockSpec((tc, TOP_K), lambda i: (i, 0))],
        out_specs=tok,
        compiler_params=_params(("parallel",)),
        name="combine",
    )(g, h, w)


def _tiles(n, seq):
    def pick(limit, of):
        t = limit
        while of % t:
            t //= 2
        return t
    return dict(
        tm_a=pick(512, seq), tm_f=pick(512, n), tm_qkv=pick(1024, seq), blk=pick(256, seq),
        tm_o=pick(1024, n), tr=pick(1024, TOP_K * n), tc=pick(1024, n))


def kernel(x, a_norm_g, a_w_in, a_v_norm_g, a_w_spatial, a_b_spatial, a_w_out, f_norm_g, f_w_in, f_w_out, kv_norm_g, kv_w, kv_b_f, k_norm_g, b_norm_g, b_w_in, q_norm_g, b_w_out, m_norm_g, m_w_router, m_w_in, m_w_out):
    batch, seq, d = x.shape
    n = batch * seq
    nh = d // HEAD_DIM
    ne = m_w_router.shape[-1]
    assert a_w_in.shape[0] == 1 and b_w_in.shape[0] == 1 and f_w_in.shape[0] == 1 and m_w_in.shape[0] == 1
    assert seq % GMLP_CHUNK == 0 and d % LANES == 0 and (SUBROWS * n) % SC_WINDOW == 0
    t = _tiles(n, seq)
    row = lambda g: g.reshape(1, -1)

    h = x.reshape(n, d)
    h = _mixer_a(h, row(a_norm_g[0]), a_w_in[0].astype(BF16), row(a_v_norm_g[0]), a_w_spatial[0],
                 a_b_spatial[0].T, a_w_out[0].astype(BF16), tm=t["tm_a"])
    h = _swiglu(h, row(f_norm_g[0]), f_w_in[0].astype(BF16), f_w_out[0].astype(BF16), tm=t["tm_f"])

    head = jnp.arange(LANES, dtype=jnp.int32) // HEAD_DIM
    hsum = (head[:, None] == head[None, :]).astype(BF16)
    q, k, vt, sg, fp = _qkv(
        h, row(kv_norm_g), row(b_norm_g[0]), kv_w[:, :d].astype(BF16),
        kv_w[:, d:2 * d].T.astype(BF16), kv_w[:, 2 * d:].T.astype(BF16), kv_b_f.reshape(nh, 1),
        row(jnp.tile(k_norm_g, nh)), row(jnp.tile(q_norm_g[0], nh)), b_w_in[0].astype(BF16), hsum,
        tm=t["tm_qkv"], batch=batch, seq=seq)
    og, (moe_w_in, moe_w_out) = _attention(q, k, vt, jnp.swapaxes(fp, 1, 2), sg,
                                           (m_w_in[0], m_w_out[0]), batch=batch, seq=seq, blk=t["blk"])

    h2, xp, meta, cnt = _out_router(og, h, b_w_out[0].astype(BF16), row(m_norm_g[0]),
                                    m_w_router[0].T, tm=t["tm_o"])

    tr = t["tr"]
    n_tiles = TOP_K * n // tr + ne
    p = n_tiles * tr
    counts = cnt[:, 0].astype(jnp.int32)
    tiles_per_expert = (counts + tr - 1) // tr
    tile_end = jnp.cumsum(tiles_per_expert)
    tile_start = tile_end - tiles_per_expert
    expert_ids = jnp.arange(ne, dtype=jnp.int32)
    idx = meta[0:TOP_K].astype(jnp.int32)
    start_of = jnp.sum(jnp.where(idx[:, :, None] == expert_ids, tile_start * tr, 0), axis=-1)
    dest = start_of + meta[4:4 + TOP_K].astype(jnp.int32)
    tile_ids = jnp.arange(n_tiles, dtype=jnp.int32)
    tile_expert = jnp.minimum(
        jnp.sum((tile_ids[:, None] >= tile_end[None, :]).astype(jnp.int32), axis=1), ne - 1)
    mine = tile_expert[:, None] == expert_ids[None, :]
    tile_rows = jnp.clip(
        jnp.sum(jnp.where(mine, counts - (tile_ids[:, None] - tile_start) * tr, 0), axis=1), 0, tr)
    tile_rows = jnp.where(tile_ids < tile_end[-1], tile_rows, 0).astype(jnp.int32)

    sub = (jnp.arange(SUBROWS, dtype=jnp.int32) * p)[None, :, None]
    sub_dest = sub + dest[:, None, :]
    sw = xp.shape[2]
    xs = _sc_scatter(xp.reshape(SUBROWS * n, sw), sub_dest.reshape(TOP_K, SUBROWS * n), SUBROWS * p)
    eo = _experts(tile_expert, tile_rows, xs.reshape(SUBROWS, p, sw), moe_w_in, moe_w_out,
                  tr=tr, tf=m_w_out.shape[2] // EXPERT_STEPS)
    g = _sc_gather(eo.reshape(SUBROWS * p, sw), sub_dest.reshape(1, TOP_K * SUBROWS * n))
    out = _combine(g.reshape(TOP_K, SUBROWS, n, sw), h2, meta[2:2 + TOP_K].T, tc=t["tc"])
    return out.reshape(batch, seq, d)
```

```python
import functools

import jax
import jax.numpy as jnp
from jax import lax
from jax.experimental import pallas as pl
from jax.experimental.pallas import tpu as pltpu
from jax.experimental.pallas import tpu_sc as plsc

RMS_EPS = 1e-6
NEG_INF = -1e30
LOG2E = 1.4426950408889634
N_PIECES = 3
SC_WINDOW = 128
SUBROWS = 2
EXPERT_STEPS = 2
EXPERT_SUBSLABS = 2
ROW_BLOCK = 256
GMLP_CHUNK = 128
CAUSAL_CHUNK = 64
A_GROUPS = 8
HEAD_DIM = 64
LANES = 128
TOP_K = 2
VMEM_LIMIT = 56 * 1024 * 1024

BF16 = jnp.bfloat16
F32 = jnp.float32


def _dot(a, b, **kw):
    return jnp.dot(a, b, preferred_element_type=F32, **kw)


def _dot_nt(a, b, **kw):
    return lax.dot_general(a, b, (((1,), (1,)), ((), ())), preferred_element_type=F32, **kw)


def _rms(x, g):
    return x * lax.rsqrt(jnp.mean(x * x, axis=-1, keepdims=True) + RMS_EPS) * g


def _gelu(x):
    return 0.5 * x * (1.0 + lax.erf(x * (2.0 ** -0.5)))


def _const_spec(shape):
    nd = len(shape)
    return pl.BlockSpec(shape, lambda *_: (0,) * nd, pipeline_mode=pl.Buffered(1))


def _params(sem):
    return pltpu.CompilerParams(dimension_semantics=sem, vmem_limit_bytes=VMEM_LIMIT)


def _mixer_a_kernel(x_ref, g_ref, win_ref, gv_ref, ws_ref, bs_ref, wout_ref, o_ref, z_ref):
    tm = x_ref.shape[0]
    half = wout_ref.shape[0]
    gd = half // A_GROUPS
    x = x_ref[...]
    xb = _rms(x, g_ref[...]).astype(BF16)
    v = _gelu(_dot(xb, win_ref[:, half:]))
    v = (_rms(v, gv_ref[...])).astype(BF16)
    u = _gelu(_dot(xb, win_ref[:, :half]))
    row = lax.broadcasted_iota(jnp.int32, (GMLP_CHUNK, GMLP_CHUNK), 0)
    col = lax.broadcasted_iota(jnp.int32, (GMLP_CHUNK, GMLP_CHUNK), 1)
    keep = (col // CAUSAL_CHUNK) <= (row // CAUSAL_CHUNK)
    bs = bs_ref[...]
    for g in range(A_GROUPS):
        wg = jnp.where(keep, ws_ref[g], 0.0).astype(BF16)
        bg = bs[:, g:g + 1]
        for c in range(tm // GMLP_CHUNK):
            rs = slice(c * GMLP_CHUNK, (c + 1) * GMLP_CHUNK)
            cs = slice(g * gd, (g + 1) * gd)
            sv = _dot(wg, v[rs, cs]) + bg
            z_ref[rs, cs] = (u[rs, cs] * sv).astype(BF16)
    o_ref[...] = x + _dot(z_ref[...], wout_ref[...])


def _mixer_a(h, g, w_in, gv, ws, bs_t, w_out, *, tm):
    n, d = h.shape
    half = w_out.shape[0]
    return pl.pallas_call(
        _mixer_a_kernel,
        out_shape=jax.ShapeDtypeStruct((n, d), F32),
        grid=(n // tm,),
        in_specs=[
            pl.BlockSpec((tm, d), lambda i: (i, 0)),
            _const_spec(g.shape), _const_spec(w_in.shape), _const_spec(gv.shape),
            _const_spec(ws.shape), _const_spec(bs_t.shape), _const_spec(w_out.shape),
        ],
        out_specs=pl.BlockSpec((tm, d), lambda i: (i, 0)),
        scratch_shapes=[pltpu.VMEM((tm, half), BF16)],
        compiler_params=_params(("parallel",)),
        name="mixer_a",
    )(h, g, w_in, gv, ws, bs_t, w_out)


def _swiglu_kernel(x_ref, g_ref, win_ref, wout_ref, o_ref):
    f = wout_ref.shape[0]
    x = x_ref[...]
    xb = _rms(x, g_ref[...]).astype(BF16)
    a = _dot(xb, win_ref[:, :f])
    b = _dot(xb, win_ref[:, f:])
    hm = (a * jax.nn.sigmoid(a) * b).astype(BF16)
    o_ref[...] = x + _dot(hm, wout_ref[...])


def _swiglu(h, g, w_in, w_out, *, tm):
    n, d = h.shape
    return pl.pallas_call(
        _swiglu_kernel,
        out_shape=jax.ShapeDtypeStruct((n, d), F32),
        grid=(n // tm,),
        in_specs=[
            pl.BlockSpec((tm, d), lambda i: (i, 0)),
            _const_spec(g.shape), _const_spec(w_in.shape), _const_spec(w_out.shape),
        ],
        out_specs=pl.BlockSpec((tm, d), lambda i: (i, 0)),
        compiler_params=_params(("parallel",)),
        name="swiglu",
    )(h, g, w_in, w_out)


def _lane_cumsum(x):
    n = x.shape[-1]
    lane = lax.broadcasted_iota(jnp.int32, x.shape, x.ndim - 1)
    sh = 1
    while sh < n:
        x = x + jnp.where(lane >= sh, pltpu.roll(x, sh, axis=x.ndim - 1), 0.0)
        sh *= 2
    return x


def _split_bf16(x):
    pieces = []
    for _ in range(N_PIECES):
        p = x.astype(BF16)
        pieces.append(p)
        x = x - p.astype(F32)
    return pieces


def _qkv_kernel(x_ref, gkv_ref, gb_ref, wk_ref, wvt_ref, wft_ref, bf_ref, gk_ref, gq_ref, wqg_ref,
                hsum_ref, q_ref, k_ref, vt_ref, sg_ref, fp_ref, carry_ref, *, tiles_per_seq):
    d = x_ref.shape[1]
    i = pl.program_id(0)

    @pl.when(i == 0)
    def _():
        carry_ref[...] = jnp.zeros_like(carry_ref)

    x = x_ref[...]
    y = x * lax.rsqrt(jnp.mean(x * x, axis=-1, keepdims=True) + RMS_EPS)
    skv = (y * gkv_ref[...]).astype(BF16)
    sb = (y * gb_ref[...]).astype(BF16)
    hsum = hsum_ref[...]

    def head_norm(t, gain):
        sq = (t * t).astype(BF16)
        ssq = jnp.concatenate([_dot(sq[:, j * LANES:(j + 1) * LANES], hsum)
                               for j in range(d // LANES)], axis=1)
        return t * lax.rsqrt(ssq * (1.0 / HEAD_DIM) + RMS_EPS) * gain

    f = _dot_nt(wft_ref[...], skv) + bf_ref[...]
    logf = jax.nn.log_sigmoid(f)
    carry = jnp.where(i % tiles_per_seq == 0, 0.0, carry_ref[:, 0:1])
    cum = _lane_cumsum(logf) + carry
    carry_ref[...] = jnp.broadcast_to(cum[:, -1:], carry_ref.shape)
    fp_ref[0] = jnp.concatenate(_split_bf16(cum * LOG2E), axis=0)

    k = _dot(skv, wk_ref[...])
    k_ref[...] = head_norm(k, gk_ref[...]).astype(BF16)
    vt_ref[0] = _dot_nt(wvt_ref[...], skv).astype(BF16)
    q = _dot(sb, wqg_ref[:, :d])
    q_ref[...] = (head_norm(q, gq_ref[...]) * (LOG2E * HEAD_DIM ** -0.5)).astype(BF16)
    sg_ref[...] = jax.nn.sigmoid(_dot(sb, wqg_ref[:, d:])).astype(BF16)


def _qkv(h, gkv, gb, wk, wvt, wft, bf, gk, gq, wqg, hsum, *, tm, batch, seq):
    n, d = h.shape
    nh = wft.shape[0]
    tps = seq // tm
    tok = pl.BlockSpec((tm, d), lambda i: (i, 0))
    tok_bf = jax.ShapeDtypeStruct((n, d), BF16)
    seq_map = lambda i: (i // tps, 0, i % tps)
    consts = (gkv, gb, wk, wvt, wft, bf, gk, gq, wqg, hsum)
    return pl.pallas_call(
        functools.partial(_qkv_kernel, tiles_per_seq=tps),
        out_shape=[tok_bf, tok_bf, jax.ShapeDtypeStruct((batch, d, seq), BF16), tok_bf,
                   jax.ShapeDtypeStruct((batch, N_PIECES * nh, seq), BF16)],
        grid=(n // tm,),
        in_specs=[tok] + [_const_spec(a.shape) for a in consts],
        out_specs=[tok, tok, pl.BlockSpec((1, d, tm), seq_map), tok,
                   pl.BlockSpec((1, N_PIECES * nh, tm), seq_map)],
        scratch_shapes=[pltpu.VMEM((nh, LANES), F32)],
        compiler_params=_params(("arbitrary",)),
        name="qkv",
    )(h, *consts)


def _attention_kernel(q_ref, k_ref, vt_ref, fp_ref, sg_ref, *rest, blk, nq, n_riders):
    rider_in, (o_ref, *rider_out) = rest[:n_riders], rest[n_riders:2 * n_riders + 1]
    kaug_ref, qaug_ref, vsum_ref = rest[2 * n_riders + 1:]
    for w_ref, c_ref in zip(rider_in, rider_out):
        c_ref[...] = w_ref[...].astype(BF16)
    hp = pl.program_id(1)
    nh = fp_ref.shape[2] // N_PIECES
    lane = lax.broadcasted_iota(jnp.int32, (1, LANES), 1)
    own = (lane < HEAD_DIM, lane >= HEAD_DIM)
    spare = (HEAD_DIM, 0)
    q_lanes = [(lane >= s) & (lane < s + N_PIECES) for s in spare]
    k_lanes = [(lane >= s + N_PIECES) & (lane < s + 2 * N_PIECES) for s in spare]

    er = lax.broadcasted_iota(jnp.int32, (N_PIECES * nh, LANES), 0)
    ec = lax.broadcasted_iota(jnp.int32, (N_PIECES * nh, LANES), 1)
    place = jnp.zeros((N_PIECES * nh, LANES), F32)
    for h in range(2):
        for j in range(N_PIECES):
            src = er == j * nh + 2 * hp + h
            place = (place + (src & (ec == spare[h] + j)).astype(F32)
                     - (src & (ec == spare[h] + N_PIECES + j)).astype(F32))
    g = _dot(fp_ref[0], place.astype(BF16))
    for h in range(2):
        k_fill = jnp.where(k_lanes[h], g, q_lanes[h].astype(F32)).astype(BF16)
        kaug_ref[h] = jnp.where(own[h], k_ref[...], k_fill)
        q_fill = jnp.where(q_lanes[h], g, k_lanes[h].astype(F32)).astype(BF16)
        qaug_ref[h] = jnp.where(own[h], q_ref[...], q_fill)

    first_rows = lax.broadcasted_iota(jnp.int32, (LANES, 1), 0) < HEAD_DIM
    vsum_ref[0] = jnp.where(first_rows, vt_ref[0], jnp.ones_like(vt_ref[0]))
    vsum_ref[1] = jnp.where(first_rows, jnp.ones_like(vt_ref[0]), vt_ref[0])
    key = lax.broadcasted_iota(jnp.int32, (blk, blk), 0)
    qry = lax.broadcasted_iota(jnp.int32, (blk, blk), 1)
    causal = qry >= key

    def qk(i):
        lo, hi = i * blk, (i + 1) * blk
        return [_dot_nt(kaug_ref[h, 0:hi, :], qaug_ref[h, lo:hi, :]) for h in range(2)]

    scores = qk(0)
    for i in range(nq):
        lo, hi = i * blk, (i + 1) * blk
        next_scores = qk(i + 1) if i + 1 < nq else None
        probs = []
        for s in scores:
            diag = jnp.where(causal, s[lo:hi], NEG_INF)
            s = diag if i == 0 else jnp.concatenate([s[0:lo], diag], axis=0)
            probs.append(jnp.exp2(s - jnp.max(s, axis=0, keepdims=True)).astype(BF16))
        outs = [_dot(vsum_ref[h, :, 0:hi], probs[h]) for h in range(2)]
        o0 = outs[0] / outs[0][HEAD_DIM:HEAD_DIM + 1, :]
        o1 = outs[1] / outs[1][0:1, :]
        o = jnp.where(first_rows, o0, o1).T
        o_ref[lo:hi, :] = (o * sg_ref[lo:hi, :].astype(F32)).astype(BF16)
        scores = next_scores


def _attention(q, k, vt, fp, sg, riders, *, batch, seq, blk):
    n, d = q.shape
    n_pairs = d // LANES
    steps = batch * n_pairs
    pair = pl.BlockSpec((seq, LANES), lambda b, hp: (b, hp))
    slabs = [w.reshape(steps, -1, w.shape[-1]) for w in riders]
    slab_specs = [pl.BlockSpec((1,) + s.shape[1:], lambda b, hp: (b * n_pairs + hp, 0, 0))
                  for s in slabs]
    out = pl.pallas_call(
        functools.partial(_attention_kernel, blk=blk, nq=seq // blk, n_riders=len(riders)),
        out_shape=[jax.ShapeDtypeStruct((n, d), BF16)]
        + [jax.ShapeDtypeStruct(s.shape, BF16) for s in slabs],
        grid=(batch, n_pairs),
        in_specs=[
            pair, pair,
            pl.BlockSpec((1, LANES, seq), lambda b, hp: (b, hp, 0)),
            pl.BlockSpec((1, seq, fp.shape[2]), lambda b, hp: (b, 0, 0)),
            pair,
        ] + slab_specs,
        out_specs=[pair] + slab_specs,
        scratch_shapes=[pltpu.VMEM((2, seq, LANES), BF16)] * 2 + [pltpu.VMEM((2, LANES, seq), BF16)],
        compiler_params=_params(("parallel", "parallel")),
        name="attention",
    )(q, k, vt, fp, sg, *slabs)
    return out[0], [c.reshape(w.shape) for c, w in zip(out[1:], riders)]


def _pack_bf16_pair(lo, hi):
    ulo = lax.bitcast_convert_type(lo.astype(BF16).astype(F32), jnp.uint32)
    uhi = lax.bitcast_convert_type(hi.astype(BF16).astype(F32), jnp.uint32)
    return (ulo >> 16) | uhi


def _pack_row(x):
    half = x.shape[1] // 2
    w = _pack_bf16_pair(x[:, :half], x[:, half:])
    sw = half // SUBROWS
    return [w[:, c * sw:(c + 1) * sw] for c in range(SUBROWS)]


def _unpack_row(subrows):
    lo = [lax.bitcast_convert_type(p << 16, F32) for p in subrows]
    hi = [lax.bitcast_convert_type(p & jnp.uint32(0xFFFF0000), F32) for p in subrows]
    return jnp.concatenate(lo + hi, axis=1)


def _out_router_kernel(og_ref, h_ref, wo_ref, gm_ref, wrt_ref, h2_ref, xp_ref, meta_ref, cnt_ref,
                       carry_ref):
    i = pl.program_id(0)
    ne, tm = meta_ref.shape
    h2 = h_ref[...] + _dot(og_ref[...], wo_ref[...])
    h2_ref[...] = h2
    xn = _rms(h2, gm_ref[...])
    for c, sub in enumerate(_pack_row(xn)):
        xp_ref[c] = sub
    xh = xn.astype(BF16)
    xl = (xn - xh.astype(F32)).astype(BF16)
    wr = wrt_ref[...]
    wh = wr.astype(BF16)
    wl = (wr - wh.astype(F32)).astype(BF16)
    logits = _dot_nt(wh, xh) + (_dot_nt(wh, xl) + _dot_nt(wl, xh))
    row = lax.broadcasted_iota(jnp.int32, (ne, tm), 0).astype(F32)
    v1 = jnp.max(logits, axis=0, keepdims=True)
    i1 = jnp.min(jnp.where(logits == v1, row, ne), axis=0, keepdims=True)
    rest = jnp.where(row == i1, -jnp.inf, logits)
    v2 = jnp.max(rest, axis=0, keepdims=True)
    i2 = jnp.min(jnp.where(rest == v2, row, ne), axis=0, keepdims=True)
    e = jnp.exp(v2 - v1)
    w1 = 1.0 / (1.0 + e)
    w2 = e / (1.0 + e)
    sel1 = row == i1
    sel2 = row == i2
    oh = (sel1 | sel2).astype(F32)

    @pl.when(i == 0)
    def _():
        carry_ref[...] = jnp.zeros_like(carry_ref)

    base = carry_ref[:, 0:1]
    rank = base + (_lane_cumsum(oh) - oh)
    r1 = jnp.sum(jnp.where(sel1, rank, 0.0), axis=0, keepdims=True)
    r2 = jnp.sum(jnp.where(sel2, rank, 0.0), axis=0, keepdims=True)
    total = base + jnp.sum(oh, axis=1, keepdims=True)
    carry_ref[...] = jnp.broadcast_to(total, carry_ref.shape)
    cnt_ref[...] = jnp.broadcast_to(total, cnt_ref.shape)
    zeros = jnp.zeros_like(w1)
    meta_ref[...] = jnp.concatenate(
        [i1.astype(F32), i2.astype(F32), w1, w2, r1, r2, zeros, zeros], axis=0)


def _out_router(og, h, wo, gm, wrt, *, tm):
    n, d = h.shape
    ne = wrt.shape[0]
    sw = d // 2 // SUBROWS
    tok = pl.BlockSpec((tm, d), lambda i: (i, 0))
    return pl.pallas_call(
        _out_router_kernel,
        out_shape=[jax.ShapeDtypeStruct((n, d), F32), jax.ShapeDtypeStruct((SUBROWS, n, sw), jnp.uint32),
                   jax.ShapeDtypeStruct((ne, n), F32), jax.ShapeDtypeStruct((ne, LANES), F32)],
        grid=(n // tm,),
        in_specs=[tok, tok, _const_spec(wo.shape), _const_spec(gm.shape), _const_spec(wrt.shape)],
        out_specs=[tok, pl.BlockSpec((SUBROWS, tm, sw), lambda i: (0, i, 0)),
                   pl.BlockSpec((ne, tm), lambda i: (0, i)),
                   pl.BlockSpec((ne, LANES), lambda i: (0, 0))],
        scratch_shapes=[pltpu.VMEM((ne, LANES), F32)],
        compiler_params=_params(("arbitrary",)),
        name="out_router",
    )(og, h, wo, gm, wrt)


def _sc_mesh():
    return plsc.VectorSubcoreMesh(core_axis_name="core", subcore_axis_name="subcore")


def _sc_scatter(x, idx, out_rows):
    rows, w = x.shape
    n_idx = idx.shape[0]

    @pl.kernel(out_type=jax.ShapeDtypeStruct((out_rows, w), x.dtype), mesh=_sc_mesh(),
               scratch_types=[])
    def scatter_rows(x_hbm, i_hbm, o_hbm):
        def body(x_vmem, i_vmem):
            for s in range(n_idx):
                pltpu.sync_copy(x_vmem, o_hbm.at[i_vmem.at[s]])

        pltpu.emit_pipeline(
            body, grid=(rows // SC_WINDOW,),
            in_specs=[pl.BlockSpec((SC_WINDOW, w), lambda i: (i, 0)),
                      pl.BlockSpec((n_idx, SC_WINDOW), lambda i: (0, i))],
            out_specs=[],
            core_axis_name=("core", "subcore"),
            dimension_semantics=(pltpu.PARALLEL,),
        )(x_hbm, i_hbm)

    return scatter_rows(x, idx)


def _sc_gather(x, idx):
    n = idx.shape[1]
    w = x.shape[1]

    @pl.kernel(out_type=jax.ShapeDtypeStruct((n, w), x.dtype), mesh=_sc_mesh(), scratch_types=[])
    def gather_rows(x_hbm, i_hbm, o_hbm):
        def body(i_vmem, o_vmem):
            pltpu.sync_copy(x_hbm.at[i_vmem.at[0]], o_vmem)

        pltpu.emit_pipeline(
            body, grid=(n // SC_WINDOW,),
            in_specs=[pl.BlockSpec((1, SC_WINDOW), lambda i: (0, i))],
            out_specs=[pl.BlockSpec((SC_WINDOW, w), lambda i: (i, 0))],
            core_axis_name=("core", "subcore"),
            dimension_semantics=(pltpu.PARALLEL,),
        )(i_hbm, o_hbm)

    return gather_rows(x, idx)


def _experts_kernel(te_ref, rows_ref, x_ref, wa_ref, wb_ref, wo_ref, o_ref, xb_ref, hm_ref, *, nf):
    del te_ref
    i = pl.program_id(0)
    f = pl.program_id(1)
    tr = xb_ref.shape[0]
    tf = wa_ref.shape[2]
    n_valid = rows_ref[i]

    @pl.when((n_valid > 0) & (f == 0))
    def _():
        x = _unpack_row([x_ref[c] for c in range(SUBROWS)])
        live = lax.broadcasted_iota(jnp.int32, (tr, 1), 0) < n_valid
        xb_ref[...] = jnp.where(live, x, 0.0).astype(BF16)

    ts = tf // EXPERT_SUBSLABS
    full = n_valid == tr

    def hidden(j, rows):
        xb = xb_ref[rows, :]
        for c in range(EXPERT_SUBSLABS):
            a = _dot(xb, wa_ref[0, :, c * ts:(c + 1) * ts])
            b = _dot(xb, wb_ref[0, :, c * ts:(c + 1) * ts])
            col = j * tf + c * ts
            hm_ref[rows, col:col + ts] = (a * jax.nn.sigmoid(a) * b).astype(BF16)

    def project(rows):
        for c, sub in enumerate(_pack_row(_dot(hm_ref[rows, :], wo_ref[0]))):
            o_ref[c, rows, :] = sub

    def row_blocks(body):
        def step(r, carry):
            body(pl.ds(pl.multiple_of(r * ROW_BLOCK, ROW_BLOCK), ROW_BLOCK))
            return carry
        lax.fori_loop(0, (n_valid + ROW_BLOCK - 1) // ROW_BLOCK, step, 0)

    for j in range(nf):
        last = j == nf - 1

        @pl.when(full & (f == j))
        def _(j=j, last=last):
            hidden(j, slice(None))
            if last:
                project(slice(None))

        @pl.when(jnp.logical_not(full) & (n_valid > 0) & (f == j))
        def _(j=j, last=last):
            row_blocks(functools.partial(hidden, j))
            if last:
                row_blocks(project)

    @pl.when(jnp.logical_not(full) & (f == 0))
    def _():
        o_ref[...] = jnp.zeros_like(o_ref)


def _experts(tile_expert, tile_rows, xs, w_in, w_out, *, tr, tf):
    _, p, sw = xs.shape
    d = w_out.shape[2]
    de = w_out.shape[1]
    nf = de // tf
    rows = pl.BlockSpec((SUBROWS, tr, sw), lambda i, f, te, tv: (0, i, 0))
    return pl.pallas_call(
        functools.partial(_experts_kernel, nf=nf),
        out_shape=jax.ShapeDtypeStruct((SUBROWS, p, sw), jnp.uint32),
        grid_spec=pltpu.PrefetchScalarGridSpec(
            num_scalar_prefetch=2,
            grid=(p // tr, nf),
            in_specs=[
                rows,
                pl.BlockSpec((1, d, tf), lambda i, f, te, tv: (te[i], 0, f)),
                pl.BlockSpec((1, d, tf), lambda i, f, te, tv: (te[i], 0, nf + f)),
                pl.BlockSpec((1, de, d), lambda i, f, te, tv: (te[i], 0, 0),
                             pipeline_mode=pl.Buffered(1)),
            ],
            out_specs=rows,
            scratch_shapes=[pltpu.VMEM((tr, d), BF16), pltpu.VMEM((tr, de), BF16)],
        ),
        compiler_params=_params(("arbitrary", "arbitrary")),
        name="experts",
    )(tile_expert, tile_rows, xs, w_in, w_in, w_out)


def _combine_kernel(g_ref, h_ref, w_ref, o_ref):
    w = w_ref[...]
    y = [_unpack_row([g_ref[s, c] for c in range(SUBROWS)]) for s in range(TOP_K)]
    o_ref[...] = h_ref[...] + (w[:, 0:1] * y[0] + w[:, 1:2] * y[1])


def _combine(g, h, w, *, tc):
    n, d = h.shape
    sw = g.shape[3]
    tok = pl.BlockSpec((tc, d), lambda i: (i, 0))
    return pl.pallas_call(
        _combine_kernel,
        out_shape=jax.ShapeDtypeStruct((n, d), F32),
        grid=(n // tc,),
        in_specs=[pl.BlockSpec((TOP_K, SUBROWS, tc, sw), lambda i: (0, 0, i, 0)), tok,
                  pl.BlockSpec((tc, TOP_K), lambda i: (i, 0))],
        out_specs=tok,
        compiler_params=_params(("parallel",)),
        name="combine",
    )(g, h, w)


def _tiles(n, seq):
    def pick(limit, of):
        t = limit
        while of % t:
            t //= 2
        return t
    return dict(
        tm_a=pick(512, seq), tm_f=pick(512, n), tm_qkv=pick(1024, seq), blk=pick(256, seq),
        tm_o=pick(1024, n), tr=pick(1024, TOP_K * n), tc=pick(1024, n))


def kernel(x, a_norm_g, a_w_in, a_v_norm_g, a_w_spatial, a_b_spatial, a_w_out, f_norm_g, f_w_in, f_w_out, kv_norm_g, kv_w, kv_b_f, k_norm_g, b_norm_g, b_w_in, q_norm_g, b_w_out, m_norm_g, m_w_router, m_w_in, m_w_out):
    batch, seq, d = x.shape
    n = batch * seq
    nh = d // HEAD_DIM
    ne = m_w_router.shape[-1]
    assert a_w_in.shape[0] == 1 and b_w_in.shape[0] == 1 and f_w_in.shape[0] == 1 and m_w_in.shape[0] == 1
    assert seq % GMLP_CHUNK == 0 and d % LANES == 0 and (SUBROWS * n) % SC_WINDOW == 0
    t = _tiles(n, seq)
    row = lambda g: g.reshape(1, -1)

    h = x.reshape(n, d)
    h = _mixer_a(h, row(a_norm_g[0]), a_w_in[0].astype(BF16), row(a_v_norm_g[0]), a_w_spatial[0],
                 a_b_spatial[0].T, a_w_out[0].astype(BF16), tm=t["tm_a"])
    h = _swiglu(h, row(f_norm_g[0]), f_w_in[0].astype(BF16), f_w_out[0].astype(BF16), tm=t["tm_f"])

    head = jnp.arange(LANES, dtype=jnp.int32) // HEAD_DIM
    hsum = (head[:, None] == head[None, :]).astype(BF16)
    q, k, vt, sg, fp = _qkv(
        h, row(kv_norm_g), row(b_norm_g[0]), kv_w[:, :d].astype(BF16),
        kv_w[:, d:2 * d].T.astype(BF16), kv_w[:, 2 * d:].T.astype(BF16), kv_b_f.reshape(nh, 1),
        row(jnp.tile(k_norm_g, nh)), row(jnp.tile(q_norm_g[0], nh)), b_w_in[0].astype(BF16), hsum,
        tm=t["tm_qkv"], batch=batch, seq=seq)
    og, (moe_w_in, moe_w_out) = _attention(q, k, vt, jnp.swapaxes(fp, 1, 2), sg,
                                           (m_w_in[0], m_w_out[0]), batch=batch, seq=seq, blk=t["blk"])

    h2, xp, meta, cnt = _out_router(og, h, b_w_out[0].astype(BF16), row(m_norm_g[0]),
                                    m_w_router[0].T, tm=t["tm_o"])

    tr = t["tr"]
    n_tiles = TOP_K * n // tr + ne
    p = n_tiles * tr
    counts = cnt[:, 0].astype(jnp.int32)
    tiles_per_expert = (counts + tr - 1) // tr
    tile_end = jnp.cumsum(tiles_per_expert)
    tile_start = tile_end - tiles_per_expert
    expert_ids = jnp.arange(ne, dtype=jnp.int32)
    idx = meta[0:TOP_K].astype(jnp.int32)
    start_of = jnp.sum(jnp.where(idx[:, :, None] == expert_ids, tile_start * tr, 0), axis=-1)
    dest = start_of + meta[4:4 + TOP_K].astype(jnp.int32)
    tile_ids = jnp.arange(n_tiles, dtype=jnp.int32)
    tile_expert = jnp.minimum(
        jnp.sum((tile_ids[:, None] >= tile_end[None, :]).astype(jnp.int32), axis=1), ne - 1)
    mine = tile_expert[:, None] == expert_ids[None, :]
    tile_rows = jnp.clip(
        jnp.sum(jnp.where(mine, counts - (tile_ids[:, None] - tile_start) * tr, 0), axis=1), 0, tr)
    tile_rows = jnp.where(tile_ids < tile_end[-1], tile_rows, 0).astype(jnp.int32)

    sub = (jnp.arange(SUBROWS, dtype=jnp.int32) * p)[None, :, None]
    sub_dest = sub + dest[:, None, :]
    sw = xp.shape[2]
    xs = _sc_scatter(xp.reshape(SUBROWS * n, sw), sub_dest.reshape(TOP_K, SUBROWS * n), SUBROWS * p)
    eo = _experts(tile_expert, tile_rows, xs.reshape(SUBROWS, p, sw), moe_w_in, moe_w_out,
                  tr=tr, tf=m_w_out.shape[2] // EXPERT_STEPS)
    g = _sc_gather(eo.reshape(SUBROWS * p, sw), sub_dest.reshape(1, TOP_K * SUBROWS * n))
    out = _combine(g.reshape(TOP_K, SUBROWS, n, sw), h2, meta[2:2 + TOP_K].T, tc=t["tc"])
    return out.reshape(batch, seq, d)
```

```python
import functools

import jax
import jax.numpy as jnp
from jax import lax
from jax.experimental import pallas as pl
from jax.experimental.pallas import tpu as pltpu
from jax.experimental.pallas import tpu_sc as plsc

RMS_EPS = 1e-6
NEG_INF = -1e30
LOG2E = 1.4426950408889634
N_PIECES = 3
SC_WINDOW = 128
SUBROWS = 2
EXPERT_STEPS = 2
EXPERT_SUBSLABS = 2
ROW_BLOCK = 256
GMLP_CHUNK = 128
CAUSAL_CHUNK = 64
A_GROUPS = 8
HEAD_DIM = 64
LANES = 128
TOP_K = 2
VMEM_LIMIT = 56 * 1024 * 1024

BF16 = jnp.bfloat16
F32 = jnp.float32


def _dot(a, b, **kw):
    return jnp.dot(a, b, preferred_element_type=F32, **kw)


def _dot_nt(a, b, **kw):
    return lax.dot_general(a, b, (((1,), (1,)), ((), ())), preferred_element_type=F32, **kw)


def _rms(x, g):
    return x * lax.rsqrt(jnp.mean(x * x, axis=-1, keepdims=True) + RMS_EPS) * g


def _gelu(x):
    return 0.5 * x * (1.0 + lax.erf(x * (2.0 ** -0.5)))


def _const_spec(shape):
    nd = len(shape)
    return pl.BlockSpec(shape, lambda *_: (0,) * nd, pipeline_mode=pl.Buffered(1))


def _params(sem):
    return pltpu.CompilerParams(dimension_semantics=sem, vmem_limit_bytes=VMEM_LIMIT)


def _mixer_a_kernel(x_ref, g_ref, win_ref, gv_ref, ws_ref, bs_ref, wout_ref, o_ref, z_ref):
    tm = x_ref.shape[0]
    half = wout_ref.shape[0]
    gd = half // A_GROUPS
    x = x_ref[...]
    xb = _rms(x, g_ref[...]).astype(BF16)
    v = _gelu(_dot(xb, win_ref[:, half:]))
    v = (_rms(v, gv_ref[...])).astype(BF16)
    u = _gelu(_dot(xb, win_ref[:, :half]))
    row = lax.broadcasted_iota(jnp.int32, (GMLP_CHUNK, GMLP_CHUNK), 0)
    col = lax.broadcasted_iota(jnp.int32, (GMLP_CHUNK, GMLP_CHUNK), 1)
    keep = (col // CAUSAL_CHUNK) <= (row // CAUSAL_CHUNK)
    bs = bs_ref[...]
    for g in range(A_GROUPS):
        wg = jnp.where(keep, ws_ref[g], 0.0).astype(BF16)
        bg = bs[:, g:g + 1]
        for c in range(tm // GMLP_CHUNK):
            rs = slice(c * GMLP_CHUNK, (c + 1) * GMLP_CHUNK)
            cs = slice(g * gd, (g + 1) * gd)
            sv = _dot(wg, v[rs, cs]) + bg
            z_ref[rs, cs] = (u[rs, cs] * sv).astype(BF16)
    o_ref[...] = x + _dot(z_ref[...], wout_ref[...])


def _mixer_a(h, g, w_in, gv, ws, bs_t, w_out, *, tm):
    n, d = h.shape
    half = w_out.shape[0]
    return pl.pallas_call(
        _mixer_a_kernel,
        out_shape=jax.ShapeDtypeStruct((n, d), F32),
        grid=(n // tm,),
        in_specs=[
            pl.BlockSpec((tm, d), lambda i: (i, 0)),
            _const_spec(g.shape), _const_spec(w_in.shape), _const_spec(gv.shape),
            _const_spec(ws.shape), _const_spec(bs_t.shape), _const_spec(w_out.shape),
        ],
        out_specs=pl.BlockSpec((tm, d), lambda i: (i, 0)),
        scratch_shapes=[pltpu.VMEM((tm, half), BF16)],
        compiler_params=_params(("parallel",)),
        name="mixer_a",
    )(h, g, w_in, gv, ws, bs_t, w_out)


def _swiglu_kernel(x_ref, g_ref, win_ref, wout_ref, o_ref):
    f = wout_ref.shape[0]
    x = x_ref[...]
    xb = _rms(x, g_ref[...]).astype(BF16)
    a = _dot(xb, win_ref[:, :f])
    b = _dot(xb, win_ref[:, f:])
    hm = (a * jax.nn.sigmoid(a) * b).astype(BF16)
    o_ref[...] = x + _dot(hm, wout_ref[...])


def _swiglu(h, g, w_in, w_out, *, tm):
    n, d = h.shape
    return pl.pallas_call(
        _swiglu_kernel,
        out_shape=jax.ShapeDtypeStruct((n, d), F32),
        grid=(n // tm,),
        in_specs=[
            pl.BlockSpec((tm, d), lambda i: (i, 0)),
            _const_spec(g.shape), _const_spec(w_in.shape), _const_spec(w_out.shape),
        ],
        out_specs=pl.BlockSpec((tm, d), lambda i: (i, 0)),
        compiler_params=_params(("parallel",)),
        name="swiglu",
    )(h, g, w_in, w_out)


def _lane_cumsum(x):
    n = x.shape[-1]
    lane = lax.broadcasted_iota(jnp.int32, x.shape, x.ndim - 1)
    sh = 1
    while sh < n:
        x = x + jnp.where(lane >= sh, pltpu.roll(x, sh, axis=x.ndim - 1), 0.0)
        sh *= 2
    return x


def _split_bf16(x):
    pieces = []
    for _ in range(N_PIECES):
        p = x.astype(BF16)
        pieces.append(p)
        x = x - p.astype(F32)
    return pieces


def _qkv_kernel(x_ref, gkv_ref, gb_ref, wk_ref, wvt_ref, wft_ref, bf_ref, gk_ref, gq_ref, wqg_ref,
                hsum_ref, q_ref, k_ref, vt_ref, sg_ref, fp_ref, carry_ref, *, tiles_per_seq):
    d = x_ref.shape[1]
    i = pl.program_id(0)

    @pl.when(i == 0)
    def _():
        carry_ref[...] = jnp.zeros_like(carry_ref)

    x = x_ref[...]
    y = x * lax.rsqrt(jnp.mean(x * x, axis=-1, keepdims=True) + RMS_EPS)
    skv = (y * gkv_ref[...]).astype(BF16)
    sb = (y * gb_ref[...]).astype(BF16)
    hsum = hsum_ref[...]

    def head_norm(t, gain):
        sq = (t * t).astype(BF16)
        ssq = jnp.concatenate([_dot(sq[:, j * LANES:(j + 1) * LANES], hsum)
                               for j in range(d // LANES)], axis=1)
        return t * lax.rsqrt(ssq * (1.0 / HEAD_DIM) + RMS_EPS) * gain

    f = _dot_nt(wft_ref[...], skv) + bf_ref[...]
    logf = jax.nn.log_sigmoid(f)
    carry = jnp.where(i % tiles_per_seq == 0, 0.0, carry_ref[:, 0:1])
    cum = _lane_cumsum(logf) + carry
    carry_ref[...] = jnp.broadcast_to(cum[:, -1:], carry_ref.shape)
    fp_ref[0] = jnp.concatenate(_split_bf16(cum * LOG2E), axis=0)

    k = _dot(skv, wk_ref[...])
    k_ref[...] = head_norm(k, gk_ref[...]).astype(BF16)
    vt_ref[0] = _dot_nt(wvt_ref[...], skv).astype(BF16)
    q = _dot(sb, wqg_ref[:, :d])
    q_ref[...] = (head_norm(q, gq_ref[...]) * (LOG2E * HEAD_DIM ** -0.5)).astype(BF16)
    sg_ref[...] = jax.nn.sigmoid(_dot(sb, wqg_ref[:, d:])).astype(BF16)


def _qkv(h, gkv, gb, wk, wvt, wft, bf, gk, gq, wqg, hsum, *, tm, batch, seq):
    n, d = h.shape
    nh = wft.shape[0]
    tps = seq // tm
    tok = pl.BlockSpec((tm, d), lambda i: (i, 0))
    tok_bf = jax.ShapeDtypeStruct((n, d), BF16)
    seq_map = lambda i: (i // tps, 0, i % tps)
    consts = (gkv, gb, wk, wvt, wft, bf, gk, gq, wqg, hsum)
    return pl.pallas_call(
        functools.partial(_qkv_kernel, tiles_per_seq=tps),
        out_shape=[tok_bf, tok_bf, jax.ShapeDtypeStruct((batch, d, seq), BF16), tok_bf,
                   jax.ShapeDtypeStruct((batch, N_PIECES * nh, seq), BF16)],
        grid=(n // tm,),
        in_specs=[tok] + [_const_spec(a.shape) for a in consts],
        out_specs=[tok, tok, pl.BlockSpec((1, d, tm), seq_map), tok,
                   pl.BlockSpec((1, N_PIECES * nh, tm), seq_map)],
        scratch_shapes=[pltpu.VMEM((nh, LANES), F32)],
        compiler_params=_params(("arbitrary",)),
        name="qkv",
    )(h, *consts)


def _attention_kernel(q_ref, k_ref, vt_ref, fp_ref, sg_ref, *rest, blk, nq, n_riders):
    rider_in, (o_ref, *rider_out) = rest[:n_riders], rest[n_riders:2 * n_riders + 1]
    kaug_ref, qaug_ref, vsum_ref = rest[2 * n_riders + 1:]
    for w_ref, c_ref in zip(rider_in, rider_out):
        c_ref[...] = w_ref[...].astype(BF16)
    hp = pl.program_id(1)
    nh = fp_ref.shape[2] // N_PIECES
    lane = lax.broadcasted_iota(jnp.int32, (1, LANES), 1)
    own = (lane < HEAD_DIM, lane >= HEAD_DIM)
    spare = (HEAD_DIM, 0)
    q_lanes = [(lane >= s) & (lane < s + N_PIECES) for s in spare]
    k_lanes = [(lane >= s + N_PIECES) & (lane < s + 2 * N_PIECES) for s in spare]

    er = lax.broadcasted_iota(jnp.int32, (N_PIECES * nh, LANES), 0)
    ec = lax.broadcasted_iota(jnp.int32, (N_PIECES * nh, LANES), 1)
    place = jnp.zeros((N_PIECES * nh, LANES), F32)
    for h in range(2):
        for j in range(N_PIECES):
            src = er == j * nh + 2 * hp + h
            place = (place + (src & (ec == spare[h] + j)).astype(F32)
                     - (src & (ec == spare[h] + N_PIECES + j)).astype(F32))
    g = _dot(fp_ref[0], place.astype(BF16))
    for h in range(2):
        k_fill = jnp.where(k_lanes[h], g, q_lanes[h].astype(F32)).astype(BF16)
        kaug_ref[h] = jnp.where(own[h], k_ref[...], k_fill)
        q_fill = jnp.where(q_lanes[h], g, k_lanes[h].astype(F32)).astype(BF16)
        qaug_ref[h] = jnp.where(own[h], q_ref[...], q_fill)

    first_rows = lax.broadcasted_iota(jnp.int32, (LANES, 1), 0) < HEAD_DIM
    vsum_ref[0] = jnp.where(first_rows, vt_ref[0], jnp.ones_like(vt_ref[0]))
    vsum_ref[1] = jnp.where(first_rows, jnp.ones_like(vt_ref[0]), vt_ref[0])
    key = lax.broadcasted_iota(jnp.int32, (blk, blk), 0)
    qry = lax.broadcasted_iota(jnp.int32, (blk, blk), 1)
    causal = qry >= key

    def qk(i):
        lo, hi = i * blk, (i + 1) * blk
        return [_dot_nt(kaug_ref[h, 0:hi, :], qaug_ref[h, lo:hi, :]) for h in range(2)]

    scores = qk(0)
    for i in range(nq):
        lo, hi = i * blk, (i + 1) * blk
        next_scores = qk(i + 1) if i + 1 < nq else None
        probs = []
        for s in scores:
            diag = jnp.where(causal, s[lo:hi], NEG_INF)
            s = diag if i == 0 else jnp.concatenate([s[0:lo], diag], axis=0)
            probs.append(jnp.exp2(s - jnp.max(s, axis=0, keepdims=True)).astype(BF16))
        outs = [_dot(vsum_ref[h, :, 0:hi], probs[h]) for h in range(2)]
        o0 = outs[0] / outs[0][HEAD_DIM:HEAD_DIM + 1, :]
        o1 = outs[1] / outs[1][0:1, :]
        o = jnp.where(first_rows, o0, o1).T
        o_ref[lo:hi, :] = (o * sg_ref[lo:hi, :].astype(F32)).astype(BF16)
        scores = next_scores


def _attention(q, k, vt, fp, sg, riders, *, batch, seq, blk):
    n, d = q.shape
    n_pairs = d // LANES
    steps = batch * n_pairs
    pair = pl.BlockSpec((seq, LANES), lambda b, hp: (b, hp))
    slabs = [w.reshape(steps, -1, w.shape[-1]) for w in riders]
    slab_specs = [pl.BlockSpec((1,) + s.shape[1:], lambda b, hp: (b * n_pairs + hp, 0, 0))
                  for s in slabs]
    out = pl.pallas_call(
        functools.partial(_attention_kernel, blk=blk, nq=seq // blk, n_riders=len(riders)),
        out_shape=[jax.ShapeDtypeStruct((n, d), BF16)]
        + [jax.ShapeDtypeStruct(s.shape, BF16) for s in slabs],
        grid=(batch, n_pairs),
        in_specs=[
            pair, pair,
            pl.BlockSpec((1, LANES, seq), lambda b, hp: (b, hp, 0)),
            pl.BlockSpec((1, seq, fp.shape[2]), lambda b, hp: (b, 0, 0)),
            pair,
        ] + slab_specs,
        out_specs=[pair] + slab_specs,
        scratch_shapes=[pltpu.VMEM((2, seq, LANES), BF16)] * 2 + [pltpu.VMEM((2, LANES, seq), BF16)],
        compiler_params=_params(("parallel", "parallel")),
        name="attention",
    )(q, k, vt, fp, sg, *slabs)
    return out[0], [c.reshape(w.shape) for c, w in zip(out[1:], riders)]


def _pack_bf16_pair(lo, hi):
    ulo = lax.bitcast_convert_type(lo.astype(BF16).astype(F32), jnp.uint32)
    uhi = lax.bitcast_convert_type(hi.astype(BF16).astype(F32), jnp.uint32)
    return (ulo >> 16) | uhi


def _pack_row(x):
    half = x.shape[1] // 2
    w = _pack_bf16_pair(x[:, :half], x[:, half:])
    sw = half // SUBROWS
    return [w[:, c * sw:(c + 1) * sw] for c in range(SUBROWS)]


def _unpack_row(subrows):
    lo = [lax.bitcast_convert_type(p << 16, F32) for p in subrows]
    hi = [lax.bitcast_convert_type(p & jnp.uint32(0xFFFF0000), F32) for p in subrows]
    return jnp.concatenate(lo + hi, axis=1)


def _out_router_kernel(og_ref, h_ref, wo_ref, gm_ref, wrt_ref, h2_ref, xp_ref, meta_ref, cnt_ref,
                       carry_ref):
    i = pl.program_id(0)
    ne, tm = meta_ref.shape
    h2 = h_ref[...] + _dot(og_ref[...], wo_ref[...])
    h2_ref[...] = h2
    xn = _rms(h2, gm_ref[...])
    for c, sub in enumerate(_pack_row(xn)):
        xp_ref[c] = sub
    xh = xn.astype(BF16)
    xl = (xn - xh.astype(F32)).astype(BF16)
    wr = wrt_ref[...]
    wh = wr.astype(BF16)
    wl = (wr - wh.astype(F32)).astype(BF16)
    logits = _dot_nt(wh, xh) + (_dot_nt(wh, xl) + _dot_nt(wl, xh))
    row = lax.broadcasted_iota(jnp.int32, (ne, tm), 0).astype(F32)
    v1 = jnp.max(logits, axis=0, keepdims=True)
    i1 = jnp.min(jnp.where(logits == v1, row, ne), axis=0, keepdims=True)
    rest = jnp.where(row == i1, -jnp.inf, logits)
    v2 = jnp.max(rest, axis=0, keepdims=True)
    i2 = jnp.min(jnp.where(rest == v2, row, ne), axis=0, keepdims=True)
    e = jnp.exp(v2 - v1)
    w1 = 1.0 / (1.0 + e)
    w2 = e / (1.0 + e)
    sel1 = row == i1
    sel2 = row == i2
    oh = (sel1 | sel2).astype(F32)

    @pl.when(i == 0)
    def _():
        carry_ref[...] = jnp.zeros_like(carry_ref)

    base = carry_ref[:, 0:1]
    rank = base + (_lane_cumsum(oh) - oh)
    r1 = jnp.sum(jnp.where(sel1, rank, 0.0), axis=0, keepdims=True)
    r2 = jnp.sum(jnp.where(sel2, rank, 0.0), axis=0, keepdims=True)
    total = base + jnp.sum(oh, axis=1, keepdims=True)
    carry_ref[...] = jnp.broadcast_to(total, carry_ref.shape)
    cnt_ref[...] = jnp.broadcast_to(total, cnt_ref.shape)
    zeros = jnp.zeros_like(w1)
    meta_ref[...] = jnp.concatenate(
        [i1.astype(F32), i2.astype(F32), w1, w2, r1, r2, zeros, zeros], axis=0)


def _out_router(og, h, wo, gm, wrt, *, tm):
    n, d = h.shape
    ne = wrt.shape[0]
    sw = d // 2 // SUBROWS
    tok = pl.BlockSpec((tm, d), lambda i: (i, 0))
    return pl.pallas_call(
        _out_router_kernel,
        out_shape=[jax.ShapeDtypeStruct((n, d), F32), jax.ShapeDtypeStruct((SUBROWS, n, sw), jnp.uint32),
                   jax.ShapeDtypeStruct((ne, n), F32), jax.ShapeDtypeStruct((ne, LANES), F32)],
        grid=(n // tm,),
        in_specs=[tok, tok, _const_spec(wo.shape), _const_spec(gm.shape), _const_spec(wrt.shape)],
        out_specs=[tok, pl.BlockSpec((SUBROWS, tm, sw), lambda i: (0, i, 0)),
                   pl.BlockSpec((ne, tm), lambda i: (0, i)),
                   pl.BlockSpec((ne, LANES), lambda i: (0, 0))],
        scratch_shapes=[pltpu.VMEM((ne, LANES), F32)],
        compiler_params=_params(("arbitrary",)),
        name="out_router",
    )(og, h, wo, gm, wrt)


def _sc_mesh():
    return plsc.VectorSubcoreMesh(core_axis_name="core", subcore_axis_name="subcore")


def _sc_scatter(x, idx, out_rows):
    rows, w = x.shape
    n_idx = idx.shape[0]

    @pl.kernel(out_type=jax.ShapeDtypeStruct((out_rows, w), x.dtype), mesh=_sc_mesh(),
               scratch_types=[])
    def scatter_rows(x_hbm, i_hbm, o_hbm):
        def body(x_vmem, i_vmem):
            for s in range(n_idx):
                pltpu.sync_copy(x_vmem, o_hbm.at[i_vmem.at[s]])

        pltpu.emit_pipeline(
            body, grid=(rows // SC_WINDOW,),
            in_specs=[pl.BlockSpec((SC_WINDOW, w), lambda i: (i, 0)),
                      pl.BlockSpec((n_idx, SC_WINDOW), lambda i: (0, i))],
            out_specs=[],
            core_axis_name=("core", "subcore"),
            dimension_semantics=(pltpu.PARALLEL,),
        )(x_hbm, i_hbm)

    return scatter_rows(x, idx)


def _sc_gather(x, idx):
    n = idx.shape[1]
    w = x.shape[1]

    @pl.kernel(out_type=jax.ShapeDtypeStruct((n, w), x.dtype), mesh=_sc_mesh(), scratch_types=[])
    def gather_rows(x_hbm, i_hbm, o_hbm):
        def body(i_vmem, o_vmem):
            pltpu.sync_copy(x_hbm.at[i_vmem.at[0]], o_vmem)

        pltpu.emit_pipeline(
            body, grid=(n // SC_WINDOW,),
            in_specs=[pl.BlockSpec((1, SC_WINDOW), lambda i: (0, i))],
            out_specs=[pl.BlockSpec((SC_WINDOW, w), lambda i: (i, 0))],
            core_axis_name=("core", "subcore"),
            dimension_semantics=(pltpu.PARALLEL,),
        )(i_hbm, o_hbm)

    return gather_rows(x, idx)


def _experts_kernel(te_ref, rows_ref, x_ref, wa_ref, wb_ref, wo_ref, o_ref, xb_ref, hm_ref, *, nf):
    del te_ref
    i = pl.program_id(0)
    f = pl.program_id(1)
    tr = xb_ref.shape[0]
    tf = wa_ref.shape[2]
    n_valid = rows_ref[i]

    @pl.when((n_valid > 0) & (f == 0))
    def _():
        x = _unpack_row([x_ref[c] for c in range(SUBROWS)])
        live = lax.broadcasted_iota(jnp.int32, (tr, 1), 0) < n_valid
        xb_ref[...] = jnp.where(live, x, 0.0).astype(BF16)

    ts = tf // EXPERT_SUBSLABS
    full = n_valid == tr

    def hidden(j, rows):
        xb = xb_ref[rows, :]
        for c in range(EXPERT_SUBSLABS):
            a = _dot(xb, wa_ref[0, :, c * ts:(c + 1) * ts])
            b = _dot(xb, wb_ref[0, :, c * ts:(c + 1) * ts])
            col = j * tf + c * ts
            hm_ref[rows, col:col + ts] = (a * jax.nn.sigmoid(a) * b).astype(BF16)

    def project(rows):
        for c, sub in enumerate(_pack_row(_dot(hm_ref[rows, :], wo_ref[0]))):
            o_ref[c, rows, :] = sub

    def row_blocks(body):
        def step(r, carry):
            body(pl.ds(pl.multiple_of(r * ROW_BLOCK, ROW_BLOCK), ROW_BLOCK))
            return carry
        lax.fori_loop(0, (n_valid + ROW_BLOCK - 1) // ROW_BLOCK, step, 0)

    for j in range(nf):
        last = j == nf - 1

        @pl.when(full & (f == j))
        def _(j=j, last=last):
            hidden(j, slice(None))
            if last:
                project(slice(None))

        @pl.when(jnp.logical_not(full) & (n_valid > 0) & (f == j))
        def _(j=j, last=last):
            row_blocks(functools.partial(hidden, j))
            if last:
                row_blocks(project)

    @pl.when(jnp.logical_not(full) & (f == 0))
    def _():
        o_ref[...] = jnp.zeros_like(o_ref)


def _experts(tile_expert, tile_rows, xs, w_in, w_out, *, tr, tf):
    _, p, sw = xs.shape
    d = w_out.shape[2]
    de = w_out.shape[1]
    nf = de // tf
    rows = pl.BlockSpec((SUBROWS, tr, sw), lambda i, f, te, tv: (0, i, 0))
    return pl.pallas_call(
        functools.partial(_experts_kernel, nf=nf),
        out_shape=jax.ShapeDtypeStruct((SUBROWS, p, sw), jnp.uint32),
        grid_spec=pltpu.PrefetchScalarGridSpec(
            num_scalar_prefetch=2,
            grid=(p // tr, nf),
            in_specs=[
                rows,
                pl.BlockSpec((1, d, tf), lambda i, f, te, tv: (te[i], 0, f)),
                pl.BlockSpec((1, d, tf), lambda i, f, te, tv: (te[i], 0, nf + f)),
                pl.BlockSpec((1, de, d), lambda i, f, te, tv: (te[i], 0, 0)),
            ],
            out_specs=rows,
            scratch_shapes=[pltpu.VMEM((tr, d), BF16), pltpu.VMEM((tr, de), BF16)],
        ),
        compiler_params=_params(("arbitrary", "arbitrary")),
        name="experts",
    )(tile_expert, tile_rows, xs, w_in, w_in, w_out)


def _combine_kernel(g_ref, h_ref, w_ref, o_ref):
    w = w_ref[...]
    y = [_unpack_row([g_ref[s, c] for c in range(SUBROWS)]) for s in range(TOP_K)]
    o_ref[...] = h_ref[...] + (w[:, 0:1] * y[0] + w[:, 1:2] * y[1])


def _combine(g, h, w, *, tc):
    n, d = h.shape
    sw = g.shape[3]
    tok = pl.BlockSpec((tc, d), lambda i: (i, 0))
    return pl.pallas_call(
        _combine_kernel,
        out_shape=jax.ShapeDtypeStruct((n, d), F32),
        grid=(n // tc,),
        in_specs=[pl.BlockSpec((TOP_K, SUBROWS, tc, sw), lambda i: (0, 0, i, 0)), tok,
                  pl.BlockSpec((tc, TOP_K), lambda i: (i, 0))],
        out_specs=tok,
        compiler_params=_params(("parallel",)),
        name="combine",
    )(g, h, w)


def _tiles(n, seq):
    def pick(limit, of):
        t = limit
        while of % t:
            t //= 2
        return t
    return dict(
        tm_a=pick(512, seq), tm_f=pick(512, n), tm_qkv=pick(1024, seq), blk=pick(256, seq),
        tm_o=pick(1024, n), tr=pick(1024, TOP_K * n), tc=pick(1024, n))


def kernel(x, a_norm_g, a_w_in, a_v_norm_g, a_w_spatial, a_b_spatial, a_w_out, f_norm_g, f_w_in, f_w_out, kv_norm_g, kv_w, kv_b_f, k_norm_g, b_norm_g, b_w_in, q_norm_g, b_w_out, m_norm_g, m_w_router, m_w_in, m_w_out):
    batch, seq, d = x.shape
    n = batch * seq
    nh = d // HEAD_DIM
    ne = m_w_router.shape[-1]
    assert a_w_in.shape[0] == 1 and b_w_in.shape[0] == 1 and f_w_in.shape[0] == 1 and m_w_in.shape[0] == 1
    assert seq % GMLP_CHUNK == 0 and d % LANES == 0 and (SUBROWS * n) % SC_WINDOW == 0
    t = _tiles(n, seq)
    row = lambda g: g.reshape(1, -1)

    h = x.reshape(n, d)
    h = _mixer_a(h, row(a_norm_g[0]), a_w_in[0].astype(BF16), row(a_v_norm_g[0]), a_w_spatial[0],
                 a_b_spatial[0].T, a_w_out[0].astype(BF16), tm=t["tm_a"])
    h = _swiglu(h, row(f_norm_g[0]), f_w_in[0].astype(BF16), f_w_out[0].astype(BF16), tm=t["tm_f"])

    head = jnp.arange(LANES, dtype=jnp.int32) // HEAD_DIM
    hsum = (head[:, None] == head[None, :]).astype(BF16)
    q, k, vt, sg, fp = _qkv(
        h, row(kv_norm_g), row(b_norm_g[0]), kv_w[:, :d].astype(BF16),
        kv_w[:, d:2 * d].T.astype(BF16), kv_w[:, 2 * d:].T.astype(BF16), kv_b_f.reshape(nh, 1),
        row(jnp.tile(k_norm_g, nh)), row(jnp.tile(q_norm_g[0], nh)), b_w_in[0].astype(BF16), hsum,
        tm=t["tm_qkv"], batch=batch, seq=seq)
    og, (moe_w_in, moe_w_out) = _attention(q, k, vt, jnp.swapaxes(fp, 1, 2), sg,
                                           (m_w_in[0], m_w_out[0]), batch=batch, seq=seq, blk=t["blk"])

    h2, xp, meta, cnt = _out_router(og, h, b_w_out[0].astype(BF16), row(m_norm_g[0]),
                                    m_w_router[0].T, tm=t["tm_o"])

    tr = t["tr"]
    n_tiles = TOP_K * n // tr + ne
    p = n_tiles * tr
    counts = cnt[:, 0].astype(jnp.int32)
    tiles_per_expert = (counts + tr - 1) // tr
    tile_end = jnp.cumsum(tiles_per_expert)
    tile_start = tile_end - tiles_per_expert
    expert_ids = jnp.arange(ne, dtype=jnp.int32)
    idx = meta[0:TOP_K].astype(jnp.int32)
    start_of = jnp.sum(jnp.where(idx[:, :, None] == expert_ids, tile_start * tr, 0), axis=-1)
    dest = start_of + meta[4:4 + TOP_K].astype(jnp.int32)
    tile_ids = jnp.arange(n_tiles, dtype=jnp.int32)
    tile_expert = jnp.minimum(
        jnp.sum((tile_ids[:, None] >= tile_end[None, :]).astype(jnp.int32), axis=1), ne - 1)
    mine = tile_expert[:, None] == expert_ids[None, :]
    tile_rows = jnp.clip(
        jnp.sum(jnp.where(mine, counts - (tile_ids[:, None] - tile_start) * tr, 0), axis=1), 0, tr)
    tile_rows = jnp.where(tile_ids < tile_end[-1], tile_rows, 0).astype(jnp.int32)

    sub = (jnp.arange(SUBROWS, dtype=jnp.int32) * p)[None, :, None]
    sub_dest = sub + dest[:, None, :]
    sw = xp.shape[2]
    xs = _sc_scatter(xp.reshape(SUBROWS * n, sw), sub_dest.reshape(TOP_K, SUBROWS * n), SUBROWS * p)
    eo = _experts(tile_expert, tile_rows, xs.reshape(SUBROWS, p, sw), moe_w_in, moe_w_out,
                  tr=tr, tf=m_w_out.shape[2] // EXPERT_STEPS)
    g = _sc_gather(eo.reshape(SUBROWS * p, sw), sub_dest.reshape(1, TOP_K * SUBROWS * n))
    out = _combine(g.reshape(TOP_K, SUBROWS, n, sw), h2, meta[2:2 + TOP_K].T, tc=t["tc"])
    return out.reshape(batch, seq, d)
```

```python
import functools

import jax
import jax.numpy as jnp
from jax import lax
from jax.experimental import pallas as pl
from jax.experimental.pallas import tpu as pltpu
from jax.experimental.pallas import tpu_sc as plsc

RMS_EPS = 1e-6
NEG_INF = -1e30
LOG2E = 1.4426950408889634
N_PIECES = 3
SC_WINDOW = 128
SUBROWS = 2
EXPERT_STEPS = 2
EXPERT_SUBSLABS = 2
ROW_BLOCK = 256
GMLP_CHUNK = 128
CAUSAL_CHUNK = 64
A_GROUPS = 8
HEAD_DIM = 64
LANES = 128
TOP_K = 2
VMEM_LIMIT = 56 * 1024 * 1024

BF16 = jnp.bfloat16
F32 = jnp.float32


def _dot(a, b, **kw):
    return jnp.dot(a, b, preferred_element_type=F32, **kw)


def _dot_nt(a, b, **kw):
    return lax.dot_general(a, b, (((1,), (1,)), ((), ())), preferred_element_type=F32, **kw)


def _rms(x, g):
    return x * lax.rsqrt(jnp.mean(x * x, axis=-1, keepdims=True) + RMS_EPS) * g


def _gelu(x):
    return 0.5 * x * (1.0 + lax.erf(x * (2.0 ** -0.5)))


def _const_spec(shape):
    nd = len(shape)
    return pl.BlockSpec(shape, lambda *_: (0,) * nd, pipeline_mode=pl.Buffered(1))


def _params(sem):
    return pltpu.CompilerParams(dimension_semantics=sem, vmem_limit_bytes=VMEM_LIMIT)


def _mixer_a_kernel(x_ref, g_ref, win_ref, gv_ref, ws_ref, bs_ref, wout_ref, o_ref, z_ref):
    tm = x_ref.shape[0]
    half = wout_ref.shape[0]
    gd = half // A_GROUPS
    x = x_ref[...]
    xb = _rms(x, g_ref[...]).astype(BF16)
    v = _gelu(_dot(xb, win_ref[:, half:]))
    v = (_rms(v, gv_ref[...])).astype(BF16)
    u = _gelu(_dot(xb, win_ref[:, :half]))
    row = lax.broadcasted_iota(jnp.int32, (GMLP_CHUNK, GMLP_CHUNK), 0)
    col = lax.broadcasted_iota(jnp.int32, (GMLP_CHUNK, GMLP_CHUNK), 1)
    keep = (col // CAUSAL_CHUNK) <= (row // CAUSAL_CHUNK)
    bs = bs_ref[...]
    for g in range(A_GROUPS):
        wg = jnp.where(keep, ws_ref[g], 0.0).astype(BF16)
        bg = bs[:, g:g + 1]
        for c in range(tm // GMLP_CHUNK):
            rs = slice(c * GMLP_CHUNK, (c + 1) * GMLP_CHUNK)
            cs = slice(g * gd, (g + 1) * gd)
            sv = _dot(wg, v[rs, cs]) + bg
            z_ref[rs, cs] = (u[rs, cs] * sv).astype(BF16)
    o_ref[...] = x + _dot(z_ref[...], wout_ref[...])


def _mixer_a(h, g, w_in, gv, ws, bs_t, w_out, *, tm):
    n, d = h.shape
    half = w_out.shape[0]
    return pl.pallas_call(
        _mixer_a_kernel,
        out_shape=jax.ShapeDtypeStruct((n, d), F32),
        grid=(n // tm,),
        in_specs=[
            pl.BlockSpec((tm, d), lambda i: (i, 0)),
            _const_spec(g.shape), _const_spec(w_in.shape), _const_spec(gv.shape),
            _const_spec(ws.shape), _const_spec(bs_t.shape), _const_spec(w_out.shape),
        ],
        out_specs=pl.BlockSpec((tm, d), lambda i: (i, 0)),
        scratch_shapes=[pltpu.VMEM((tm, half), BF16)],
        compiler_params=_params(("parallel",)),
        name="mixer_a",
    )(h, g, w_in, gv, ws, bs_t, w_out)


def _swiglu_kernel(x_ref, g_ref, win_ref, wout_ref, o_ref):
    f = wout_ref.shape[0]
    x = x_ref[...]
    xb = _rms(x, g_ref[...]).astype(BF16)
    a = _dot(xb, win_ref[:, :f])
    b = _dot(xb, win_ref[:, f:])
    hm = (a * jax.nn.sigmoid(a) * b).astype(BF16)
    o_ref[...] = x + _dot(hm, wout_ref[...])


def _swiglu(h, g, w_in, w_out, *, tm):
    n, d = h.shape
    return pl.pallas_call(
        _swiglu_kernel,
        out_shape=jax.ShapeDtypeStruct((n, d), F32),
        grid=(n // tm,),
        in_specs=[
            pl.BlockSpec((tm, d), lambda i: (i, 0)),
            _const_spec(g.shape), _const_spec(w_in.shape), _const_spec(w_out.shape),
        ],
        out_specs=pl.BlockSpec((tm, d), lambda i: (i, 0)),
        compiler_params=_params(("parallel",)),
        name="swiglu",
    )(h, g, w_in, w_out)


def _lane_cumsum(x):
    n = x.shape[-1]
    lane = lax.broadcasted_iota(jnp.int32, x.shape, x.ndim - 1)
    sh = 1
    while sh < n:
        x = x + jnp.where(lane >= sh, pltpu.roll(x, sh, axis=x.ndim - 1), 0.0)
        sh *= 2
    return x


def _split_bf16(x):
    pieces = []
    for _ in range(N_PIECES):
        p = x.astype(BF16)
        pieces.append(p)
        x = x - p.astype(F32)
    return pieces


def _qkv_kernel(x_ref, gkv_ref, gb_ref, wk_ref, wvt_ref, wft_ref, bf_ref, gk_ref, gq_ref, wqg_ref,
                hsum_ref, q_ref, k_ref, vt_ref, sg_ref, fp_ref, carry_ref, *, tiles_per_seq):
    d = x_ref.shape[1]
    i = pl.program_id(0)

    @pl.when(i == 0)
    def _():
        carry_ref[...] = jnp.zeros_like(carry_ref)

    x = x_ref[...]
    y = x * lax.rsqrt(jnp.mean(x * x, axis=-1, keepdims=True) + RMS_EPS)
    skv = (y * gkv_ref[...]).astype(BF16)
    sb = (y * gb_ref[...]).astype(BF16)
    hsum = hsum_ref[...]

    def head_norm(t, gain):
        sq = (t * t).astype(BF16)
        ssq = jnp.concatenate([_dot(sq[:, j * LANES:(j + 1) * LANES], hsum)
                               for j in range(d // LANES)], axis=1)
        return t * lax.rsqrt(ssq * (1.0 / HEAD_DIM) + RMS_EPS) * gain

    f = _dot_nt(wft_ref[...], skv) + bf_ref[...]
    logf = jax.nn.log_sigmoid(f)
    carry = jnp.where(i % tiles_per_seq == 0, 0.0, carry_ref[:, 0:1])
    cum = _lane_cumsum(logf) + carry
    carry_ref[...] = jnp.broadcast_to(cum[:, -1:], carry_ref.shape)
    fp_ref[0] = jnp.concatenate(_split_bf16(cum * LOG2E), axis=0)

    k = _dot(skv, wk_ref[...])
    k_ref[...] = head_norm(k, gk_ref[...]).astype(BF16)
    vt_ref[0] = _dot_nt(wvt_ref[...], skv).astype(BF16)
    q = _dot(sb, wqg_ref[:, :d])
    q_ref[...] = (head_norm(q, gq_ref[...]) * (LOG2E * HEAD_DIM ** -0.5)).astype(BF16)
    sg_ref[...] = jax.nn.sigmoid(_dot(sb, wqg_ref[:, d:])).astype(BF16)


def _qkv(h, gkv, gb, wk, wvt, wft, bf, gk, gq, wqg, hsum, *, tm, batch, seq):
    n, d = h.shape
    nh = wft.shape[0]
    tps = seq // tm
    tok = pl.BlockSpec((tm, d), lambda i: (i, 0))
    tok_bf = jax.ShapeDtypeStruct((n, d), BF16)
    seq_map = lambda i: (i // tps, 0, i % tps)
    consts = (gkv, gb, wk, wvt, wft, bf, gk, gq, wqg, hsum)
    return pl.pallas_call(
        functools.partial(_qkv_kernel, tiles_per_seq=tps),
        out_shape=[tok_bf, tok_bf, jax.ShapeDtypeStruct((batch, d, seq), BF16), tok_bf,
                   jax.ShapeDtypeStruct((batch, N_PIECES * nh, seq), BF16)],
        grid=(n // tm,),
        in_specs=[tok] + [_const_spec(a.shape) for a in consts],
        out_specs=[tok, tok, pl.BlockSpec((1, d, tm), seq_map), tok,
                   pl.BlockSpec((1, N_PIECES * nh, tm), seq_map)],
        scratch_shapes=[pltpu.VMEM((nh, LANES), F32)],
        compiler_params=_params(("arbitrary",)),
        name="qkv",
    )(h, *consts)


def _attention_kernel(q_ref, k_ref, vt_ref, fp_ref, sg_ref, *rest, blk, nq, n_riders):
    rider_in, (o_ref, *rider_out) = rest[:n_riders], rest[n_riders:2 * n_riders + 1]
    kaug_ref, qaug_ref, vsum_ref = rest[2 * n_riders + 1:]
    for w_ref, c_ref in zip(rider_in, rider_out):
        c_ref[...] = w_ref[...].astype(BF16)
    hp = pl.program_id(1)
    nh = fp_ref.shape[2] // N_PIECES
    lane = lax.broadcasted_iota(jnp.int32, (1, LANES), 1)
    own = (lane < HEAD_DIM, lane >= HEAD_DIM)
    spare = (HEAD_DIM, 0)
    q_lanes = [(lane >= s) & (lane < s + N_PIECES) for s in spare]
    k_lanes = [(lane >= s + N_PIECES) & (lane < s + 2 * N_PIECES) for s in spare]

    er = lax.broadcasted_iota(jnp.int32, (N_PIECES * nh, LANES), 0)
    ec = lax.broadcasted_iota(jnp.int32, (N_PIECES * nh, LANES), 1)
    place = jnp.zeros((N_PIECES * nh, LANES), F32)
    for h in range(2):
        for j in range(N_PIECES):
            src = er == j * nh + 2 * hp + h
            place = (place + (src & (ec == spare[h] + j)).astype(F32)
                     - (src & (ec == spare[h] + N_PIECES + j)).astype(F32))
    g = _dot(fp_ref[0], place.astype(BF16))
    for h in range(2):
        k_fill = jnp.where(k_lanes[h], g, q_lanes[h].astype(F32)).astype(BF16)
        kaug_ref[h] = jnp.where(own[h], k_ref[...], k_fill)
        q_fill = jnp.where(q_lanes[h], g, k_lanes[h].astype(F32)).astype(BF16)
        qaug_ref[h] = jnp.where(own[h], q_ref[...], q_fill)

    first_rows = lax.broadcasted_iota(jnp.int32, (LANES, 1), 0) < HEAD_DIM
    vsum_ref[0] = jnp.where(first_rows, vt_ref[0], jnp.ones_like(vt_ref[0]))
    vsum_ref[1] = jnp.where(first_rows, jnp.ones_like(vt_ref[0]), vt_ref[0])
    key = lax.broadcasted_iota(jnp.int32, (blk, blk), 0)
    qry = lax.broadcasted_iota(jnp.int32, (blk, blk), 1)
    causal = qry >= key

    def qk(i):
        lo, hi = i * blk, (i + 1) * blk
        return [_dot_nt(kaug_ref[h, 0:hi, :], qaug_ref[h, lo:hi, :]) for h in range(2)]

    scores = qk(0)
    for i in range(nq):
        lo, hi = i * blk, (i + 1) * blk
        next_scores = qk(i + 1) if i + 1 < nq else None
        probs = []
        for s in scores:
            diag = jnp.where(causal, s[lo:hi], NEG_INF)
            s = diag if i == 0 else jnp.concatenate([s[0:lo], diag], axis=0)
            probs.append(jnp.exp2(s - jnp.max(s, axis=0, keepdims=True)).astype(BF16))
        outs = [_dot(vsum_ref[h, :, 0:hi], probs[h]) for h in range(2)]
        o0 = outs[0] / outs[0][HEAD_DIM:HEAD_DIM + 1, :]
        o1 = outs[1] / outs[1][0:1, :]
        o = jnp.where(first_rows, o0, o1).T
        o_ref[lo:hi, :] = (o * sg_ref[lo:hi, :].astype(F32)).astype(BF16)
        scores = next_scores


def _attention(q, k, vt, fp, sg, riders, *, batch, seq, blk):
    n, d = q.shape
    n_pairs = d // LANES
    steps = batch * n_pairs
    pair = pl.BlockSpec((seq, LANES), lambda b, hp: (b, hp))
    slabs = [w.reshape(steps, -1, w.shape[-1]) for w in riders]
    slab_specs = [pl.BlockSpec((1,) + s.shape[1:], lambda b, hp: (b * n_pairs + hp, 0, 0))
                  for s in slabs]
    out = pl.pallas_call(
        functools.partial(_attention_kernel, blk=blk, nq=seq // blk, n_riders=len(riders)),
        out_shape=[jax.ShapeDtypeStruct((n, d), BF16)]
        + [jax.ShapeDtypeStruct(s.shape, BF16) for s in slabs],
        grid=(batch, n_pairs),
        in_specs=[
            pair, pair,
            pl.BlockSpec((1, LANES, seq), lambda b, hp: (b, hp, 0)),
            pl.BlockSpec((1, seq, fp.shape[2]), lambda b, hp: (b, 0, 0)),
            pair,
        ] + slab_specs,
        out_specs=[pair] + slab_specs,
        scratch_shapes=[pltpu.VMEM((2, seq, LANES), BF16)] * 2 + [pltpu.VMEM((2, LANES, seq), BF16)],
        compiler_params=_params(("parallel", "parallel")),
        name="attention",
    )(q, k, vt, fp, sg, *slabs)
    return out[0], [c.reshape(w.shape) for c, w in zip(out[1:], riders)]


def _pack_bf16_pair(lo, hi):
    ulo = lax.bitcast_convert_type(lo.astype(BF16).astype(F32), jnp.uint32)
    uhi = lax.bitcast_convert_type(hi.astype(BF16).astype(F32), jnp.uint32)
    return (ulo >> 16) | uhi


def _pack_row(x):
    half = x.shape[1] // 2
    w = _pack_bf16_pair(x[:, :half], x[:, half:])
    sw = half // SUBROWS
    return [w[:, c * sw:(c + 1) * sw] for c in range(SUBROWS)]


def _unpack_row(subrows):
    lo = [lax.bitcast_convert_type(p << 16, F32) for p in subrows]
    hi = [lax.bitcast_convert_type(p & jnp.uint32(0xFFFF0000), F32) for p in subrows]
    return jnp.concatenate(lo + hi, axis=1)


def _out_router_kernel(og_ref, h_ref, wo_ref, gm_ref, wrt_ref, h2_ref, xp_ref, meta_ref, cnt_ref,
                       carry_ref):
    i = pl.program_id(0)
    ne, tm = meta_ref.shape
    h2 = h_ref[...] + _dot(og_ref[...], wo_ref[...])
    h2_ref[...] = h2
    xn = _rms(h2, gm_ref[...])
    for c, sub in enumerate(_pack_row(xn)):
        xp_ref[c] = sub
    xh = xn.astype(BF16)
    xl = (xn - xh.astype(F32)).astype(BF16)
    wr = wrt_ref[...]
    wh = wr.astype(BF16)
    wl = (wr - wh.astype(F32)).astype(BF16)
    logits = _dot_nt(wh, xh) + (_dot_nt(wh, xl) + _dot_nt(wl, xh))
    row = lax.broadcasted_iota(jnp.int32, (ne, tm), 0).astype(F32)
    v1 = jnp.max(logits, axis=0, keepdims=True)
    i1 = jnp.min(jnp.where(logits == v1, row, ne), axis=0, keepdims=True)
    rest = jnp.where(row == i1, -jnp.inf, logits)
    v2 = jnp.max(rest, axis=0, keepdims=True)
    i2 = jnp.min(jnp.where(rest == v2, row, ne), axis=0, keepdims=True)
    e = jnp.exp(v2 - v1)
    w1 = 1.0 / (1.0 + e)
    w2 = e / (1.0 + e)
    sel1 = row == i1
    sel2 = row == i2
    oh = (sel1 | sel2).astype(F32)

    @pl.when(i == 0)
    def _():
        carry_ref[...] = jnp.zeros_like(carry_ref)

    base = carry_ref[:, 0:1]
    rank = base + (_lane_cumsum(oh) - oh)
    r1 = jnp.sum(jnp.where(sel1, rank, 0.0), axis=0, keepdims=True)
    r2 = jnp.sum(jnp.where(sel2, rank, 0.0), axis=0, keepdims=True)
    total = base + jnp.sum(oh, axis=1, keepdims=True)
    carry_ref[...] = jnp.broadcast_to(total, carry_ref.shape)
    cnt_ref[...] = jnp.broadcast_to(total, cnt_ref.shape)
    zeros = jnp.zeros_like(w1)
    meta_ref[...] = jnp.concatenate(
        [i1.astype(F32), i2.astype(F32), w1, w2, r1, r2, zeros, zeros], axis=0)


def _out_router(og, h, wo, gm, wrt, *, tm):
    n, d = h.shape
    ne = wrt.shape[0]
    sw = d // 2 // SUBROWS
    tok = pl.BlockSpec((tm, d), lambda i: (i, 0))
    return pl.pallas_call(
        _out_router_kernel,
        out_shape=[jax.ShapeDtypeStruct((n, d), F32), jax.ShapeDtypeStruct((SUBROWS, n, sw), jnp.uint32),
                   jax.ShapeDtypeStruct((ne, n), F32), jax.ShapeDtypeStruct((ne, LANES), F32)],
        grid=(n // tm,),
        in_specs=[tok, tok, _const_spec(wo.shape), _const_spec(gm.shape), _const_spec(wrt.shape)],
        out_specs=[tok, pl.BlockSpec((SUBROWS, tm, sw), lambda i: (0, i, 0)),
                   pl.BlockSpec((ne, tm), lambda i: (0, i)),
                   pl.BlockSpec((ne, LANES), lambda i: (0, 0))],
        scratch_shapes=[pltpu.VMEM((ne, LANES), F32)],
        compiler_params=_params(("arbitrary",)),
        name="out_router",
    )(og, h, wo, gm, wrt)


def _sc_mesh():
    return plsc.VectorSubcoreMesh(core_axis_name="core", subcore_axis_name="subcore")


def _sc_scatter(x, idx, out_rows):
    rows, w = x.shape
    n_idx = idx.shape[0]

    @pl.kernel(out_type=jax.ShapeDtypeStruct((out_rows, w), x.dtype), mesh=_sc_mesh(),
               scratch_types=[])
    def scatter_rows(x_hbm, i_hbm, o_hbm):
        def body(x_vmem, i_vmem):
            for s in range(n_idx):
                pltpu.sync_copy(x_vmem, o_hbm.at[i_vmem.at[s]])

        pltpu.emit_pipeline(
            body, grid=(rows // SC_WINDOW,),
            in_specs=[pl.BlockSpec((SC_WINDOW, w), lambda i: (i, 0)),
                      pl.BlockSpec((n_idx, SC_WINDOW), lambda i: (0, i))],
            out_specs=[],
            core_axis_name=("core", "subcore"),
            dimension_semantics=(pltpu.PARALLEL,),
        )(x_hbm, i_hbm)

    return scatter_rows(x, idx)


def _sc_gather(x, idx):
    n = idx.shape[1]
    w = x.shape[1]

    @pl.kernel(out_type=jax.ShapeDtypeStruct((n, w), x.dtype), mesh=_sc_mesh(), scratch_types=[])
    def gather_rows(x_hbm, i_hbm, o_hbm):
        def body(i_vmem, o_vmem):
            pltpu.sync_copy(x_hbm.at[i_vmem.at[0]], o_vmem)

        pltpu.emit_pipeline(
            body, grid=(n // SC_WINDOW,),
            in_specs=[pl.BlockSpec((1, SC_WINDOW), lambda i: (0, i))],
            out_specs=[pl.BlockSpec((SC_WINDOW, w), lambda i: (i, 0))],
            core_axis_name=("core", "subcore"),
            dimension_semantics=(pltpu.PARALLEL,),
        )(i_hbm, o_hbm)

    return gather_rows(x, idx)


def _experts_kernel(te_ref, rows_ref, x_ref, wa_ref, wb_ref, wo_ref, o_ref, xb_ref, hm_ref, *, nf):
    del te_ref
    i = pl.program_id(0)
    f = pl.program_id(1)
    tr = xb_ref.shape[0]
    tf = wa_ref.shape[2]
    n_valid = rows_ref[i]

    ts = tf // EXPERT_SUBSLABS
    full = n_valid == tr

    def tokens(rows, start):
        x = _unpack_row([x_ref[c, rows, :] for c in range(SUBROWS)])
        if start is not None:
            live = start + lax.broadcasted_iota(jnp.int32, (x.shape[0], 1), 0) < n_valid
            x = jnp.where(live, x, 0.0)
        xb = x.astype(BF16)
        xb_ref[rows, :] = xb
        return xb

    def hidden(j, rows, start=None):
        xb = tokens(rows, start) if j == 0 else xb_ref[rows, :]
        for c in range(EXPERT_SUBSLABS):
            a = _dot(xb, wa_ref[0, :, c * ts:(c + 1) * ts])
            b = _dot(xb, wb_ref[0, :, c * ts:(c + 1) * ts])
            col = j * tf + c * ts
            hm_ref[rows, col:col + ts] = (a * jax.nn.sigmoid(a) * b).astype(BF16)

    def project(rows, start=None):
        for c, sub in enumerate(_pack_row(_dot(hm_ref[rows, :], wo_ref[0]))):
            o_ref[c, rows, :] = sub

    def row_blocks(body):
        def step(r, carry):
            start = pl.multiple_of(r * ROW_BLOCK, ROW_BLOCK)
            body(pl.ds(start, ROW_BLOCK), start)
            return carry
        lax.fori_loop(0, (n_valid + ROW_BLOCK - 1) // ROW_BLOCK, step, 0)

    for j in range(nf):
        last = j == nf - 1

        @pl.when(full & (f == j))
        def _(j=j, last=last):
            hidden(j, slice(None))
            if last:
                project(slice(None))

        @pl.when(jnp.logical_not(full) & (n_valid > 0) & (f == j))
        def _(j=j, last=last):
            row_blocks(functools.partial(hidden, j))
            if last:
                row_blocks(project)

    @pl.when(jnp.logical_not(full) & (f == 0))
    def _():
        o_ref[...] = jnp.zeros_like(o_ref)


def _experts(tile_expert, tile_rows, xs, w_in, w_out, *, tr, tf):
    _, p, sw = xs.shape
    d = w_out.shape[2]
    de = w_out.shape[1]
    nf = de // tf
    rows = pl.BlockSpec((SUBROWS, tr, sw), lambda i, f, te, tv: (0, i, 0))
    return pl.pallas_call(
        functools.partial(_experts_kernel, nf=nf),
        out_shape=jax.ShapeDtypeStruct((SUBROWS, p, sw), jnp.uint32),
        grid_spec=pltpu.PrefetchScalarGridSpec(
            num_scalar_prefetch=2,
            grid=(p // tr, nf),
            in_specs=[
                rows,
                pl.BlockSpec((1, d, tf), lambda i, f, te, tv: (te[i], 0, f)),
                pl.BlockSpec((1, d, tf), lambda i, f, te, tv: (te[i], 0, nf + f)),
                pl.BlockSpec((1, de, d), lambda i, f, te, tv: (te[i], 0, 0)),
            ],
            out_specs=rows,
            scratch_shapes=[pltpu.VMEM((tr, d), BF16), pltpu.VMEM((tr, de), BF16)],
        ),
        compiler_params=_params(("arbitrary", "arbitrary")),
        name="experts",
    )(tile_expert, tile_rows, xs, w_in, w_in, w_out)


def _combine_kernel(g_ref, h_ref, w_ref, o_ref):
    w = w_ref[...]
    y = [_unpack_row([g_ref[s, c] for c in range(SUBROWS)]) for s in range(TOP_K)]
    o_ref[...] = h_ref[...] + (w[:, 0:1] * y[0] + w[:, 1:2] * y[1])


def _combine(g, h, w, *, tc):
    n, d = h.shape
    sw = g.shape[3]
    tok = pl.BlockSpec((tc, d), lambda i: (i, 0))
    return pl.pallas_call(
        _combine_kernel,
        out_shape=jax.ShapeDtypeStruct((n, d), F32),
        grid=(n // tc,),
        in_specs=[pl.BlockSpec((TOP_K, SUBROWS, tc, sw), lambda i: (0, 0, i, 0)), tok,
                  pl.BlockSpec((tc, TOP_K), lambda i: (i, 0))],
        out_specs=tok,
        compiler_params=_params(("parallel",)),
        name="combine",
    )(g, h, w)


def _tiles(n, seq):
    def pick(limit, of):
        t = limit
        while of % t:
            t //= 2
        return t
    return dict(
        tm_a=pick(512, seq), tm_f=pick(512, n), tm_qkv=pick(1024, seq), blk=pick(256, seq),
        tm_o=pick(1024, n), tr=pick(1024, TOP_K * n), tc=pick(1024, n))


def kernel(x, a_norm_g, a_w_in, a_v_norm_g, a_w_spatial, a_b_spatial, a_w_out, f_norm_g, f_w_in, f_w_out, kv_norm_g, kv_w, kv_b_f, k_norm_g, b_norm_g, b_w_in, q_norm_g, b_w_out, m_norm_g, m_w_router, m_w_in, m_w_out):
    batch, seq, d = x.shape
    n = batch * seq
    nh = d // HEAD_DIM
    ne = m_w_router.shape[-1]
    assert a_w_in.shape[0] == 1 and b_w_in.shape[0] == 1 and f_w_in.shape[0] == 1 and m_w_in.shape[0] == 1
    assert seq % GMLP_CHUNK == 0 and d % LANES == 0 and (SUBROWS * n) % SC_WINDOW == 0
    t = _tiles(n, seq)
    row = lambda g: g.reshape(1, -1)

    h = x.reshape(n, d)
    h = _mixer_a(h, row(a_norm_g[0]), a_w_in[0].astype(BF16), row(a_v_norm_g[0]), a_w_spatial[0],
                 a_b_spatial[0].T, a_w_out[0].astype(BF16), tm=t["tm_a"])
    h = _swiglu(h, row(f_norm_g[0]), f_w_in[0].astype(BF16), f_w_out[0].astype(BF16), tm=t["tm_f"])

    head = jnp.arange(LANES, dtype=jnp.int32) // HEAD_DIM
    hsum = (head[:, None] == head[None, :]).astype(BF16)
    q, k, vt, sg, fp = _qkv(
        h, row(kv_norm_g), row(b_norm_g[0]), kv_w[:, :d].astype(BF16),
        kv_w[:, d:2 * d].T.astype(BF16), kv_w[:, 2 * d:].T.astype(BF16), kv_b_f.reshape(nh, 1),
        row(jnp.tile(k_norm_g, nh)), row(jnp.tile(q_norm_g[0], nh)), b_w_in[0].astype(BF16), hsum,
        tm=t["tm_qkv"], batch=batch, seq=seq)
    og, (moe_w_in, moe_w_out) = _attention(q, k, vt, jnp.swapaxes(fp, 1, 2), sg,
                                           (m_w_in[0], m_w_out[0]), batch=batch, seq=seq, blk=t["blk"])

    h2, xp, meta, cnt = _out_router(og, h, b_w_out[0].astype(BF16), row(m_norm_g[0]),
                                    m_w_router[0].T, tm=t["tm_o"])

    tr = t["tr"]
    n_tiles = TOP_K * n // tr + ne
    p = n_tiles * tr
    counts = cnt[:, 0].astype(jnp.int32)
    tiles_per_expert = (counts + tr - 1) // tr
    tile_end = jnp.cumsum(tiles_per_expert)
    tile_start = tile_end - tiles_per_expert
    expert_ids = jnp.arange(ne, dtype=jnp.int32)
    idx = meta[0:TOP_K].astype(jnp.int32)
    start_of = jnp.sum(jnp.where(idx[:, :, None] == expert_ids, tile_start * tr, 0), axis=-1)
    dest = start_of + meta[4:4 + TOP_K].astype(jnp.int32)
    tile_ids = jnp.arange(n_tiles, dtype=jnp.int32)
    tile_expert = jnp.minimum(
        jnp.sum((tile_ids[:, None] >= tile_end[None, :]).astype(jnp.int32), axis=1), ne - 1)
    mine = tile_expert[:, None] == expert_ids[None, :]
    tile_rows = jnp.clip(
        jnp.sum(jnp.where(mine, counts - (tile_ids[:, None] - tile_start) * tr, 0), axis=1), 0, tr)
    tile_rows = jnp.where(tile_ids < tile_end[-1], tile_rows, 0).astype(jnp.int32)

    sub = (jnp.arange(SUBROWS, dtype=jnp.int32) * p)[None, :, None]
    sub_dest = sub + dest[:, None, :]
    sw = xp.shape[2]
    xs = _sc_scatter(xp.reshape(SUBROWS * n, sw), sub_dest.reshape(TOP_K, SUBROWS * n), SUBROWS * p)
    eo = _experts(tile_expert, tile_rows, xs.reshape(SUBROWS, p, sw), moe_w_in, moe_w_out,
                  tr=tr, tf=m_w_out.shape[2] // EXPERT_STEPS)
    g = _sc_gather(eo.reshape(SUBROWS * p, sw), sub_dest.reshape(1, TOP_K * SUBROWS * n))
    out = _combine(g.reshape(TOP_K, SUBROWS, n, sw), h2, meta[2:2 + TOP_K].T, tc=t["tc"])
    return out.reshape(batch, seq, d)
```

```python
import functools

import jax
import jax.numpy as jnp
from jax import lax
from jax.experimental import pallas as pl
from jax.experimental.pallas import tpu as pltpu
from jax.experimental.pallas import tpu_sc as plsc

RMS_EPS = 1e-6
NEG_INF = -1e30
LOG2E = 1.4426950408889634
N_PIECES = 3
SC_WINDOW = 128
SUBROWS = 2
EXPERT_STEPS = 2
EXPERT_SUBSLABS = 2
ROW_BLOCK = 256
GMLP_CHUNK = 128
CAUSAL_CHUNK = 64
A_GROUPS = 8
HEAD_DIM = 64
LANES = 128
TOP_K = 2
VMEM_LIMIT = 56 * 1024 * 1024

BF16 = jnp.bfloat16
F32 = jnp.float32


def _dot(a, b, **kw):
    return jnp.dot(a, b, preferred_element_type=F32, **kw)


def _dot_nt(a, b, **kw):
    return lax.dot_general(a, b, (((1,), (1,)), ((), ())), preferred_element_type=F32, **kw)


def _rms(x, g):
    return x * lax.rsqrt(jnp.mean(x * x, axis=-1, keepdims=True) + RMS_EPS) * g


def _gelu(x):
    return 0.5 * x * (1.0 + lax.erf(x * (2.0 ** -0.5)))


def _const_spec(shape):
    nd = len(shape)
    return pl.BlockSpec(shape, lambda *_: (0,) * nd, pipeline_mode=pl.Buffered(1))


def _params(sem):
    return pltpu.CompilerParams(dimension_semantics=sem, vmem_limit_bytes=VMEM_LIMIT)


def _mixer_a_kernel(x_ref, g_ref, win_ref, gv_ref, ws_ref, bs_ref, wout_ref, o_ref, z_ref):
    tm = x_ref.shape[0]
    half = wout_ref.shape[0]
    gd = half // A_GROUPS
    x = x_ref[...]
    xb = _rms(x, g_ref[...]).astype(BF16)
    v = _gelu(_dot(xb, win_ref[:, half:]))
    v = (_rms(v, gv_ref[...])).astype(BF16)
    u = _gelu(_dot(xb, win_ref[:, :half]))
    row = lax.broadcasted_iota(jnp.int32, (GMLP_CHUNK, GMLP_CHUNK), 0)
    col = lax.broadcasted_iota(jnp.int32, (GMLP_CHUNK, GMLP_CHUNK), 1)
    keep = (col // CAUSAL_CHUNK) <= (row // CAUSAL_CHUNK)
    bs = bs_ref[...]
    for g in range(A_GROUPS):
        wg = jnp.where(keep, ws_ref[g], 0.0).astype(BF16)
        bg = bs[:, g:g + 1]
        for c in range(tm // GMLP_CHUNK):
            rs = slice(c * GMLP_CHUNK, (c + 1) * GMLP_CHUNK)
            cs = slice(g * gd, (g + 1) * gd)
            sv = _dot(wg, v[rs, cs]) + bg
            z_ref[rs, cs] = (u[rs, cs] * sv).astype(BF16)
    o_ref[...] = x + _dot(z_ref[...], wout_ref[...])


def _mixer_a(h, g, w_in, gv, ws, bs_t, w_out, *, tm):
    n, d = h.shape
    half = w_out.shape[0]
    return pl.pallas_call(
        _mixer_a_kernel,
        out_shape=jax.ShapeDtypeStruct((n, d), F32),
        grid=(n // tm,),
        in_specs=[
            pl.BlockSpec((tm, d), lambda i: (i, 0)),
            _const_spec(g.shape), _const_spec(w_in.shape), _const_spec(gv.shape),
            _const_spec(ws.shape), _const_spec(bs_t.shape), _const_spec(w_out.shape),
        ],
        out_specs=pl.BlockSpec((tm, d), lambda i: (i, 0)),
        scratch_shapes=[pltpu.VMEM((tm, half), BF16)],
        compiler_params=_params(("parallel",)),
        name="mixer_a",
    )(h, g, w_in, gv, ws, bs_t, w_out)


def _swiglu_kernel(x_ref, g_ref, win_ref, wout_ref, o_ref):
    f = wout_ref.shape[0]
    x = x_ref[...]
    xb = _rms(x, g_ref[...]).astype(BF16)
    a = _dot(xb, win_ref[:, :f])
    b = _dot(xb, win_ref[:, f:])
    hm = (a * jax.nn.sigmoid(a) * b).astype(BF16)
    o_ref[...] = x + _dot(hm, wout_ref[...])


def _swiglu(h, g, w_in, w_out, *, tm):
    n, d = h.shape
    return pl.pallas_call(
        _swiglu_kernel,
        out_shape=jax.ShapeDtypeStruct((n, d), F32),
        grid=(n // tm,),
        in_specs=[
            pl.BlockSpec((tm, d), lambda i: (i, 0)),
            _const_spec(g.shape), _const_spec(w_in.shape), _const_spec(w_out.shape),
        ],
        out_specs=pl.BlockSpec((tm, d), lambda i: (i, 0)),
        compiler_params=_params(("parallel",)),
        name="swiglu",
    )(h, g, w_in, w_out)


def _lane_cumsum(x):
    n = x.shape[-1]
    lane = lax.broadcasted_iota(jnp.int32, x.shape, x.ndim - 1)
    sh = 1
    while sh < n:
        x = x + jnp.where(lane >= sh, pltpu.roll(x, sh, axis=x.ndim - 1), 0.0)
        sh *= 2
    return x


def _split_bf16(x):
    pieces = []
    for _ in range(N_PIECES):
        p = x.astype(BF16)
        pieces.append(p)
        x = x - p.astype(F32)
    return pieces


def _qkv_kernel(x_ref, gkv_ref, gb_ref, wk_ref, wvt_ref, wft_ref, bf_ref, gk_ref, gq_ref, wqg_ref,
                hsum_ref, q_ref, k_ref, vt_ref, sg_ref, fp_ref, carry_ref, *, tiles_per_seq):
    d = x_ref.shape[1]
    i = pl.program_id(0)

    @pl.when(i == 0)
    def _():
        carry_ref[...] = jnp.zeros_like(carry_ref)

    x = x_ref[...]
    y = x * lax.rsqrt(jnp.mean(x * x, axis=-1, keepdims=True) + RMS_EPS)
    skv = (y * gkv_ref[...]).astype(BF16)
    sb = (y * gb_ref[...]).astype(BF16)
    hsum = hsum_ref[...]

    def head_norm(t, gain):
        sq = (t * t).astype(BF16)
        ssq = jnp.concatenate([_dot(sq[:, j * LANES:(j + 1) * LANES], hsum)
                               for j in range(d // LANES)], axis=1)
        return t * lax.rsqrt(ssq * (1.0 / HEAD_DIM) + RMS_EPS) * gain

    f = _dot_nt(wft_ref[...], skv) + bf_ref[...]
    logf = jax.nn.log_sigmoid(f)
    carry = jnp.where(i % tiles_per_seq == 0, 0.0, carry_ref[:, 0:1])
    cum = _lane_cumsum(logf) + carry
    carry_ref[...] = jnp.broadcast_to(cum[:, -1:], carry_ref.shape)
    fp_ref[0] = jnp.concatenate(_split_bf16(cum * LOG2E), axis=0)

    k = _dot(skv, wk_ref[...])
    k_ref[...] = head_norm(k, gk_ref[...]).astype(BF16)
    vt_ref[0] = _dot_nt(wvt_ref[...], skv).astype(BF16)
    q = _dot(sb, wqg_ref[:, :d])
    q_ref[...] = (head_norm(q, gq_ref[...]) * (LOG2E * HEAD_DIM ** -0.5)).astype(BF16)
    sg_ref[...] = jax.nn.sigmoid(_dot(sb, wqg_ref[:, d:])).astype(BF16)


def _qkv(h, gkv, gb, wk, wvt, wft, bf, gk, gq, wqg, hsum, *, tm, batch, seq):
    n, d = h.shape
    nh = wft.shape[0]
    tps = seq // tm
    tok = pl.BlockSpec((tm, d), lambda i: (i, 0))
    tok_bf = jax.ShapeDtypeStruct((n, d), BF16)
    seq_map = lambda i: (i // tps, 0, i % tps)
    consts = (gkv, gb, wk, wvt, wft, bf, gk, gq, wqg, hsum)
    return pl.pallas_call(
        functools.partial(_qkv_kernel, tiles_per_seq=tps),
        out_shape=[tok_bf, tok_bf, jax.ShapeDtypeStruct((batch, d, seq), BF16), tok_bf,
                   jax.ShapeDtypeStruct((batch, N_PIECES * nh, seq), BF16)],
        grid=(n // tm,),
        in_specs=[tok] + [_const_spec(a.shape) for a in consts],
        out_specs=[tok, tok, pl.BlockSpec((1, d, tm), seq_map), tok,
                   pl.BlockSpec((1, N_PIECES * nh, tm), seq_map)],
        scratch_shapes=[pltpu.VMEM((nh, LANES), F32)],
        compiler_params=_params(("arbitrary",)),
        name="qkv",
    )(h, *consts)


def _attention_kernel(q_ref, k_ref, vt_ref, fp_ref, sg_ref, *rest, blk, nq, n_riders):
    rider_in, (o_ref, *rider_out) = rest[:n_riders], rest[n_riders:2 * n_riders + 1]
    kaug_ref, qaug_ref, vsum_ref = rest[2 * n_riders + 1:]
    for w_ref, c_ref in zip(rider_in, rider_out):
        c_ref[...] = w_ref[...].astype(BF16)
    hp = pl.program_id(1)
    nh = fp_ref.shape[2] // N_PIECES
    lane = lax.broadcasted_iota(jnp.int32, (1, LANES), 1)
    own = (lane < HEAD_DIM, lane >= HEAD_DIM)
    spare = (HEAD_DIM, 0)
    q_lanes = [(lane >= s) & (lane < s + N_PIECES) for s in spare]
    k_lanes = [(lane >= s + N_PIECES) & (lane < s + 2 * N_PIECES) for s in spare]

    er = lax.broadcasted_iota(jnp.int32, (N_PIECES * nh, LANES), 0)
    ec = lax.broadcasted_iota(jnp.int32, (N_PIECES * nh, LANES), 1)
    place = jnp.zeros((N_PIECES * nh, LANES), F32)
    for h in range(2):
        for j in range(N_PIECES):
            src = er == j * nh + 2 * hp + h
            place = (place + (src & (ec == spare[h] + j)).astype(F32)
                     - (src & (ec == spare[h] + N_PIECES + j)).astype(F32))
    g = _dot(fp_ref[0], place.astype(BF16))
    for h in range(2):
        k_fill = jnp.where(k_lanes[h], g, q_lanes[h].astype(F32)).astype(BF16)
        kaug_ref[h] = jnp.where(own[h], k_ref[...], k_fill)
        q_fill = jnp.where(q_lanes[h], g, k_lanes[h].astype(F32)).astype(BF16)
        qaug_ref[h] = jnp.where(own[h], q_ref[...], q_fill)

    first_rows = lax.broadcasted_iota(jnp.int32, (LANES, 1), 0) < HEAD_DIM
    vsum_ref[0] = jnp.where(first_rows, vt_ref[0], jnp.ones_like(vt_ref[0]))
    vsum_ref[1] = jnp.where(first_rows, jnp.ones_like(vt_ref[0]), vt_ref[0])
    key = lax.broadcasted_iota(jnp.int32, (blk, blk), 0)
    qry = lax.broadcasted_iota(jnp.int32, (blk, blk), 1)
    causal = qry >= key

    def qk(i):
        lo, hi = i * blk, (i + 1) * blk
        return [_dot_nt(kaug_ref[h, 0:hi, :], qaug_ref[h, lo:hi, :]) for h in range(2)]

    scores = qk(0)
    for i in range(nq):
        lo, hi = i * blk, (i + 1) * blk
        next_scores = qk(i + 1) if i + 1 < nq else None
        probs = []
        for s in scores:
            diag = jnp.where(causal, s[lo:hi], NEG_INF)
            s = diag if i == 0 else jnp.concatenate([s[0:lo], diag], axis=0)
            probs.append(jnp.exp2(s - jnp.max(s, axis=0, keepdims=True)).astype(BF16))
        outs = [_dot(vsum_ref[h, :, 0:hi], probs[h]) for h in range(2)]
        o0 = outs[0] / outs[0][HEAD_DIM:HEAD_DIM + 1, :]
        o1 = outs[1] / outs[1][0:1, :]
        o = jnp.where(first_rows, o0, o1).T
        o_ref[lo:hi, :] = (o * sg_ref[lo:hi, :].astype(F32)).astype(BF16)
        scores = next_scores


def _attention(q, k, vt, fp, sg, riders, *, batch, seq, blk):
    n, d = q.shape
    n_pairs = d // LANES
    steps = batch * n_pairs
    pair = pl.BlockSpec((seq, LANES), lambda b, hp: (b, hp))
    slabs = [w.reshape(steps, -1, w.shape[-1]) for w in riders]
    slab_specs = [pl.BlockSpec((1,) + s.shape[1:], lambda b, hp: (b * n_pairs + hp, 0, 0))
                  for s in slabs]
    out = pl.pallas_call(
        functools.partial(_attention_kernel, blk=blk, nq=seq // blk, n_riders=len(riders)),
        out_shape=[jax.ShapeDtypeStruct((n, d), BF16)]
        + [jax.ShapeDtypeStruct(s.shape, BF16) for s in slabs],
        grid=(batch, n_pairs),
        in_specs=[
            pair, pair,
            pl.BlockSpec((1, LANES, seq), lambda b, hp: (b, hp, 0)),
            pl.BlockSpec((1, seq, fp.shape[2]), lambda b, hp: (b, 0, 0)),
            pair,
        ] + slab_specs,
        out_specs=[pair] + slab_specs,
        scratch_shapes=[pltpu.VMEM((2, seq, LANES), BF16)] * 2 + [pltpu.VMEM((2, LANES, seq), BF16)],
        compiler_params=_params(("parallel", "parallel")),
        name="attention",
    )(q, k, vt, fp, sg, *slabs)
    return out[0], [c.reshape(w.shape) for c, w in zip(out[1:], riders)]


def _pack_bf16_pair(lo, hi):
    ulo = lax.bitcast_convert_type(lo.astype(BF16).astype(F32), jnp.uint32)
    uhi = lax.bitcast_convert_type(hi.astype(BF16).astype(F32), jnp.uint32)
    return (ulo >> 16) | uhi


def _pack_row(x):
    half = x.shape[1] // 2
    w = _pack_bf16_pair(x[:, :half], x[:, half:])
    sw = half // SUBROWS
    return [w[:, c * sw:(c + 1) * sw] for c in range(SUBROWS)]


def _unpack_row(subrows):
    lo = [lax.bitcast_convert_type(p << 16, F32) for p in subrows]
    hi = [lax.bitcast_convert_type(p & jnp.uint32(0xFFFF0000), F32) for p in subrows]
    return jnp.concatenate(lo + hi, axis=1)


def _out_router_kernel(og_ref, h_ref, wo_ref, gm_ref, wrt_ref, h2_ref, xp_ref, meta_ref, cnt_ref,
                       carry_ref):
    i = pl.program_id(0)
    ne, tm = meta_ref.shape
    h2 = h_ref[...] + _dot(og_ref[...], wo_ref[...])
    h2_ref[...] = h2
    xn = _rms(h2, gm_ref[...])
    for c, sub in enumerate(_pack_row(xn)):
        xp_ref[c] = sub
    xh = xn.astype(BF16)
    xl = (xn - xh.astype(F32)).astype(BF16)
    wr = wrt_ref[...]
    wh = wr.astype(BF16)
    wl = (wr - wh.astype(F32)).astype(BF16)
    logits = _dot_nt(wh, xh) + (_dot_nt(wh, xl) + _dot_nt(wl, xh))
    row = lax.broadcasted_iota(jnp.int32, (ne, tm), 0).astype(F32)
    v1 = jnp.max(logits, axis=0, keepdims=True)
    i1 = jnp.min(jnp.where(logits == v1, row, ne), axis=0, keepdims=True)
    rest = jnp.where(row == i1, -jnp.inf, logits)
    v2 = jnp.max(rest, axis=0, keepdims=True)
    i2 = jnp.min(jnp.where(rest == v2, row, ne), axis=0, keepdims=True)
    e = jnp.exp(v2 - v1)
    w1 = 1.0 / (1.0 + e)
    w2 = e / (1.0 + e)
    sel1 = row == i1
    sel2 = row == i2
    oh = (sel1 | sel2).astype(F32)

    @pl.when(i == 0)
    def _():
        carry_ref[...] = jnp.zeros_like(carry_ref)

    base = carry_ref[:, 0:1]
    rank = base + (_lane_cumsum(oh) - oh)
    r1 = jnp.sum(jnp.where(sel1, rank, 0.0), axis=0, keepdims=True)
    r2 = jnp.sum(jnp.where(sel2, rank, 0.0), axis=0, keepdims=True)
    total = base + jnp.sum(oh, axis=1, keepdims=True)
    carry_ref[...] = jnp.broadcast_to(total, carry_ref.shape)
    cnt_ref[...] = jnp.broadcast_to(total, cnt_ref.shape)
    zeros = jnp.zeros_like(w1)
    meta_ref[...] = jnp.concatenate(
        [i1.astype(F32), i2.astype(F32), w1, w2, r1, r2, zeros, zeros], axis=0)


def _out_router(og, h, wo, gm, wrt, *, tm):
    n, d = h.shape
    ne = wrt.shape[0]
    sw = d // 2 // SUBROWS
    tok = pl.BlockSpec((tm, d), lambda i: (i, 0))
    return pl.pallas_call(
        _out_router_kernel,
        out_shape=[jax.ShapeDtypeStruct((n, d), F32), jax.ShapeDtypeStruct((SUBROWS, n, sw), jnp.uint32),
                   jax.ShapeDtypeStruct((ne, n), F32), jax.ShapeDtypeStruct((ne, LANES), F32)],
        grid=(n // tm,),
        in_specs=[tok, tok, _const_spec(wo.shape), _const_spec(gm.shape), _const_spec(wrt.shape)],
        out_specs=[tok, pl.BlockSpec((SUBROWS, tm, sw), lambda i: (0, i, 0)),
                   pl.BlockSpec((ne, tm), lambda i: (0, i)),
                   pl.BlockSpec((ne, LANES), lambda i: (0, 0))],
        scratch_shapes=[pltpu.VMEM((ne, LANES), F32)],
        compiler_params=_params(("arbitrary",)),
        name="out_router",
    )(og, h, wo, gm, wrt)


def _sc_mesh():
    return plsc.VectorSubcoreMesh(core_axis_name="core", subcore_axis_name="subcore")


def _sc_scatter(x, idx, out_rows):
    rows, w = x.shape
    n_idx = idx.shape[0]

    @pl.kernel(out_type=jax.ShapeDtypeStruct((out_rows, w), x.dtype), mesh=_sc_mesh(),
               scratch_types=[])
    def scatter_rows(x_hbm, i_hbm, o_hbm):
        def body(x_vmem, i_vmem):
            for s in range(n_idx):
                pltpu.sync_copy(x_vmem, o_hbm.at[i_vmem.at[s]])

        pltpu.emit_pipeline(
            body, grid=(rows // SC_WINDOW,),
            in_specs=[pl.BlockSpec((SC_WINDOW, w), lambda i: (i, 0)),
                      pl.BlockSpec((n_idx, SC_WINDOW), lambda i: (0, i))],
            out_specs=[],
            core_axis_name=("core", "subcore"),
            dimension_semantics=(pltpu.PARALLEL,),
        )(x_hbm, i_hbm)

    return scatter_rows(x, idx)


def _sc_gather(x, idx):
    n = idx.shape[1]
    w = x.shape[1]

    @pl.kernel(out_type=jax.ShapeDtypeStruct((n, w), x.dtype), mesh=_sc_mesh(), scratch_types=[])
    def gather_rows(x_hbm, i_hbm, o_hbm):
        def body(i_vmem, o_vmem):
            pltpu.sync_copy(x_hbm.at[i_vmem.at[0]], o_vmem)

        pltpu.emit_pipeline(
            body, grid=(n // SC_WINDOW,),
            in_specs=[pl.BlockSpec((1, SC_WINDOW), lambda i: (0, i))],
            out_specs=[pl.BlockSpec((SC_WINDOW, w), lambda i: (i, 0))],
            core_axis_name=("core", "subcore"),
            dimension_semantics=(pltpu.PARALLEL,),
        )(i_hbm, o_hbm)

    return gather_rows(x, idx)


def _experts_kernel(te_ref, rows_ref, x_ref, wa_ref, wb_ref, wo_ref, o_ref, xb_ref, hm_ref, *, nf):
    del te_ref
    i = pl.program_id(0)
    f = pl.program_id(1)
    tr = xb_ref.shape[0]
    tf = wa_ref.shape[2]
    n_valid = rows_ref[i]

    ts = tf // EXPERT_SUBSLABS
    full = n_valid == tr

    def tokens(rows, start):
        x = _unpack_row([x_ref[c, rows, :] for c in range(SUBROWS)])
        if start is not None:
            live = start + lax.broadcasted_iota(jnp.int32, (x.shape[0], 1), 0) < n_valid
            x = jnp.where(live, x, 0.0)
        xb = x.astype(BF16)
        xb_ref[rows, :] = xb
        return xb

    def hidden(j, rows, start=None):
        xb = tokens(rows, start) if j == 0 else xb_ref[rows, :]
        for c in range(EXPERT_SUBSLABS):
            a = _dot(xb, wa_ref[0, :, c * ts:(c + 1) * ts])
            b = _dot(xb, wb_ref[0, :, c * ts:(c + 1) * ts])
            col = j * tf + c * ts
            hm_ref[rows, col:col + ts] = (a * jax.nn.sigmoid(a) * b).astype(BF16)

    def project(rows, start=None):
        for c, sub in enumerate(_pack_row(_dot(hm_ref[rows, :], wo_ref[0]))):
            o_ref[c, rows, :] = sub

    def row_blocks(body):
        def step(r, carry):
            start = pl.multiple_of(r * ROW_BLOCK, ROW_BLOCK)
            body(pl.ds(start, ROW_BLOCK), start)
            return carry
        lax.fori_loop(0, (n_valid + ROW_BLOCK - 1) // ROW_BLOCK, step, 0)

    for j in range(nf):
        last = j == nf - 1

        @pl.when(full & (f == j))
        def _(j=j, last=last):
            hidden(j, slice(None))
            if last:
                project(slice(None))

        @pl.when(jnp.logical_not(full) & (n_valid > 0) & (f == j))
        def _(j=j, last=last):
            row_blocks(functools.partial(hidden, j))
            if last:
                row_blocks(project)

    @pl.when(jnp.logical_not(full) & (f == 0))
    def _():
        o_ref[...] = jnp.zeros_like(o_ref)


def _experts(tile_expert, tile_rows, xs, w_in, w_out, *, tr, tf):
    _, p, sw = xs.shape
    d = w_out.shape[2]
    de = w_out.shape[1]
    nf = de // tf
    rows = pl.BlockSpec((SUBROWS, tr, sw), lambda i, f, te, tv: (0, i, 0))

    def slab(i, f, tv):
        return jnp.where(tv[i] > 0, f, nf - 1)

    return pl.pallas_call(
        functools.partial(_experts_kernel, nf=nf),
        out_shape=jax.ShapeDtypeStruct((SUBROWS, p, sw), jnp.uint32),
        grid_spec=pltpu.PrefetchScalarGridSpec(
            num_scalar_prefetch=2,
            grid=(p // tr, nf),
            in_specs=[
                rows,
                pl.BlockSpec((1, d, tf), lambda i, f, te, tv: (te[i], 0, slab(i, f, tv))),
                pl.BlockSpec((1, d, tf), lambda i, f, te, tv: (te[i], 0, nf + slab(i, f, tv))),
                pl.BlockSpec((1, de, d), lambda i, f, te, tv: (te[i], 0, 0)),
            ],
            out_specs=rows,
            scratch_shapes=[pltpu.VMEM((tr, d), BF16), pltpu.VMEM((tr, de), BF16)],
        ),
        compiler_params=_params(("arbitrary", "arbitrary")),
        name="experts",
    )(tile_expert, tile_rows, xs, w_in, w_in, w_out)


def _combine_kernel(g_ref, h_ref, w_ref, o_ref):
    w = w_ref[...]
    y = [_unpack_row([g_ref[s, c] for c in range(SUBROWS)]) for s in range(TOP_K)]
    o_ref[...] = h_ref[...] + (w[:, 0:1] * y[0] + w[:, 1:2] * y[1])


def _combine(g, h, w, *, tc):
    n, d = h.shape
    sw = g.shape[3]
    tok = pl.BlockSpec((tc, d), lambda i: (i, 0))
    return pl.pallas_call(
        _combine_kernel,
        out_shape=jax.ShapeDtypeStruct((n, d), F32),
        grid=(n // tc,),
        in_specs=[pl.BlockSpec((TOP_K, SUBROWS, tc, sw), lambda i: (0, 0, i, 0)), tok,
                  pl.BlockSpec((tc, TOP_K), lambda i: (i, 0))],
        out_specs=tok,
        compiler_params=_params(("parallel",)),
        name="combine",
    )(g, h, w)


def _tiles(n, seq):
    def pick(limit, of):
        t = limit
        while of % t:
            t //= 2
        return t
    return dict(
        tm_a=pick(512, seq), tm_f=pick(512, n), tm_qkv=pick(1024, seq), blk=pick(256, seq),
        tm_o=pick(1024, n), tr=pick(1024, TOP_K * n), tc=pick(1024, n))


def kernel(x, a_norm_g, a_w_in, a_v_norm_g, a_w_spatial, a_b_spatial, a_w_out, f_norm_g, f_w_in, f_w_out, kv_norm_g, kv_w, kv_b_f, k_norm_g, b_norm_g, b_w_in, q_norm_g, b_w_out, m_norm_g, m_w_router, m_w_in, m_w_out):
    batch, seq, d = x.shape
    n = batch * seq
    nh = d // HEAD_DIM
    ne = m_w_router.shape[-1]
    assert a_w_in.shape[0] == 1 and b_w_in.shape[0] == 1 and f_w_in.shape[0] == 1 and m_w_in.shape[0] == 1
    assert seq % GMLP_CHUNK == 0 and d % LANES == 0 and (SUBROWS * n) % SC_WINDOW == 0
    t = _tiles(n, seq)
    row = lambda g: g.reshape(1, -1)

    h = x.reshape(n, d)
    h = _mixer_a(h, row(a_norm_g[0]), a_w_in[0].astype(BF16), row(a_v_norm_g[0]), a_w_spatial[0],
                 a_b_spatial[0].T, a_w_out[0].astype(BF16), tm=t["tm_a"])
    h = _swiglu(h, row(f_norm_g[0]), f_w_in[0].astype(BF16), f_w_out[0].astype(BF16), tm=t["tm_f"])

    head = jnp.arange(LANES, dtype=jnp.int32) // HEAD_DIM
    hsum = (head[:, None] == head[None, :]).astype(BF16)
    q, k, vt, sg, fp = _qkv(
        h, row(kv_norm_g), row(b_norm_g[0]), kv_w[:, :d].astype(BF16),
        kv_w[:, d:2 * d].T.astype(BF16), kv_w[:, 2 * d:].T.astype(BF16), kv_b_f.reshape(nh, 1),
        row(jnp.tile(k_norm_g, nh)), row(jnp.tile(q_norm_g[0], nh)), b_w_in[0].astype(BF16), hsum,
        tm=t["tm_qkv"], batch=batch, seq=seq)
    og, (moe_w_in, moe_w_out) = _attention(q, k, vt, jnp.swapaxes(fp, 1, 2), sg,
                                           (m_w_in[0], m_w_out[0]), batch=batch, seq=seq, blk=t["blk"])

    h2, xp, meta, cnt = _out_router(og, h, b_w_out[0].astype(BF16), row(m_norm_g[0]),
                                    m_w_router[0].T, tm=t["tm_o"])

    tr = t["tr"]
    n_tiles = TOP_K * n // tr + ne
    p = n_tiles * tr
    counts = cnt[:, 0].astype(jnp.int32)
    tiles_per_expert = (counts + tr - 1) // tr
    tile_end = jnp.cumsum(tiles_per_expert)
    tile_start = tile_end - tiles_per_expert
    expert_ids = jnp.arange(ne, dtype=jnp.int32)
    idx = meta[0:TOP_K].astype(jnp.int32)
    start_of = jnp.sum(jnp.where(idx[:, :, None] == expert_ids, tile_start * tr, 0), axis=-1)
    dest = start_of + meta[4:4 + TOP_K].astype(jnp.int32)
    tile_ids = jnp.arange(n_tiles, dtype=jnp.int32)
    tile_expert = jnp.minimum(
        jnp.sum((tile_ids[:, None] >= tile_end[None, :]).astype(jnp.int32), axis=1), ne - 1)
    mine = tile_expert[:, None] == expert_ids[None, :]
    tile_rows = jnp.clip(
        jnp.sum(jnp.where(mine, counts - (tile_ids[:, None] - tile_start) * tr, 0), axis=1), 0, tr)
    tile_rows = jnp.where(tile_ids < tile_end[-1], tile_rows, 0).astype(jnp.int32)

    sub = (jnp.arange(SUBROWS, dtype=jnp.int32) * p)[None, :, None]
    sub_dest = sub + dest[:, None, :]
    sw = xp.shape[2]
    xs = _sc_scatter(xp.reshape(SUBROWS * n, sw), sub_dest.reshape(TOP_K, SUBROWS * n), SUBROWS * p)
    eo = _experts(tile_expert, tile_rows, xs.reshape(SUBROWS, p, sw), moe_w_in, moe_w_out,
                  tr=tr, tf=m_w_out.shape[2] // EXPERT_STEPS)
    g = _sc_gather(eo.reshape(SUBROWS * p, sw), sub_dest.reshape(1, TOP_K * SUBROWS * n))
    out = _combine(g.reshape(TOP_K, SUBROWS, n, sw), h2, meta[2:2 + TOP_K].T, tc=t["tc"])
    return out.reshape(batch, seq, d)
```

```python
import functools

import jax
import jax.numpy as jnp
from jax import lax
from jax.experimental import pallas as pl
from jax.experimental.pallas import tpu as pltpu
from jax.experimental.pallas import tpu_sc as plsc

RMS_EPS = 1e-6
NEG_INF = -1e30
LOG2E = 1.4426950408889634
N_PIECES = 3
SC_WINDOW = 128
SUBROWS = 2
EXPERT_STEPS = 2
EXPERT_SUBSLABS = 2
ROW_BLOCK = 256
GMLP_CHUNK = 128
CAUSAL_CHUNK = 64
A_GROUPS = 8
HEAD_DIM = 64
LANES = 128
TOP_K = 2
VMEM_LIMIT = 56 * 1024 * 1024

BF16 = jnp.bfloat16
F32 = jnp.float32


def _dot(a, b, **kw):
    return jnp.dot(a, b, preferred_element_type=F32, **kw)


def _dot_nt(a, b, **kw):
    return lax.dot_general(a, b, (((1,), (1,)), ((), ())), preferred_element_type=F32, **kw)


def _rms(x, g):
    return x * lax.rsqrt(jnp.mean(x * x, axis=-1, keepdims=True) + RMS_EPS) * g


def _gelu(x):
    return 0.5 * x * (1.0 + lax.erf(x * (2.0 ** -0.5)))


def _const_spec(shape):
    nd = len(shape)
    return pl.BlockSpec(shape, lambda *_: (0,) * nd, pipeline_mode=pl.Buffered(1))


def _params(sem):
    return pltpu.CompilerParams(dimension_semantics=sem, vmem_limit_bytes=VMEM_LIMIT)


def _rider_slabs(riders, steps, step_index):
    slabs = [w.reshape(steps, -1, w.shape[-1]) for w in riders]
    specs = [pl.BlockSpec((1,) + s.shape[1:], lambda *g: (step_index(*g), 0, 0)) for s in slabs]
    return slabs, specs


def _split_rider_refs(rest, n_riders):
    return rest[:n_riders], rest[n_riders], rest[n_riders + 1:2 * n_riders + 1], rest[2 * n_riders + 1:]


def _convert_riders(rider_in, rider_out):
    for w_ref, c_ref in zip(rider_in, rider_out):
        c_ref[...] = w_ref[...].astype(BF16)


def _unslab(outs, riders):
    return [c.reshape(w.shape) for c, w in zip(outs, riders)]


def _mixer_a_kernel(x_ref, g_ref, win_ref, gv_ref, ws_ref, bs_ref, wout_ref, *rest, n_riders):
    rider_in, o_ref, rider_out, (z_ref,) = _split_rider_refs(rest, n_riders)
    _convert_riders(rider_in, rider_out)
    tm = x_ref.shape[0]
    half = wout_ref.shape[0]
    gd = half // A_GROUPS
    x = x_ref[...]
    xb = _rms(x, g_ref[...]).astype(BF16)
    v = _gelu(_dot(xb, win_ref[:, half:]))
    v = (_rms(v, gv_ref[...])).astype(BF16)
    u = _gelu(_dot(xb, win_ref[:, :half]))
    row = lax.broadcasted_iota(jnp.int32, (GMLP_CHUNK, GMLP_CHUNK), 0)
    col = lax.broadcasted_iota(jnp.int32, (GMLP_CHUNK, GMLP_CHUNK), 1)
    keep = (col // CAUSAL_CHUNK) <= (row // CAUSAL_CHUNK)
    bs = bs_ref[...]
    for g in range(A_GROUPS):
        wg = jnp.where(keep, ws_ref[g], 0.0).astype(BF16)
        bg = bs[:, g:g + 1]
        for c in range(tm // GMLP_CHUNK):
            rs = slice(c * GMLP_CHUNK, (c + 1) * GMLP_CHUNK)
            cs = slice(g * gd, (g + 1) * gd)
            sv = _dot(wg, v[rs, cs]) + bg
            z_ref[rs, cs] = (u[rs, cs] * sv).astype(BF16)
    o_ref[...] = x + _dot(z_ref[...], wout_ref[...])


def _mixer_a(h, g, w_in, gv, ws, bs_t, w_out, riders, *, tm):
    n, d = h.shape
    half = w_out.shape[0]
    slabs, slab_specs = _rider_slabs(riders, n // tm, lambda i: i)
    tok = pl.BlockSpec((tm, d), lambda i: (i, 0))
    out = pl.pallas_call(
        functools.partial(_mixer_a_kernel, n_riders=len(riders)),
        out_shape=[jax.ShapeDtypeStruct((n, d), F32)]
        + [jax.ShapeDtypeStruct(sl.shape, BF16) for sl in slabs],
        grid=(n // tm,),
        in_specs=[
            tok,
            _const_spec(g.shape), _const_spec(w_in.shape), _const_spec(gv.shape),
            _const_spec(ws.shape), _const_spec(bs_t.shape), _const_spec(w_out.shape),
        ] + slab_specs,
        out_specs=[tok] + slab_specs,
        scratch_shapes=[pltpu.VMEM((tm, half), BF16)],
        compiler_params=_params(("parallel",)),
        name="mixer_a",
    )(h, g, w_in, gv, ws, bs_t, w_out, *slabs)
    return out[0], _unslab(out[1:], riders)


def _swiglu_kernel(x_ref, g_ref, win_ref, wout_ref, *rest, n_riders):
    rider_in, o_ref, rider_out, _ = _split_rider_refs(rest, n_riders)
    _convert_riders(rider_in, rider_out)
    f = wout_ref.shape[0]
    x = x_ref[...]
    xb = _rms(x, g_ref[...]).astype(BF16)
    a = _dot(xb, win_ref[:, :f])
    b = _dot(xb, win_ref[:, f:])
    hm = (a * jax.nn.sigmoid(a) * b).astype(BF16)
    o_ref[...] = x + _dot(hm, wout_ref[...])


def _swiglu(h, g, w_in, w_out, riders, *, tm):
    n, d = h.shape
    slabs, slab_specs = _rider_slabs(riders, n // tm, lambda i: i)
    tok = pl.BlockSpec((tm, d), lambda i: (i, 0))
    out = pl.pallas_call(
        functools.partial(_swiglu_kernel, n_riders=len(riders)),
        out_shape=[jax.ShapeDtypeStruct((n, d), F32)]
        + [jax.ShapeDtypeStruct(sl.shape, BF16) for sl in slabs],
        grid=(n // tm,),
        in_specs=[tok, _const_spec(g.shape), _const_spec(w_in.shape), _const_spec(w_out.shape)]
        + slab_specs,
        out_specs=[tok] + slab_specs,
        compiler_params=_params(("parallel",)),
        name="swiglu",
    )(h, g, w_in, w_out, *slabs)
    return out[0], _unslab(out[1:], riders)


def _lane_cumsum(x):
    n = x.shape[-1]
    lane = lax.broadcasted_iota(jnp.int32, x.shape, x.ndim - 1)
    sh = 1
    while sh < n:
        x = x + jnp.where(lane >= sh, pltpu.roll(x, sh, axis=x.ndim - 1), 0.0)
        sh *= 2
    return x


def _split_bf16(x):
    pieces = []
    for _ in range(N_PIECES):
        p = x.astype(BF16)
        pieces.append(p)
        x = x - p.astype(F32)
    return pieces


def _qkv_kernel(x_ref, gkv_ref, gb_ref, wk_ref, wvt_ref, wft_ref, bf_ref, gk_ref, gq_ref, wqg_ref,
                hsum_ref, q_ref, k_ref, vt_ref, sg_ref, fp_ref, carry_ref, *, tiles_per_seq):
    d = x_ref.shape[1]
    i = pl.program_id(0)

    @pl.when(i == 0)
    def _():
        carry_ref[...] = jnp.zeros_like(carry_ref)

    x = x_ref[...]
    y = x * lax.rsqrt(jnp.mean(x * x, axis=-1, keepdims=True) + RMS_EPS)
    skv = (y * gkv_ref[...]).astype(BF16)
    sb = (y * gb_ref[...]).astype(BF16)
    hsum = hsum_ref[...]

    def head_norm(t, gain):
        sq = (t * t).astype(BF16)
        ssq = jnp.concatenate([_dot(sq[:, j * LANES:(j + 1) * LANES], hsum)
                               for j in range(d // LANES)], axis=1)
        return t * lax.rsqrt(ssq * (1.0 / HEAD_DIM) + RMS_EPS) * gain

    f = _dot_nt(wft_ref[...], skv) + bf_ref[...]
    logf = jax.nn.log_sigmoid(f)
    carry = jnp.where(i % tiles_per_seq == 0, 0.0, carry_ref[:, 0:1])
    cum = _lane_cumsum(logf) + carry
    carry_ref[...] = jnp.broadcast_to(cum[:, -1:], carry_ref.shape)
    fp_ref[0] = jnp.concatenate(_split_bf16(cum * LOG2E), axis=0)

    k = _dot(skv, wk_ref[...])
    k_ref[...] = head_norm(k, gk_ref[...]).astype(BF16)
    vt_ref[0] = _dot_nt(wvt_ref[...], skv).astype(BF16)
    q = _dot(sb, wqg_ref[:, :d])
    q_ref[...] = (head_norm(q, gq_ref[...]) * (LOG2E * HEAD_DIM ** -0.5)).astype(BF16)
    sg_ref[...] = jax.nn.sigmoid(_dot(sb, wqg_ref[:, d:])).astype(BF16)


def _qkv(h, gkv, gb, wk, wvt, wft, bf, gk, gq, wqg, hsum, *, tm, batch, seq):
    n, d = h.shape
    nh = wft.shape[0]
    tps = seq // tm
    tok = pl.BlockSpec((tm, d), lambda i: (i, 0))
    tok_bf = jax.ShapeDtypeStruct((n, d), BF16)
    seq_map = lambda i: (i // tps, 0, i % tps)
    consts = (gkv, gb, wk, wvt, wft, bf, gk, gq, wqg, hsum)
    return pl.pallas_call(
        functools.partial(_qkv_kernel, tiles_per_seq=tps),
        out_shape=[tok_bf, tok_bf, jax.ShapeDtypeStruct((batch, d, seq), BF16), tok_bf,
                   jax.ShapeDtypeStruct((batch, N_PIECES * nh, seq), BF16)],
        grid=(n // tm,),
        in_specs=[tok] + [_const_spec(a.shape) for a in consts],
        out_specs=[tok, tok, pl.BlockSpec((1, d, tm), seq_map), tok,
                   pl.BlockSpec((1, N_PIECES * nh, tm), seq_map)],
        scratch_shapes=[pltpu.VMEM((nh, LANES), F32)],
        compiler_params=_params(("arbitrary",)),
        name="qkv",
    )(h, *consts)


def _attention_kernel(q_ref, k_ref, vt_ref, fp_ref, sg_ref, *rest, blk, nq, n_riders):
    rider_in, (o_ref, *rider_out) = rest[:n_riders], rest[n_riders:2 * n_riders + 1]
    kaug_ref, qaug_ref, vsum_ref = rest[2 * n_riders + 1:]
    for w_ref, c_ref in zip(rider_in, rider_out):
        c_ref[...] = w_ref[...].astype(BF16)
    hp = pl.program_id(1)
    nh = fp_ref.shape[2] // N_PIECES
    lane = lax.broadcasted_iota(jnp.int32, (1, LANES), 1)
    own = (lane < HEAD_DIM, lane >= HEAD_DIM)
    spare = (HEAD_DIM, 0)
    q_lanes = [(lane >= s) & (lane < s + N_PIECES) for s in spare]
    k_lanes = [(lane >= s + N_PIECES) & (lane < s + 2 * N_PIECES) for s in spare]

    er = lax.broadcasted_iota(jnp.int32, (N_PIECES * nh, LANES), 0)
    ec = lax.broadcasted_iota(jnp.int32, (N_PIECES * nh, LANES), 1)
    place = jnp.zeros((N_PIECES * nh, LANES), F32)
    for h in range(2):
        for j in range(N_PIECES):
            src = er == j * nh + 2 * hp + h
            place = (place + (src & (ec == spare[h] + j)).astype(F32)
                     - (src & (ec == spare[h] + N_PIECES + j)).astype(F32))
    g = _dot(fp_ref[0], place.astype(BF16))
    for h in range(2):
        k_fill = jnp.where(k_lanes[h], g, q_lanes[h].astype(F32)).astype(BF16)
        kaug_ref[h] = jnp.where(own[h], k_ref[...], k_fill)
        q_fill = jnp.where(q_lanes[h], g, k_lanes[h].astype(F32)).astype(BF16)
        qaug_ref[h] = jnp.where(own[h], q_ref[...], q_fill)

    first_rows = lax.broadcasted_iota(jnp.int32, (LANES, 1), 0) < HEAD_DIM
    vsum_ref[0] = jnp.where(first_rows, vt_ref[0], jnp.ones_like(vt_ref[0]))
    vsum_ref[1] = jnp.where(first_rows, jnp.ones_like(vt_ref[0]), vt_ref[0])
    key = lax.broadcasted_iota(jnp.int32, (blk, blk), 0)
    qry = lax.broadcasted_iota(jnp.int32, (blk, blk), 1)
    causal = qry >= key

    def qk(i):
        lo, hi = i * blk, (i + 1) * blk
        return [_dot_nt(kaug_ref[h, 0:hi, :], qaug_ref[h, lo:hi, :]) for h in range(2)]

    scores = qk(0)
    for i in range(nq):
        lo, hi = i * blk, (i + 1) * blk
        next_scores = qk(i + 1) if i + 1 < nq else None
        probs = []
        for s in scores:
            diag = jnp.where(causal, s[lo:hi], NEG_INF)
            s = diag if i == 0 else jnp.concatenate([s[0:lo], diag], axis=0)
            probs.append(jnp.exp2(s - jnp.max(s, axis=0, keepdims=True)).astype(BF16))
        outs = [_dot(vsum_ref[h, :, 0:hi], probs[h]) for h in range(2)]
        o0 = outs[0] / outs[0][HEAD_DIM:HEAD_DIM + 1, :]
        o1 = outs[1] / outs[1][0:1, :]
        o = jnp.where(first_rows, o0, o1).T
        o_ref[lo:hi, :] = (o * sg_ref[lo:hi, :].astype(F32)).astype(BF16)
        scores = next_scores


def _attention(q, k, vt, fp, sg, riders, *, batch, seq, blk):
    n, d = q.shape
    n_pairs = d // LANES
    steps = batch * n_pairs
    pair = pl.BlockSpec((seq, LANES), lambda b, hp: (b, hp))
    slabs = [w.reshape(steps, -1, w.shape[-1]) for w in riders]
    slab_specs = [pl.BlockSpec((1,) + s.shape[1:], lambda b, hp: (b * n_pairs + hp, 0, 0))
                  for s in slabs]
    out = pl.pallas_call(
        functools.partial(_attention_kernel, blk=blk, nq=seq // blk, n_riders=len(riders)),
        out_shape=[jax.ShapeDtypeStruct((n, d), BF16)]
        + [jax.ShapeDtypeStruct(s.shape, BF16) for s in slabs],
        grid=(batch, n_pairs),
        in_specs=[
            pair, pair,
            pl.BlockSpec((1, LANES, seq), lambda b, hp: (b, hp, 0)),
            pl.BlockSpec((1, seq, fp.shape[2]), lambda b, hp: (b, 0, 0)),
            pair,
        ] + slab_specs,
        out_specs=[pair] + slab_specs,
        scratch_shapes=[pltpu.VMEM((2, seq, LANES), BF16)] * 2 + [pltpu.VMEM((2, LANES, seq), BF16)],
        compiler_params=_params(("parallel", "parallel")),
        name="attention",
    )(q, k, vt, fp, sg, *slabs)
    return out[0], [c.reshape(w.shape) for c, w in zip(out[1:], riders)]


def _pack_bf16_pair(lo, hi):
    ulo = lax.bitcast_convert_type(lo.astype(BF16).astype(F32), jnp.uint32)
    uhi = lax.bitcast_convert_type(hi.astype(BF16).astype(F32), jnp.uint32)
    return (ulo >> 16) | uhi


def _pack_row(x):
    half = x.shape[1] // 2
    w = _pack_bf16_pair(x[:, :half], x[:, half:])
    sw = half // SUBROWS
    return [w[:, c * sw:(c + 1) * sw] for c in range(SUBROWS)]


def _unpack_row(subrows):
    lo = [lax.bitcast_convert_type(p << 16, F32) for p in subrows]
    hi = [lax.bitcast_convert_type(p & jnp.uint32(0xFFFF0000), F32) for p in subrows]
    return jnp.concatenate(lo + hi, axis=1)


def _out_router_kernel(og_ref, h_ref, wo_ref, gm_ref, wrt_ref, h2_ref, xp_ref, meta_ref, cnt_ref,
                       carry_ref):
    i = pl.program_id(0)
    ne, tm = meta_ref.shape
    h2 = h_ref[...] + _dot(og_ref[...], wo_ref[...])
    h2_ref[...] = h2
    xn = _rms(h2, gm_ref[...])
    for c, sub in enumerate(_pack_row(xn)):
        xp_ref[c] = sub
    xh = xn.astype(BF16)
    xl = (xn - xh.astype(F32)).astype(BF16)
    wr = wrt_ref[...]
    wh = wr.astype(BF16)
    wl = (wr - wh.astype(F32)).astype(BF16)
    logits = _dot_nt(wh, xh) + (_dot_nt(wh, xl) + _dot_nt(wl, xh))
    row = lax.broadcasted_iota(jnp.int32, (ne, tm), 0).astype(F32)
    v1 = jnp.max(logits, axis=0, keepdims=True)
    i1 = jnp.min(jnp.where(logits == v1, row, ne), axis=0, keepdims=True)
    rest = jnp.where(row == i1, -jnp.inf, logits)
    v2 = jnp.max(rest, axis=0, keepdims=True)
    i2 = jnp.min(jnp.where(rest == v2, row, ne), axis=0, keepdims=True)
    e = jnp.exp(v2 - v1)
    w1 = 1.0 / (1.0 + e)
    w2 = e / (1.0 + e)
    sel1 = row == i1
    sel2 = row == i2
    oh = (sel1 | sel2).astype(F32)

    @pl.when(i == 0)
    def _():
        carry_ref[...] = jnp.zeros_like(carry_ref)

    base = carry_ref[:, 0:1]
    rank = base + (_lane_cumsum(oh) - oh)
    r1 = jnp.sum(jnp.where(sel1, rank, 0.0), axis=0, keepdims=True)
    r2 = jnp.sum(jnp.where(sel2, rank, 0.0), axis=0, keepdims=True)
    total = base + jnp.sum(oh, axis=1, keepdims=True)
    carry_ref[...] = jnp.broadcast_to(total, carry_ref.shape)
    cnt_ref[...] = jnp.broadcast_to(total, cnt_ref.shape)
    zeros = jnp.zeros_like(w1)
    meta_ref[...] = jnp.concatenate(
        [i1.astype(F32), i2.astype(F32), w1, w2, r1, r2, zeros, zeros], axis=0)


def _out_router(og, h, wo, gm, wrt, *, tm):
    n, d = h.shape
    ne = wrt.shape[0]
    sw = d // 2 // SUBROWS
    tok = pl.BlockSpec((tm, d), lambda i: (i, 0))
    return pl.pallas_call(
        _out_router_kernel,
        out_shape=[jax.ShapeDtypeStruct((n, d), F32), jax.ShapeDtypeStruct((SUBROWS, n, sw), jnp.uint32),
                   jax.ShapeDtypeStruct((ne, n), F32), jax.ShapeDtypeStruct((ne, LANES), F32)],
        grid=(n // tm,),
        in_specs=[tok, tok, _const_spec(wo.shape), _const_spec(gm.shape), _const_spec(wrt.shape)],
        out_specs=[tok, pl.BlockSpec((SUBROWS, tm, sw), lambda i: (0, i, 0)),
                   pl.BlockSpec((ne, tm), lambda i: (0, i)),
                   pl.BlockSpec((ne, LANES), lambda i: (0, 0))],
        scratch_shapes=[pltpu.VMEM((ne, LANES), F32)],
        compiler_params=_params(("arbitrary",)),
        name="out_router",
    )(og, h, wo, gm, wrt)


def _sc_mesh():
    return plsc.VectorSubcoreMesh(core_axis_name="core", subcore_axis_name="subcore")


def _sc_scatter(x, idx, out_rows):
    rows, w = x.shape
    n_idx = idx.shape[0]

    @pl.kernel(out_type=jax.ShapeDtypeStruct((out_rows, w), x.dtype), mesh=_sc_mesh(),
               scratch_types=[])
    def scatter_rows(x_hbm, i_hbm, o_hbm):
        def body(x_vmem, i_vmem):
            for s in range(n_idx):
                pltpu.sync_copy(x_vmem, o_hbm.at[i_vmem.at[s]])

        pltpu.emit_pipeline(
            body, grid=(rows // SC_WINDOW,),
            in_specs=[pl.BlockSpec((SC_WINDOW, w), lambda i: (i, 0)),
                      pl.BlockSpec((n_idx, SC_WINDOW), lambda i: (0, i))],
            out_specs=[],
            core_axis_name=("core", "subcore"),
            dimension_semantics=(pltpu.PARALLEL,),
        )(x_hbm, i_hbm)

    return scatter_rows(x, idx)


def _sc_gather(x, idx):
    n = idx.shape[1]
    w = x.shape[1]

    @pl.kernel(out_type=jax.ShapeDtypeStruct((n, w), x.dtype), mesh=_sc_mesh(), scratch_types=[])
    def gather_rows(x_hbm, i_hbm, o_hbm):
        def body(i_vmem, o_vmem):
            pltpu.sync_copy(x_hbm.at[i_vmem.at[0]], o_vmem)

        pltpu.emit_pipeline(
            body, grid=(n // SC_WINDOW,),
            in_specs=[pl.BlockSpec((1, SC_WINDOW), lambda i: (0, i))],
            out_specs=[pl.BlockSpec((SC_WINDOW, w), lambda i: (i, 0))],
            core_axis_name=("core", "subcore"),
            dimension_semantics=(pltpu.PARALLEL,),
        )(i_hbm, o_hbm)

    return gather_rows(x, idx)


def _experts_kernel(te_ref, rows_ref, x_ref, wa_ref, wb_ref, wo_ref, o_ref, xb_ref, hm_ref, *, nf):
    del te_ref
    i = pl.program_id(0)
    f = pl.program_id(1)
    tr = xb_ref.shape[0]
    tf = wa_ref.shape[2]
    n_valid = rows_ref[i]

    ts = tf // EXPERT_SUBSLABS
    full = n_valid == tr

    def tokens(rows, start):
        x = _unpack_row([x_ref[c, rows, :] for c in range(SUBROWS)])
        if start is not None:
            live = start + lax.broadcasted_iota(jnp.int32, (x.shape[0], 1), 0) < n_valid
            x = jnp.where(live, x, 0.0)
        xb = x.astype(BF16)
        xb_ref[rows, :] = xb
        return xb

    def hidden(j, rows, start=None):
        xb = tokens(rows, start) if j == 0 else xb_ref[rows, :]
        for c in range(EXPERT_SUBSLABS):
            a = _dot(xb, wa_ref[0, :, c * ts:(c + 1) * ts])
            b = _dot(xb, wb_ref[0, :, c * ts:(c + 1) * ts])
            col = j * tf + c * ts
            hm_ref[rows, col:col + ts] = (a * jax.nn.sigmoid(a) * b).astype(BF16)

    def project(rows, start=None):
        for c, sub in enumerate(_pack_row(_dot(hm_ref[rows, :], wo_ref[0]))):
            o_ref[c, rows, :] = sub

    def row_blocks(body):
        def step(r, carry):
            start = pl.multiple_of(r * ROW_BLOCK, ROW_BLOCK)
            body(pl.ds(start, ROW_BLOCK), start)
            return carry
        lax.fori_loop(0, (n_valid + ROW_BLOCK - 1) // ROW_BLOCK, step, 0)

    for j in range(nf):
        last = j == nf - 1

        @pl.when(full & (f == j))
        def _(j=j, last=last):
            hidden(j, slice(None))
            if last:
                project(slice(None))

        @pl.when(jnp.logical_not(full) & (n_valid > 0) & (f == j))
        def _(j=j, last=last):
            row_blocks(functools.partial(hidden, j))
            if last:
                row_blocks(project)

    @pl.when(jnp.logical_not(full) & (f == 0))
    def _():
        o_ref[...] = jnp.zeros_like(o_ref)


def _experts(tile_expert, tile_rows, xs, w_in, w_out, *, tr, tf):
    _, p, sw = xs.shape
    d = w_out.shape[2]
    de = w_out.shape[1]
    nf = de // tf
    rows = pl.BlockSpec((SUBROWS, tr, sw), lambda i, f, te, tv: (0, i, 0))
    return pl.pallas_call(
        functools.partial(_experts_kernel, nf=nf),
        out_shape=jax.ShapeDtypeStruct((SUBROWS, p, sw), jnp.uint32),
        grid_spec=pltpu.PrefetchScalarGridSpec(
            num_scalar_prefetch=2,
            grid=(p // tr, nf),
            in_specs=[
                rows,
                pl.BlockSpec((1, d, tf), lambda i, f, te, tv: (te[i], 0, f)),
                pl.BlockSpec((1, d, tf), lambda i, f, te, tv: (te[i], 0, nf + f)),
                pl.BlockSpec((1, de, d), lambda i, f, te, tv: (te[i], 0, 0)),
            ],
            out_specs=rows,
            scratch_shapes=[pltpu.VMEM((tr, d), BF16), pltpu.VMEM((tr, de), BF16)],
        ),
        compiler_params=_params(("arbitrary", "arbitrary")),
        name="experts",
    )(tile_expert, tile_rows, xs, w_in, w_in, w_out)


def _combine_kernel(g_ref, h_ref, w_ref, o_ref):
    w = w_ref[...]
    y = [_unpack_row([g_ref[s, c] for c in range(SUBROWS)]) for s in range(TOP_K)]
    o_ref[...] = h_ref[...] + (w[:, 0:1] * y[0] + w[:, 1:2] * y[1])


def _combine(g, h, w, *, tc):
    n, d = h.shape
    sw = g.shape[3]
    tok = pl.BlockSpec((tc, d), lambda i: (i, 0))
    return pl.pallas_call(
        _combine_kernel,
        out_shape=jax.ShapeDtypeStruct((n, d), F32),
        grid=(n // tc,),
        in_specs=[pl.BlockSpec((TOP_K, SUBROWS, tc, sw), lambda i: (0, 0, i, 0)), tok,
                  pl.BlockSpec((tc, TOP_K), lambda i: (i, 0))],
        out_specs=tok,
        compiler_params=_params(("parallel",)),
        name="combine",
    )(g, h, w)


def _tiles(n, seq):
    def pick(limit, of):
        t = limit
        while of % t:
            t //= 2
        return t
    return dict(
        tm_a=pick(512, seq), tm_f=pick(512, n), tm_qkv=pick(1024, seq), blk=pick(256, seq),
        tm_o=pick(1024, n), tr=pick(1024, TOP_K * n), tc=pick(1024, n))


def kernel(x, a_norm_g, a_w_in, a_v_norm_g, a_w_spatial, a_b_spatial, a_w_out, f_norm_g, f_w_in, f_w_out, kv_norm_g, kv_w, kv_b_f, k_norm_g, b_norm_g, b_w_in, q_norm_g, b_w_out, m_norm_g, m_w_router, m_w_in, m_w_out):
    batch, seq, d = x.shape
    n = batch * seq
    nh = d // HEAD_DIM
    ne = m_w_router.shape[-1]
    assert a_w_in.shape[0] == 1 and b_w_in.shape[0] == 1 and f_w_in.shape[0] == 1 and m_w_in.shape[0] == 1
    assert seq % GMLP_CHUNK == 0 and d % LANES == 0 and (SUBROWS * n) % SC_WINDOW == 0
    t = _tiles(n, seq)
    row = lambda g: g.reshape(1, -1)

    h = x.reshape(n, d)
    h, (f_in,) = _mixer_a(
        h, row(a_norm_g[0]), a_w_in[0].astype(BF16), row(a_v_norm_g[0]), a_w_spatial[0],
        a_b_spatial[0].T, a_w_out[0].astype(BF16), (f_w_in[0],), tm=t["tm_a"])
    h, (b_in, b_out) = _swiglu(h, row(f_norm_g[0]), f_in, f_w_out[0].astype(BF16),
                               (b_w_in[0], b_w_out[0]), tm=t["tm_f"])

    head = jnp.arange(LANES, dtype=jnp.int32) // HEAD_DIM
    hsum = (head[:, None] == head[None, :]).astype(BF16)
    q, k, vt, sg, fp = _qkv(
        h, row(kv_norm_g), row(b_norm_g[0]), kv_w[:, :d].astype(BF16),
        kv_w[:, d:2 * d].T.astype(BF16), kv_w[:, 2 * d:].T.astype(BF16), kv_b_f.reshape(nh, 1),
        row(jnp.tile(k_norm_g, nh)), row(jnp.tile(q_norm_g[0], nh)), b_in, hsum,
        tm=t["tm_qkv"], batch=batch, seq=seq)
    og, (moe_w_in, moe_w_out) = _attention(q, k, vt, jnp.swapaxes(fp, 1, 2), sg,
                                           (m_w_in[0], m_w_out[0]), batch=batch, seq=seq, blk=t["blk"])

    h2, xp, meta, cnt = _out_router(og, h, b_out, row(m_norm_g[0]),
                                    m_w_router[0].T, tm=t["tm_o"])

    tr = t["tr"]
    n_tiles = TOP_K * n // tr + ne
    p = n_tiles * tr
    counts = cnt[:, 0].astype(jnp.int32)
    tiles_per_expert = (counts + tr - 1) // tr
    tile_end = jnp.cumsum(tiles_per_expert)
    tile_start = tile_end - tiles_per_expert
    expert_ids = jnp.arange(ne, dtype=jnp.int32)
    idx = meta[0:TOP_K].astype(jnp.int32)
    start_of = jnp.sum(jnp.where(idx[:, :, None] == expert_ids, tile_start * tr, 0), axis=-1)
    dest = start_of + meta[4:4 + TOP_K].astype(jnp.int32)
    tile_ids = jnp.arange(n_tiles, dtype=jnp.int32)
    tile_expert = jnp.minimum(
        jnp.sum((tile_ids[:, None] >= tile_end[None, :]).astype(jnp.int32), axis=1), ne - 1)
    mine = tile_expert[:, None] == expert_ids[None, :]
    tile_rows = jnp.clip(
        jnp.sum(jnp.where(mine, counts - (tile_ids[:, None] - tile_start) * tr, 0), axis=1), 0, tr)
    tile_rows = jnp.where(tile_ids < tile_end[-1], tile_rows, 0).astype(jnp.int32)

    sub = (jnp.arange(SUBROWS, dtype=jnp.int32) * p)[None, :, None]
    sub_dest = sub + dest[:, None, :]
    sw = xp.shape[2]
    xs = _sc_scatter(xp.reshape(SUBROWS * n, sw), sub_dest.reshape(TOP_K, SUBROWS * n), SUBROWS * p)
    eo = _experts(tile_expert, tile_rows, xs.reshape(SUBROWS, p, sw), moe_w_in, moe_w_out,
                  tr=tr, tf=m_w_out.shape[2] // EXPERT_STEPS)
    g = _sc_gather(eo.reshape(SUBROWS * p, sw), sub_dest.reshape(1, TOP_K * SUBROWS * n))
    out = _combine(g.reshape(TOP_K, SUBROWS, n, sw), h2, meta[2:2 + TOP_K].T, tc=t["tc"])
    return out.reshape(batch, seq, d)
```
